```python
import math
import jax, jax.numpy as jnp
from jax import lax
import numpy as np

D_MODEL = 1024
BATCH = 32
SEQ = 2048
DEPTH = 4

N_MIXERS = 2
N_GDN_LAYERS = (DEPTH + 1) // 2
N_DIFF_LAYERS = DEPTH // 2

GDN_HEADS = 8
GDN_DK = 128
GDN_DV = 128
GDN_QK = GDN_HEADS * GDN_DK
GDN_V = GDN_HEADS * GDN_DV
GDN_CONV = 4
GDN_CHUNK = 64
GDN_IN = 2 * GDN_QK + 2 * GDN_V + 2 * GDN_HEADS

DIFF_HEADS = 8
DIFF_HD = D_MODEL // (2 * DIFF_HEADS)
DIFF_IN = 3 * D_MODEL
DIFF_Q_BLOCK = 128
ROPE_THETA = 10000.0

FFN_HIDDEN = 2816
FFN_CONV = 3

DEEPNORM_ALPHA = (2.0 * DEPTH) ** 0.25
DEEPNORM_BETA = (8.0 * DEPTH) ** -0.25
LN_EPS = 1e-5
RMS_EPS = 1e-6

kernel_name = "hybrid_gdn_diffattn_convffn_deepnorm"


def layer_norm(x, g, b):
    xf = x.astype(jnp.float32)
    mu = jnp.mean(xf, -1, keepdims=True)
    var = jnp.mean(jnp.square(xf - mu), -1, keepdims=True)
    return ((xf - mu) * lax.rsqrt(var + LN_EPS)).astype(x.dtype) * g + b


def rms_norm(x, w):
    xf = x.astype(jnp.float32)
    return (xf * lax.rsqrt(jnp.mean(xf * xf, -1, keepdims=True) + RMS_EPS)).astype(x.dtype) * w


def l2_norm(x):
    xf = x.astype(jnp.float32)
    return xf * lax.rsqrt(jnp.sum(xf * xf, -1, keepdims=True) + RMS_EPS)


def causal_dwconv(x, w):
    K = w.shape[0]
    T = x.shape[1]
    xp = jnp.pad(x, ((0, 0), (K - 1, 0), (0, 0)))
    y = xp[:, K - 1:K - 1 + T] * w[K - 1]
    for j in range(K - 1):
        y = y + xp[:, j:j + T] * w[j]
    return y


def rope_tables(positions, head_dim):
    inv_freq = ROPE_THETA ** (-jnp.arange(0, head_dim, 2, dtype=jnp.float32) / head_dim)
    ang = positions.astype(jnp.float32)[..., None] * inv_freq
    return jnp.cos(ang)[:, :, None, None, :], jnp.sin(ang)[:, :, None, None, :]


def apply_rope(x, cos, sin):
    xf = x.astype(jnp.float32)
    x1, x2 = jnp.split(xf, 2, -1)
    return jnp.concatenate([x1 * cos - x2 * sin, x2 * cos + x1 * sin], -1).astype(x.dtype)


def gated_delta_rule_chunked(q, k, v, g, beta):
    f32 = jnp.float32
    B, T, H, dk = q.shape
    dv = v.shape[-1]
    C = GDN_CHUNK
    N = T // C

    def to_chunks(t):
        t = t.astype(f32).reshape((B, N, C, H) + t.shape[3:])
        return jnp.moveaxis(t, (1, 3), (0, 2))

    q = to_chunks(q) * (dk ** -0.5)
    k = to_chunks(k)
    v = to_chunks(v)
    beta = to_chunks(beta)
    gc = jnp.cumsum(to_chunks(g), axis=-1)
    causal = jnp.tril(jnp.ones((C, C), bool))
    strict = jnp.tril(jnp.ones((C, C), bool), -1)
    decay = jnp.exp(jnp.where(causal, gc[..., :, None] - gc[..., None, :], -jnp.inf))

    kb = k * beta[..., None]
    low = jnp.where(strict, jnp.einsum('nbhid,nbhjd->nbhij', kb, k) * decay, 0.0)
    a_mat = low + jnp.eye(C, dtype=f32)
    rhs = jnp.concatenate([v * beta[..., None], kb * jnp.exp(gc)[..., None]], -1)
    sol = lax.linalg.triangular_solve(a_mat, rhs, left_side=True, lower=True, unit_diagonal=True)
    u, w = sol[..., :dv], sol[..., dv:]

    attn_intra = jnp.where(causal, jnp.einsum('nbhid,nbhjd->nbhij', q, k) * decay, 0.0)
    q_dec = q * jnp.exp(gc)[..., None]
    g_last = gc[..., -1]
    k_dec = k * jnp.exp(g_last[..., None] - gc)[..., None]

    def step(S, xs):
        q_i, k_i, u_i, w_i, a_i, gl = xs
        v_new = u_i - jnp.einsum('bhcd,bhde->bhce', w_i, S)
        o = jnp.einsum('bhcd,bhde->bhce', q_i, S) + jnp.einsum('bhij,bhje->bhie', a_i, v_new)
        S = S * jnp.exp(gl)[..., None, None] + jnp.einsum('bhcd,bhce->bhde', k_i, v_new)
        return S, o

    S0 = jnp.zeros((B, H, dk, dv), f32)
    _, o = lax.scan(step, S0, (q_dec, k_dec, u, w, attn_intra, g_last))
    return jnp.moveaxis(o, (0, 2), (1, 3)).reshape(B, T, H, dv)


def gated_deltanet(x, w_in, conv_w, a_log, dt_bias, norm_w, w_out):
    B, T, _ = x.shape
    h = x @ w_in
    s0 = 2 * GDN_QK + GDN_V
    qkv, gate, a, b = jnp.split(h, [s0, s0 + GDN_V, s0 + GDN_V + GDN_HEADS], axis=-1)
    qkv = jax.nn.silu(causal_dwconv(qkv, conv_w))
    q, k, v = jnp.split(qkv, [GDN_QK, 2 * GDN_QK], axis=-1)
    q = l2_norm(q.reshape(B, T, GDN_HEADS, GDN_DK))
    k = l2_norm(k.reshape(B, T, GDN_HEADS, GDN_DK))
    v = v.reshape(B, T, GDN_HEADS, GDN_DV)
    beta = jax.nn.sigmoid(b.astype(jnp.float32))
    g = -jnp.exp(a_log.astype(jnp.float32)) * jax.nn.softplus(a.astype(jnp.float32) + dt_bias.astype(jnp.float32))
    o = gated_delta_rule_chunked(q, k, v, g, beta).astype(x.dtype)
    o = rms_norm(o, norm_w) * jax.nn.silu(gate.reshape(B, T, GDN_HEADS, GDN_DV))
    return o.reshape(B, T, GDN_V) @ w_out


def diff_attention(x, cos, sin, w_in, lam_q1, lam_k1, lam_q2, lam_k2, subln_w, w_out, lambda_init):
    B, T, _ = x.shape
    q, k, v = jnp.split(x @ w_in, 3, axis=-1)
    q = apply_rope(q.reshape(B, T, DIFF_HEADS, 2, DIFF_HD), cos, sin)
    k = apply_rope(k.reshape(B, T, DIFF_HEADS, 2, DIFF_HD), cos, sin)
    v = v.reshape(B, T, DIFF_HEADS, 2 * DIFF_HD)
    f32 = jnp.float32
    lam = (jnp.exp(jnp.sum(lam_q1.astype(f32) * lam_k1.astype(f32)))
           - jnp.exp(jnp.sum(lam_q2.astype(f32) * lam_k2.astype(f32))) + lambda_init)
    nb = T // DIFF_Q_BLOCK
    q_blocks = jnp.moveaxis(q.reshape(B, nb, DIFF_Q_BLOCK, DIFF_HEADS, 2, DIFF_HD), 1, 0)
    kpos = jnp.arange(T)
    scale = DIFF_HD ** -0.5

    def block(args):
        q_blk, i = args
        s = jnp.einsum('bqhcd,bkhcd->bhcqk', q_blk, k, preferred_element_type=f32) * scale
        qpos = i * DIFF_Q_BLOCK + jnp.arange(DIFF_Q_BLOCK)
        s = jnp.where(kpos[None, :] <= qpos[:, None], s, -jnp.inf)
        p = jax.nn.softmax(s, axis=-1)
        p = p[:, :, 0] - lam * p[:, :, 1]
        return jnp.einsum('bhqk,bkhe->bqhe', p.astype(v.dtype), v)

    o = lax.map(block, (q_blocks, jnp.arange(nb)))
    o = jnp.moveaxis(o, 0, 1).reshape(B, T, DIFF_HEADS, 2 * DIFF_HD)
    o = rms_norm(o, subln_w) * (1.0 - lambda_init)
    return o.reshape(B, T, D_MODEL) @ w_out


def conv_ffn(x, w_up, conv_w, conv_b, w_down):
    h = causal_dwconv(x @ w_up, conv_w) + conv_b
    a, b = jnp.split(h, 2, axis=-1)
    return (jax.nn.silu(a) * b) @ w_down


def setup_inputs(seed: int = 0) -> dict:
    key = jax.random.key(seed)
    ks = jax.random.split(key, 24)
    f32 = jnp.float32
    nrm = lambda k, shape, s: jax.random.normal(k, shape, f32) * s
    x = jax.random.normal(ks[0], (BATCH, SEQ, D_MODEL), f32)
    offset = jax.random.randint(ks[1], (BATCH, 1), 0, 4096, dtype=jnp.int32)
    positions = offset + jnp.arange(SEQ, dtype=jnp.int32)[None, :]
    ng, nd = N_GDN_LAYERS, N_DIFF_LAYERS
    conv_ch = 2 * GDN_QK + GDN_V
    dt = jnp.exp(jax.random.uniform(ks[2], (ng, GDN_HEADS), f32, math.log(1e-3), math.log(1e-1)))
    return {
        "x": x,
        "positions": positions,
        "gdn_w_in": nrm(ks[3], (ng, D_MODEL, GDN_IN), D_MODEL ** -0.5),
        "gdn_conv_w": nrm(ks[4], (ng, GDN_CONV, conv_ch), GDN_CONV ** -0.5),
        "gdn_a_log": jnp.log(jax.random.uniform(ks[5], (ng, GDN_HEADS), f32, 1.0, 16.0)),
        "gdn_dt_bias": dt + jnp.log(-jnp.expm1(-dt)),
        "gdn_norm_w": 1.0 + nrm(ks[6], (ng, GDN_DV), 0.02),
        "gdn_w_out": nrm(ks[7], (ng, GDN_V, D_MODEL), GDN_V ** -0.5 * DEEPNORM_BETA),
        "diff_w_in": nrm(ks[8], (nd, D_MODEL, DIFF_IN), D_MODEL ** -0.5),
        "diff_lam_q1": nrm(ks[9], (nd, DIFF_HD), 0.1),
        "diff_lam_k1": nrm(ks[10], (nd, DIFF_HD), 0.1),
        "diff_lam_q2": nrm(ks[11], (nd, DIFF_HD), 0.1),
        "diff_lam_k2": nrm(ks[12], (nd, DIFF_HD), 0.1),
        "diff_subln_w": 1.0 + nrm(ks[13], (nd, 2 * DIFF_HD), 0.02),
        "diff_w_out": nrm(ks[14], (nd, D_MODEL, D_MODEL), D_MODEL ** -0.5 * DEEPNORM_BETA),
        "ffn_w_up": nrm(ks[15], (DEPTH, D_MODEL, 2 * FFN_HIDDEN), D_MODEL ** -0.5),
        "ffn_conv_w": nrm(ks[16], (DEPTH, FFN_CONV, 2 * FFN_HIDDEN), FFN_CONV ** -0.5),
        "ffn_conv_b": nrm(ks[17], (DEPTH, 2 * FFN_HIDDEN), 0.01),
        "ffn_w_down": nrm(ks[18], (DEPTH, FFN_HIDDEN, D_MODEL), FFN_HIDDEN ** -0.5 * DEEPNORM_BETA),
        "ln_mix_g": 1.0 + nrm(ks[19], (DEPTH, D_MODEL), 0.02),
        "ln_mix_b": nrm(ks[20], (DEPTH, D_MODEL), 0.01),
        "ln_ffn_g": 1.0 + nrm(ks[21], (DEPTH, D_MODEL), 0.02),
        "ln_ffn_b": nrm(ks[22], (DEPTH, D_MODEL), 0.01),
    }


def reference(x, positions, gdn_w_in, gdn_conv_w, gdn_a_log, gdn_dt_bias, gdn_norm_w, gdn_w_out,
              diff_w_in, diff_lam_q1, diff_lam_k1, diff_lam_q2, diff_lam_k2, diff_subln_w, diff_w_out,
              ffn_w_up, ffn_conv_w, ffn_conv_b, ffn_w_down, ln_mix_g, ln_mix_b, ln_ffn_g, ln_ffn_b):
    cos, sin = rope_tables(positions, DIFF_HD)
    for i in range(DEPTH):
        j = i // N_MIXERS
        if i % N_MIXERS == 0:
            m = gated_deltanet(x, gdn_w_in[j], gdn_conv_w[j], gdn_a_log[j], gdn_dt_bias[j],
                               gdn_norm_w[j], gdn_w_out[j])
        else:
            lambda_init = 0.8 - 0.6 * math.exp(-0.3 * i)
            m = diff_attention(x, cos, sin, diff_w_in[j], diff_lam_q1[j], diff_lam_k1[j],
                               diff_lam_q2[j], diff_lam_k2[j], diff_subln_w[j], diff_w_out[j],
                               lambda_init)
        x = layer_norm(DEEPNORM_ALPHA * x + m, ln_mix_g[i], ln_mix_b[i])
        f = conv_ffn(x, ffn_w_up[i], ffn_conv_w[i], ffn_conv_b[i], ffn_w_down[i])
        x = layer_norm(DEEPNORM_ALPHA * x + f, ln_ffn_g[i], ln_ffn_b[i])
    return x
```

```python
import functools
import math

import jax
import jax.numpy as jnp
from jax import lax
from jax.experimental import pallas as pl
from jax.experimental.pallas import tpu as pltpu

F32 = jnp.float32
BF16 = jnp.bfloat16

DEPTH = 4
GDN_HEADS = 8
GDN_DK = 128
GDN_DV = 128
GDN_CONV = 4
GDN_CHUNK = 64
DIFF_HEADS = 8
DIFF_HD = 64
ROPE_THETA = 10000.0
FFN_CONV = 3
DEEPNORM_ALPHA = (2.0 * DEPTH) ** 0.25
LN_EPS = 1e-5
RMS_EPS = 1e-6

LANES = 128
SUBLANES = 8
MXU_N = 256
VMEM_LIMIT = 56 * 1024 * 1024

ROW_TILE = 512
ATTN_BLOCK = 512
GDN_STEP = 128


def _row_tile(t):
    return ROW_TILE if t % ROW_TILE == 0 else t


def _sigmoid(x):
    return 1.0 / (1.0 + jnp.exp(-x))


def _silu(x):
    return x * _sigmoid(x)


def _dot(a, b):
    return jnp.dot(a, b, preferred_element_type=F32)


def _dot_nt(a, b):
    return lax.dot_general(a, b, (((1,), (1,)), ((), ())), preferred_element_type=F32)


def _dot_tn(a, b):
    return lax.dot_general(a, b, (((0,), (0,)), ((), ())), preferred_element_type=F32)


def _shift_rows(h, s):
    return pltpu.roll(h, s, 0)


def _causal_conv(h, cw, width):
    y = h * cw[width - 1:width, :]
    for s in range(1, width):
        y = y + _shift_rows(h, s) * cw[width - 1 - s:width - s, :]
    return y[SUBLANES:, :]


def _layer_norm_rows(z, g, b):
    mu = jnp.mean(z, axis=-1, keepdims=True)
    zc = z - mu
    var = jnp.mean(zc * zc, axis=-1, keepdims=True)
    return zc * lax.rsqrt(var + LN_EPS) * g + b


def _stage_rows(x_ref, xh_ref, xb_ref, seq_tiles):
    first = (pl.program_id(0) % seq_tiles) == 0
    halo = jnp.where(first, 0.0, xh_ref[...])
    xb_ref[0:SUBLANES, :] = halo.astype(BF16)
    xb_ref[SUBLANES:, :] = x_ref[...].astype(BF16)


def _halo_spec(tm, d):
    return pl.BlockSpec((SUBLANES, d), lambda i: (jnp.maximum(i * (tm // SUBLANES) - 1, 0), 0))


def _const_spec(shape):
    return pl.BlockSpec(shape, lambda *_: (0,) * len(shape))


def _params(n_axes):
    return pltpu.CompilerParams(
        dimension_semantics=("arbitrary",) * n_axes, vmem_limit_bytes=VMEM_LIMIT)


def _rope_kernel(pos_ref, inv_ref, c_ref, s1_ref, s2_ref):
    ang = pos_ref[...].astype(F32) * inv_ref[...]
    cos = jnp.cos(ang)
    sin = jnp.sin(ang)
    lane = lax.broadcasted_iota(jnp.int32, ang.shape, 1)
    lower = (lane % DIFF_HD) < (DIFF_HD // 2)
    c_ref[...] = cos
    s1_ref[...] = jnp.where(lower, -sin, 0.0)
    s2_ref[...] = jnp.where(lower, 0.0, sin)


def _rope_tables(positions, tm):
    n = positions.size
    inv_freq = ROPE_THETA ** (-jnp.arange(0, DIFF_HD, 2, dtype=F32) / DIFF_HD)
    inv = jnp.tile(inv_freq, LANES // (DIFF_HD // 2)).reshape(1, LANES)
    pos = positions.reshape(n, 1)
    out = jax.ShapeDtypeStruct((n, LANES), F32)
    return pl.pallas_call(
        _rope_kernel,
        grid=(n // tm,),
        in_specs=[pl.BlockSpec((tm, 1), lambda i: (i, 0)), _const_spec((1, LANES))],
        out_specs=[pl.BlockSpec((tm, LANES), lambda i: (i, 0))] * 3,
        out_shape=[out] * 3,
        compiler_params=_params(1),
        name="rope_tables",
    )(pos, inv)


def _gdn_in_kernel(x_ref, xh_ref, w_ref, wab_ref, cw_ref, alog_ref, dtb_ref,
                   qkv_ref, gate_ref, gb_ref, xb_ref, *, seq_tiles, n_qk, n_v, n_gate):
    _stage_rows(x_ref, xh_ref, xb_ref, seq_tiles)
    xb = xb_ref[...]
    for c in range(n_qk + n_v):
        cols = slice(c * MXU_N, (c + 1) * MXU_N)
        h = _dot(xb, w_ref[:, cols])
        y = _silu(_causal_conv(h, cw_ref[:, cols], GDN_CONV))
        if c < n_qk:
            parts = []
            for j in range(MXU_N // GDN_DK):
                yh = y[:, j * GDN_DK:(j + 1) * GDN_DK]
                ss = jnp.sum(yh * yh, axis=-1, keepdims=True)
                parts.append(yh * lax.rsqrt(ss + RMS_EPS))
            y = jnp.concatenate(parts, axis=1)
        qkv_ref[:, cols] = y.astype(BF16)
    xt = xb[SUBLANES:, :]
    base = (n_qk + n_v) * MXU_N
    for c in range(n_gate):
        gate_ref[:, c * MXU_N:(c + 1) * MXU_N] = _dot(
            xt, w_ref[:, base + c * MXU_N:base + (c + 1) * MXU_N]).astype(BF16)
    hab = _dot(xt, wab_ref[...])
    z = hab + dtb_ref[...]
    softplus = jnp.maximum(z, 0.0) + jnp.log(1.0 + jnp.exp(-jnp.abs(z)))
    g = -jnp.exp(alog_ref[...]) * softplus
    lane = lax.broadcasted_iota(jnp.int32, hab.shape, 1)
    gb_ref[...] = jnp.where(lane < GDN_HEADS, g, _sigmoid(hab))


def _gdn_in(x2, w_in, conv_w, a_log, dt_bias, tm, seq_tiles):
    n, d = x2.shape
    qk_w = 2 * GDN_HEADS * GDN_DK
    v_w = GDN_HEADS * GDN_DV
    main = qk_w + 2 * v_w
    w_main = w_in[:, :main].astype(BF16)
    w_ab = jnp.pad(w_in[:, main:], ((0, 0), (0, LANES - 2 * GDN_HEADS))).astype(BF16)
    pad = (0, LANES - GDN_HEADS)
    alog = jnp.pad(a_log.astype(F32), pad).reshape(1, LANES)
    dtb = jnp.pad(dt_bias.astype(F32), pad).reshape(1, LANES)
    kern = functools.partial(_gdn_in_kernel, seq_tiles=seq_tiles, n_qk=qk_w // MXU_N,
                             n_v=v_w // MXU_N, n_gate=v_w // MXU_N)
    row = lambda w: pl.BlockSpec((tm, w), lambda i: (i, 0))
    return pl.pallas_call(
        kern,
        grid=(n // tm,),
        in_specs=[row(d), _halo_spec(tm, d), _const_spec((d, main)), _const_spec((d, LANES)),
                  _const_spec((GDN_CONV, qk_w + v_w)), _const_spec((1, LANES)),
                  _const_spec((1, LANES))],
        out_specs=[row(qk_w + v_w), row(v_w), row(LANES)],
        out_shape=[jax.ShapeDtypeStruct((n, qk_w + v_w), BF16),
                   jax.ShapeDtypeStruct((n, v_w), BF16),
                   jax.ShapeDtypeStruct((n, LANES), F32)],
        scratch_shapes=[pltpu.VMEM((SUBLANES + tm, d), BF16)],
        compiler_params=_params(1),
        name="gdn_in",
    )(x2, x2, w_main, w_ab, conv_w.astype(F32), alog, dtb)


def _unit_lower_inverse(a, eye, blk16, off32, off64):
    n = jnp.where(blk16, -a, 0.0)
    p = eye + n
    for _ in range(3):
        nb = n.astype(BF16)
        n = _dot(nb, nb)
        p = p + _dot(p.astype(BF16), n.astype(BF16))
    for off in (off32, off64):
        pb = p.astype(BF16)
        t = _dot(jnp.where(off, a, 0.0).astype(BF16), pb)
        p = p - _dot(pb, t.astype(BF16))
    return p


def _gdn_chunk_kernel(q_ref, k_ref, v_ref, gate_ref, g_ref, gt_ref, beta_ref, nw_ref,
                      o_ref, s_ref, *, n_sub):
    c = GDN_CHUNK

    @pl.when(pl.program_id(1) == 0)
    def _():
        s_ref[...] = jnp.zeros_like(s_ref)

    step = n_sub * c
    row = lax.broadcasted_iota(jnp.int32, (step, step), 0)
    col = lax.broadcasted_iota(jnp.int32, (step, step), 1)
    same = (row // c) == (col // c)
    tri = jnp.where(same & (col <= row), 1.0, 0.0)
    tri_t = jnp.where(same & (row <= col), 1.0, 0.0)
    gc_cols = jnp.dot(tri, g_ref[0], precision=lax.Precision.HIGHEST, preferred_element_type=F32)
    gc_rows = jnp.dot(gt_ref[0], tri_t, precision=lax.Precision.HIGHEST, preferred_element_type=F32)
    beta = beta_ref[0]

    r = lax.broadcasted_iota(jnp.int32, (c, c), 0)
    s = lax.broadcasted_iota(jnp.int32, (c, c), 1)
    causal = s <= r
    strict = s < r
    eye = jnp.where(r == s, 1.0, 0.0)
    blk16 = (r // 16) == (s // 16)
    off32 = ((r // 32) == (s // 32)) & ((r // 16) != (s // 16))
    off64 = (r // 32) != (s // 32)
    scale = GDN_DK ** -0.5
    nw = nw_ref[...]

    for sub in range(n_sub):
        rows = slice(sub * c, (sub + 1) * c)
        for h in range(GDN_HEADS):
            cols = slice(h * GDN_DK, (h + 1) * GDN_DK)
            qh = q_ref[0, rows, cols]
            kh = k_ref[0, rows, cols]
            vh = v_ref[0, rows, cols]
            gc_col = gc_cols[rows, h:h + 1]
            gc_row = gc_rows[h:h + 1, rows]
            b_col = beta[rows, h:h + 1]
            diff = jnp.where(causal, gc_col - gc_row, 0.0)
            decay = jnp.where(causal, jnp.exp(diff), 0.0)
            kk = _dot_nt(kh, kh)
            qk = _dot_nt(qh, kh) * scale
            a = jnp.where(strict, kk * decay * b_col, 0.0)
            attn = qk * decay
            tm_inv = _unit_lower_inverse(a, eye, blk16, off32, off64)
            egc = jnp.exp(gc_col)
            kf = kh.astype(F32)
            rhs = jnp.concatenate([vh.astype(F32) * b_col, kf * (b_col * egc)], axis=1)
            sol = _dot(tm_inv.astype(BF16), rhs.astype(BF16))
            u = sol[:, :GDN_DV]
            w = sol[:, GDN_DV:]
            st = s_ref[h]
            sb = st.astype(BF16)
            v_new = u - _dot(w.astype(BF16), sb)
            vb = v_new.astype(BF16)
            qd = (qh.astype(F32) * (egc * scale)).astype(BF16)
            o = _dot(qd, sb) + _dot(attn.astype(BF16), vb)
            g_last = gc_col[c - 1:c, :]
            kd = (kf * jnp.exp(g_last - gc_col)).astype(BF16)
            s_ref[h] = st * jnp.exp(g_last) + _dot_tn(kd, vb)
            ms = jnp.mean(o * o, axis=-1, keepdims=True)
            on = o * lax.rsqrt(ms + RMS_EPS) * nw
            gt = gate_ref[0, rows, cols].astype(F32)
            o_ref[0, rows, cols] = (on * _silu(gt)).astype(BF16)


def _gdn_chunk(qkv, gate, g, beta, norm_w, b, t):
    hq = GDN_HEADS * GDN_DK
    hv = GDN_HEADS * GDN_DV
    step = GDN_STEP if t % GDN_STEP == 0 else t
    qkv3 = qkv.reshape(b, t, 2 * hq + hv)
    gate3 = gate.reshape(b, t, hv)
    g3 = g.reshape(b, t, GDN_HEADS)
    gt3 = jnp.swapaxes(g3, 1, 2)
    beta3 = beta.reshape(b, t, GDN_HEADS)
    kern = functools.partial(_gdn_chunk_kernel, n_sub=step // GDN_CHUNK)
    col = lambda j: pl.BlockSpec((1, step, hq), lambda bi, ci: (bi, ci, j))
    small = pl.BlockSpec((1, step, GDN_HEADS), lambda bi, ci: (bi, ci, 0))
    out = pl.pallas_call(
        kern,
        grid=(b, t // step),
        in_specs=[col(0), col(1), col(2), col(0), small,
                  pl.BlockSpec((1, GDN_HEADS, step), lambda bi, ci: (bi, 0, ci)), small,
                  _const_spec((1, GDN_DV))],
        out_specs=col(0),
        out_shape=jax.ShapeDtypeStruct((b, t, hv), BF16),
        scratch_shapes=[pltpu.VMEM((GDN_HEADS, GDN_DK, GDN_DV), F32)],
        compiler_params=_params(2),
        name="gdn_chunk",
    )(qkv3, qkv3, qkv3, gate3, g3, gt3, beta3, norm_w.astype(F32).reshape(1, GDN_DV))
    return out.reshape(b * t, hv)


def _proj_ln_kernel(y_ref, w_ref, x_ref, g_ref, b_ref, o_ref):
    m = _dot(y_ref[...], w_ref[...])
    z = DEEPNORM_ALPHA * x_ref[...] + m
    o_ref[...] = _layer_norm_rows(z, g_ref[...], b_ref[...])


def _proj_ln(y, w, x2, g, b, tm):
    n, d = x2.shape
    k = y.shape[1]
    row = lambda w_: pl.BlockSpec((tm, w_), lambda i: (i, 0))
    return pl.pallas_call(
        _proj_ln_kernel,
        grid=(n // tm,),
        in_specs=[row(k), _const_spec((k, d)), row(d), _const_spec((1, d)), _const_spec((1, d))],
        out_specs=row(d),
        out_shape=jax.ShapeDtypeStruct((n, d), F32),
        compiler_params=_params(1),
        name="proj_ln",
    )(y, w.astype(BF16), x2, g.astype(F32).reshape(1, d), b.astype(F32).reshape(1, d))


def _ffn_kernel(x_ref, xh_ref, wup_ref, cw_ref, cb_ref, wdn_ref, g_ref, b_ref, o_ref,
                xb_ref, act_ref, *, seq_tiles, hidden):
    _stage_rows(x_ref, xh_ref, xb_ref, seq_tiles)
    xb = xb_ref[...]
    for c in range(hidden // MXU_N):
        halves = []
        for base in (0, hidden):
            cols = slice(base + c * MXU_N, base + (c + 1) * MXU_N)
            h = _dot(xb, wup_ref[:, cols])
            halves.append(_causal_conv(h, cw_ref[:, cols], FFN_CONV) + cb_ref[:, cols])
        act_ref[:, c * MXU_N:(c + 1) * MXU_N] = (_silu(halves[0]) * halves[1]).astype(BF16)
    f = _dot(act_ref[...], wdn_ref[...])
    z = DEEPNORM_ALPHA * x_ref[...] + f
    o_ref[...] = _layer_norm_rows(z, g_ref[...], b_ref[...])


def _ffn(x2, w_up, conv_w, conv_b, w_down, g, b, tm, seq_tiles):
    n, d = x2.shape
    hidden = w_down.shape[0]
    kern = functools.partial(_ffn_kernel, seq_tiles=seq_tiles, hidden=hidden)
    row = pl.BlockSpec((tm, d), lambda i: (i, 0))
    single = lambda shape: pl.BlockSpec(shape, lambda *_: (0,) * len(shape),
                                        pipeline_mode=pl.Buffered(1))
    return pl.pallas_call(
        kern,
        grid=(n // tm,),
        in_specs=[row, _halo_spec(tm, d), single((d, 2 * hidden)),
                  _const_spec((FFN_CONV, 2 * hidden)), _const_spec((1, 2 * hidden)),
                  single((hidden, d)), _const_spec((1, d)), _const_spec((1, d))],
        out_specs=row,
        out_shape=jax.ShapeDtypeStruct((n, d), F32),
        scratch_shapes=[pltpu.VMEM((SUBLANES + tm, d), BF16), pltpu.VMEM((tm, hidden), BF16)],
        compiler_params=_params(1),
        name="conv_ffn",
    )(x2, x2, w_up.astype(BF16), conv_w.astype(F32), conv_b.astype(F32).reshape(1, 2 * hidden),
      w_down.astype(BF16), g.astype(F32).reshape(1, d), b.astype(F32).reshape(1, d))


def _diff_in_kernel(x_ref, w_ref, c_ref, s1_ref, s2_ref, qk_ref, v_ref, *, n_qk, n_v, q_cols):
    xb = x_ref[...].astype(BF16)
    cos = c_ref[...]
    s1 = s1_ref[...]
    s2 = s2_ref[...]
    half = DIFF_HD // 2
    for c in range(n_qk):
        h = _dot(xb, w_ref[:, c * MXU_N:(c + 1) * MXU_N])
        for j in range(MXU_N // LANES):
            hs = h[:, j * LANES:(j + 1) * LANES]
            rot = hs * cos + pltpu.roll(hs, LANES - half, 1) * s1 + pltpu.roll(hs, half, 1) * s2
            lo = c * MXU_N + j * LANES
            if lo < q_cols:
                rot = rot * (DIFF_HD ** -0.5)
            qk_ref[:, lo:lo + LANES] = rot.astype(BF16)
    base = n_qk * MXU_N
    for c in range(n_v):
        v_ref[:, c * MXU_N:(c + 1) * MXU_N] = _dot(
            xb, w_ref[:, base + c * MXU_N:base + (c + 1) * MXU_N]).astype(BF16)


def _diff_in(x2, w_in, tables, tm):
    n, d = x2.shape
    hw = DIFF_HEADS * 2 * DIFF_HD
    kern = functools.partial(_diff_in_kernel, n_qk=2 * hw // MXU_N, n_v=hw // MXU_N, q_cols=hw)
    row = lambda w: pl.BlockSpec((tm, w), lambda i: (i, 0))
    return pl.pallas_call(
        kern,
        grid=(n // tm,),
        in_specs=[row(d), _const_spec((d, 3 * hw)), row(LANES), row(LANES), row(LANES)],
        out_specs=[row(2 * hw), row(hw)],
        out_shape=[jax.ShapeDtypeStruct((n, 2 * hw), BF16), jax.ShapeDtypeStruct((n, hw), BF16)],
        compiler_params=_params(1),
        name="diff_in",
    )(x2, w_in.astype(BF16), *tables)


def _diff_attn_kernel(q_ref, k_ref, v_ref, lq1_ref, lk1_ref, lq2_ref, lk2_ref, sw_ref, o_ref,
                      *, t, blk, lambda_init):
    lam = (jnp.exp(jnp.sum(lq1_ref[...] * lk1_ref[...], axis=-1, keepdims=True))
           - jnp.exp(jnp.sum(lq2_ref[...] * lk2_ref[...], axis=-1, keepdims=True))
           + lambda_init)
    lane = lax.broadcasted_iota(jnp.int32, (blk, 2 * DIFF_HD), 1)
    first_map = lane < DIFF_HD
    r = lax.broadcasted_iota(jnp.int32, (2 * blk, blk), 0)
    s = lax.broadcasted_iota(jnp.int32, (2 * blk, blk), 1)
    diag_ok = s <= (r % blk)
    sw = sw_ref[...]
    for qi in range(t // blk):
        q = q_ref[0, qi * blk:(qi + 1) * blk, :]
        zero = jnp.zeros_like(q)
        qs = jnp.concatenate([jnp.where(first_map, q, zero), jnp.where(first_map, zero, q)], axis=0)
        m = l = acc = None
        for kj in range(qi + 1):
            kb = k_ref[0, kj * blk:(kj + 1) * blk, :]
            vb = v_ref[0, kj * blk:(kj + 1) * blk, :]
            sc = _dot_nt(qs, kb)
            if kj == qi:
                sc = jnp.where(diag_ok, sc, -jnp.inf)
            m_blk = jnp.max(sc, axis=-1, keepdims=True)
            if kj == 0:
                m = m_blk
                p = jnp.exp(sc - m)
                l = jnp.sum(p, axis=-1, keepdims=True)
                acc = _dot(p.astype(BF16), vb)
            else:
                m_new = jnp.maximum(m, m_blk)
                alpha = jnp.exp(m - m_new)
                p = jnp.exp(sc - m_new)
                l = alpha * l + jnp.sum(p, axis=-1, keepdims=True)
                acc = alpha * acc + _dot(p.astype(BF16), vb)
                m = m_new
        on = acc / l
        o = on[:blk] - lam * on[blk:]
        ms = jnp.mean(o * o, axis=-1, keepdims=True)
        o = o * lax.rsqrt(ms + RMS_EPS) * sw * (1.0 - lambda_init)
        o_ref[0, qi * blk:(qi + 1) * blk, :] = o.astype(BF16)


def _diff_attn(qk, v, lam_q1, lam_k1, lam_q2, lam_k2, subln_w, lambda_init, b, t):
    hw = DIFF_HEADS * 2 * DIFF_HD
    hd2 = 2 * DIFF_HD
    blk = ATTN_BLOCK if t % ATTN_BLOCK == 0 else t
    qk3 = qk.reshape(b, t, 2 * hw)
    v3 = v.reshape(b, t, hw)
    kern = functools.partial(_diff_attn_kernel, t=t, blk=blk, lambda_init=lambda_init)
    head = lambda off: pl.BlockSpec((1, t, hd2), lambda bi, hi: (bi, 0, hi + off))
    vec = lambda a: a.astype(F32).reshape(1, -1)
    out = pl.pallas_call(
        kern,
        grid=(b, DIFF_HEADS),
        in_specs=[head(0), head(DIFF_HEADS), head(0)] + [_const_spec((1, DIFF_HD))] * 4
                 + [_const_spec((1, hd2))],
        out_specs=head(0),
        out_shape=jax.ShapeDtypeStruct((b, t, hw), BF16),
        compiler_params=_params(2),
        name="diff_attn",
    )(qk3, qk3, v3, vec(lam_q1), vec(lam_k1), vec(lam_q2), vec(lam_k2), vec(subln_w))
    return out.reshape(b * t, hw)


def kernel(x, positions, gdn_w_in, gdn_conv_w, gdn_a_log, gdn_dt_bias, gdn_norm_w, gdn_w_out,
           diff_w_in, diff_lam_q1, diff_lam_k1, diff_lam_q2, diff_lam_k2, diff_subln_w, diff_w_out,
           ffn_w_up, ffn_conv_w, ffn_conv_b, ffn_w_down, ln_mix_g, ln_mix_b, ln_ffn_g, ln_ffn_b):
    b, t, d = x.shape
    tm = _row_tile(t)
    seq_tiles = t // tm
    x2 = x.reshape(b * t, d).astype(F32)
    tables = _rope_tables(positions, tm)
    for i in range(DEPTH):
        j = i // 2
        if i % 2 == 0:
            qkv, gate, gb = _gdn_in(x2, gdn_w_in[j], gdn_conv_w[j], gdn_a_log[j], gdn_dt_bias[j],
                                    tm, seq_tiles)
            g = gb[:, :GDN_HEADS]
            beta = gb[:, GDN_HEADS:2 * GDN_HEADS]
            mixed = _gdn_chunk(qkv, gate, g, beta, gdn_norm_w[j], b, t)
            w_out = gdn_w_out[j]
        else:
            lambda_init = 0.8 - 0.6 * math.exp(-0.3 * i)
            qk, v = _diff_in(x2, diff_w_in[j], tables, tm)
            mixed = _diff_attn(qk, v, diff_lam_q1[j], diff_lam_k1[j], diff_lam_q2[j], diff_lam_k2[j],
                               diff_subln_w[j], lambda_init, b, t)
            w_out = diff_w_out[j]
        x2 = _proj_ln(mixed, w_out, x2, ln_mix_g[i], ln_mix_b[i], tm)
        x2 = _ffn(x2, ffn_w_up[i], ffn_conv_w[i], ffn_conv_b[i], ffn_w_down[i],
                  ln_ffn_g[i], ln_ffn_b[i], tm, seq_tiles)
    return x2.reshape(b, t, d)
```

```python
import functools
import math

import jax
import jax.numpy as jnp
from jax import lax
from jax.experimental import pallas as pl
from jax.experimental.pallas import tpu as pltpu

F32 = jnp.float32
BF16 = jnp.bfloat16

DEPTH = 4
GDN_HEADS = 8
GDN_DK = 128
GDN_DV = 128
GDN_CONV = 4
GDN_CHUNK = 64
DIFF_HEADS = 8
DIFF_HD = 64
ROPE_THETA = 10000.0
FFN_CONV = 3
DEEPNORM_ALPHA = (2.0 * DEPTH) ** 0.25
LN_EPS = 1e-5
RMS_EPS = 1e-6

LANES = 128
SUBLANES = 8
MXU_N = 256
VMEM_LIMIT = 56 * 1024 * 1024

ROW_TILE = 512
ATTN_BLOCK = 512
GDN_STEP = 256


def _row_tile(t):
    return ROW_TILE if t % ROW_TILE == 0 else t


def _sigmoid(x):
    return 1.0 / (1.0 + jnp.exp(-x))


def _silu(x):
    return x * _sigmoid(x)


def _dot(a, b):
    return jnp.dot(a, b, preferred_element_type=F32)


def _dot_nt(a, b):
    return lax.dot_general(a, b, (((1,), (1,)), ((), ())), preferred_element_type=F32)


def _dot_tn(a, b):
    return lax.dot_general(a, b, (((0,), (0,)), ((), ())), preferred_element_type=F32)


def _shift_rows(h, s):
    return pltpu.roll(h, s, 0)


def _causal_conv(h, cw, width):
    y = h * cw[width - 1:width, :]
    for s in range(1, width):
        y = y + _shift_rows(h, s) * cw[width - 1 - s:width - s, :]
    return y[SUBLANES:, :]


def _layer_norm_rows(z, g, b):
    mu = jnp.mean(z, axis=-1, keepdims=True)
    zc = z - mu
    var = jnp.mean(zc * zc, axis=-1, keepdims=True)
    return zc * lax.rsqrt(var + LN_EPS) * g + b


def _stage_rows(x_ref, xh_ref, xb_ref, seq_tiles):
    first = (pl.program_id(0) % seq_tiles) == 0
    halo = jnp.where(first, 0.0, xh_ref[...])
    xb_ref[0:SUBLANES, :] = halo.astype(BF16)
    xb_ref[SUBLANES:, :] = x_ref[...].astype(BF16)


def _halo_spec(tm, d):
    return pl.BlockSpec((SUBLANES, d), lambda i: (jnp.maximum(i * (tm // SUBLANES) - 1, 0), 0))


def _const_spec(shape):
    return pl.BlockSpec(shape, lambda *_: (0,) * len(shape))


def _params(n_axes):
    return pltpu.CompilerParams(
        dimension_semantics=("arbitrary",) * n_axes, vmem_limit_bytes=VMEM_LIMIT)


def _rope_kernel(pos_ref, inv_ref, c_ref, s1_ref, s2_ref):
    ang = pos_ref[...].astype(F32) * inv_ref[...]
    cos = jnp.cos(ang)
    sin = jnp.sin(ang)
    lane = lax.broadcasted_iota(jnp.int32, ang.shape, 1)
    lower = (lane % DIFF_HD) < (DIFF_HD // 2)
    c_ref[...] = cos
    s1_ref[...] = jnp.where(lower, -sin, 0.0)
    s2_ref[...] = jnp.where(lower, 0.0, sin)


def _rope_tables(positions, tm):
    n = positions.size
    inv_freq = ROPE_THETA ** (-jnp.arange(0, DIFF_HD, 2, dtype=F32) / DIFF_HD)
    inv = jnp.tile(inv_freq, LANES // (DIFF_HD // 2)).reshape(1, LANES)
    pos = positions.reshape(n, 1)
    out = jax.ShapeDtypeStruct((n, LANES), F32)
    return pl.pallas_call(
        _rope_kernel,
        grid=(n // tm,),
        in_specs=[pl.BlockSpec((tm, 1), lambda i: (i, 0)), _const_spec((1, LANES))],
        out_specs=[pl.BlockSpec((tm, LANES), lambda i: (i, 0))] * 3,
        out_shape=[out] * 3,
        compiler_params=_params(1),
        name="rope_tables",
    )(pos, inv)


def _gdn_in_kernel(x_ref, xh_ref, w_ref, wab_ref, cw_ref, alog_ref, dtb_ref,
                   qkv_ref, gate_ref, gb_ref, xb_ref, *, seq_tiles, n_qk, n_v, n_gate):
    _stage_rows(x_ref, xh_ref, xb_ref, seq_tiles)
    xb = xb_ref[...]
    for c in range(n_qk + n_v):
        cols = slice(c * MXU_N, (c + 1) * MXU_N)
        h = _dot(xb, w_ref[:, cols])
        y = _silu(_causal_conv(h, cw_ref[:, cols], GDN_CONV))
        if c < n_qk:
            parts = []
            for j in range(MXU_N // GDN_DK):
                yh = y[:, j * GDN_DK:(j + 1) * GDN_DK]
                ss = jnp.sum(yh * yh, axis=-1, keepdims=True)
                parts.append(yh * lax.rsqrt(ss + RMS_EPS))
            y = jnp.concatenate(parts, axis=1)
        qkv_ref[:, cols] = y.astype(BF16)
    xt = xb[SUBLANES:, :]
    base = (n_qk + n_v) * MXU_N
    for c in range(n_gate):
        gate_ref[:, c * MXU_N:(c + 1) * MXU_N] = _dot(
            xt, w_ref[:, base + c * MXU_N:base + (c + 1) * MXU_N]).astype(BF16)
    hab = _dot(xt, wab_ref[...])
    z = hab + dtb_ref[...]
    softplus = jnp.maximum(z, 0.0) + jnp.log(1.0 + jnp.exp(-jnp.abs(z)))
    g = -jnp.exp(alog_ref[...]) * softplus
    lane = lax.broadcasted_iota(jnp.int32, hab.shape, 1)
    gb_ref[...] = jnp.where(lane < GDN_HEADS, g, _sigmoid(hab))


def _gdn_in(x2, w_in, conv_w, a_log, dt_bias, tm, seq_tiles):
    n, d = x2.shape
    qk_w = 2 * GDN_HEADS * GDN_DK
    v_w = GDN_HEADS * GDN_DV
    main = qk_w + 2 * v_w
    w_main = w_in[:, :main].astype(BF16)
    w_ab = jnp.pad(w_in[:, main:], ((0, 0), (0, LANES - 2 * GDN_HEADS))).astype(BF16)
    pad = (0, LANES - GDN_HEADS)
    alog = jnp.pad(a_log.astype(F32), pad).reshape(1, LANES)
    dtb = jnp.pad(dt_bias.astype(F32), pad).reshape(1, LANES)
    kern = functools.partial(_gdn_in_kernel, seq_tiles=seq_tiles, n_qk=qk_w // MXU_N,
                             n_v=v_w // MXU_N, n_gate=v_w // MXU_N)
    row = lambda w: pl.BlockSpec((tm, w), lambda i: (i, 0))
    return pl.pallas_call(
        kern,
        grid=(n // tm,),
        in_specs=[row(d), _halo_spec(tm, d), _const_spec((d, main)), _const_spec((d, LANES)),
                  _const_spec((GDN_CONV, qk_w + v_w)), _const_spec((1, LANES)),
                  _const_spec((1, LANES))],
        out_specs=[row(qk_w + v_w), row(v_w), row(LANES)],
        out_shape=[jax.ShapeDtypeStruct((n, qk_w + v_w), BF16),
                   jax.ShapeDtypeStruct((n, v_w), BF16),
                   jax.ShapeDtypeStruct((n, LANES), F32)],
        scratch_shapes=[pltpu.VMEM((SUBLANES + tm, d), BF16)],
        compiler_params=_params(1),
        name="gdn_in",
    )(x2, x2, w_main, w_ab, conv_w.astype(F32), alog, dtb)


def _unit_lower_inverses(a_list, eye, blk16, off32, off64):
    c = GDN_CHUNK
    n = [jnp.where(blk16, -a, 0.0) for a in a_list]
    p = [eye + x for x in n]
    nb = [x.astype(BF16) for x in n]
    n = [_dot(x, x) for x in nb]
    for _ in range(2):
        nb = [x.astype(BF16) for x in n]
        r = [_dot(jnp.concatenate([pi.astype(BF16), ni], axis=0), ni) for pi, ni in zip(p, nb)]
        p = [pi + ri[:c] for pi, ri in zip(p, r)]
        n = [ri[c:] for ri in r]
    p = [pi + _dot(pi.astype(BF16), ni.astype(BF16)) for pi, ni in zip(p, n)]
    for off in (off32, off64):
        pb = [pi.astype(BF16) for pi in p]
        t = [_dot(jnp.where(off, a, 0.0).astype(BF16), pbi) for a, pbi in zip(a_list, pb)]
        p = [pi - _dot(pbi, ti.astype(BF16)) for pi, pbi, ti in zip(p, pb, t)]
    return p


def _gdn_chunk_kernel(q_ref, k_ref, v_ref, gate_ref, g_ref, gt_ref, beta_ref, nw_ref,
                      o_ref, s_ref, wq_ref, u_ref, ak_ref, *, n_sub):
    c = GDN_CHUNK

    @pl.when(pl.program_id(1) == 0)
    def _():
        s_ref[...] = jnp.zeros_like(s_ref)

    step = n_sub * c
    row = lax.broadcasted_iota(jnp.int32, (step, step), 0)
    col = lax.broadcasted_iota(jnp.int32, (step, step), 1)
    same = (row // c) == (col // c)
    tri = jnp.where(same & (col <= row), 1.0, 0.0)
    tri_t = jnp.where(same & (row <= col), 1.0, 0.0)
    gc_cols = jnp.dot(tri, g_ref[0], precision=lax.Precision.HIGHEST, preferred_element_type=F32)
    gc_rows = jnp.dot(gt_ref[0], tri_t, precision=lax.Precision.HIGHEST, preferred_element_type=F32)
    beta = beta_ref[0]
    egc_cols = jnp.exp(gc_cols)

    r = lax.broadcasted_iota(jnp.int32, (c, c), 0)
    s = lax.broadcasted_iota(jnp.int32, (c, c), 1)
    causal = s <= r
    strict = s < r
    eye = jnp.where(r == s, 1.0, 0.0)
    blk16 = (r // 16) == (s // 16)
    off32 = ((r // 32) == (s // 32)) & ((r // 16) != (s // 16))
    off64 = (r // 32) != (s // 32)
    scale = GDN_DK ** -0.5
    heads = range(GDN_HEADS)
    hcols = [slice(h * GDN_DK, (h + 1) * GDN_DK) for h in heads]

    for sub in range(n_sub):
        rows = slice(sub * c, (sub + 1) * c)
        qs = [q_ref[0, rows, hc] for hc in hcols]
        ks = [k_ref[0, rows, hc] for hc in hcols]
        qkk = [_dot_nt(jnp.concatenate([q, k], axis=0), k) for q, k in zip(qs, ks)]
        gcc = [gc_cols[rows, h:h + 1] for h in heads]
        bcol = [beta[rows, h:h + 1] for h in heads]
        egc = [egc_cols[rows, h:h + 1] for h in heads]
        decay = []
        for h in heads:
            diff = jnp.where(causal, gcc[h] - gc_rows[h:h + 1, rows], 0.0)
            decay.append(jnp.where(causal, jnp.exp(diff), 0.0))
        a_list = [jnp.where(strict, x[c:] * dc * bc, 0.0) for x, dc, bc in zip(qkk, decay, bcol)]
        attn = [(x[:c] * scale * dc).astype(BF16) for x, dc in zip(qkk, decay)]
        inv = _unit_lower_inverses(a_list, eye, blk16, off32, off64)
        kf = [k.astype(F32) for k in ks]
        rhs = [jnp.concatenate([v_ref[0, rows, hcols[h]].astype(F32) * bcol[h],
                                kf[h] * (bcol[h] * egc[h])], axis=1).astype(BF16) for h in heads]
        sol = [_dot(t.astype(BF16), x) for t, x in zip(inv, rhs)]
        g_last = gc_cols[(sub + 1) * c - 1:(sub + 1) * c, :]
        for h in heads:
            u_ref[sub, h] = sol[h][:, :GDN_DV]
            qd = (qs[h].astype(F32) * (egc[h] * scale)).astype(BF16)
            wq_ref[sub, h] = jnp.concatenate([sol[h][:, GDN_DV:].astype(BF16), qd], axis=0)
            kd = kf[h] * jnp.exp(g_last[:, h:h + 1] - gcc[h])
            ak_ref[sub, h] = jnp.concatenate([attn[h], kd.T.astype(BF16)], axis=0)

    nw = nw_ref[...]
    for sub in range(n_sub):
        rows = slice(sub * c, (sub + 1) * c)
        g_last = gc_cols[(sub + 1) * c - 1:(sub + 1) * c, :]
        st = [s_ref[h] for h in heads]
        r1 = [_dot(wq_ref[sub, h], st[h].astype(BF16)) for h in heads]
        vb = [(u_ref[sub, h] - r1[h][:c]).astype(BF16) for h in heads]
        r2 = [_dot(ak_ref[sub, h], vb[h]) for h in heads]
        for h in heads:
            s_ref[h] = st[h] * jnp.exp(g_last[:, h:h + 1]) + r2[h][c:]
            o = r1[h][c:] + r2[h][:c]
            ms = jnp.mean(o * o, axis=-1, keepdims=True)
            on = o * lax.rsqrt(ms + RMS_EPS) * nw
            gt = gate_ref[0, rows, hcols[h]].astype(F32)
            o_ref[0, rows, hcols[h]] = (on * _silu(gt)).astype(BF16)


def _gdn_chunk(qkv, gate, g, beta, norm_w, b, t):
    hq = GDN_HEADS * GDN_DK
    hv = GDN_HEADS * GDN_DV
    step = GDN_STEP if t % GDN_STEP == 0 else t
    qkv3 = qkv.reshape(b, t, 2 * hq + hv)
    gate3 = gate.reshape(b, t, hv)
    g3 = g.reshape(b, t, GDN_HEADS)
    gt3 = jnp.swapaxes(g3, 1, 2)
    beta3 = beta.reshape(b, t, GDN_HEADS)
    n_sub = step // GDN_CHUNK
    kern = functools.partial(_gdn_chunk_kernel, n_sub=n_sub)
    col = lambda j: pl.BlockSpec((1, step, hq), lambda bi, ci: (bi, ci, j))
    small = pl.BlockSpec((1, step, GDN_HEADS), lambda bi, ci: (bi, ci, 0))
    out = pl.pallas_call(
        kern,
        grid=(b, t // step),
        in_specs=[col(0), col(1), col(2), col(0), small,
                  pl.BlockSpec((1, GDN_HEADS, step), lambda bi, ci: (bi, 0, ci)), small,
                  _const_spec((1, GDN_DV))],
        out_specs=col(0),
        out_shape=jax.ShapeDtypeStruct((b, t, hv), BF16),
        scratch_shapes=[pltpu.VMEM((GDN_HEADS, GDN_DK, GDN_DV), F32),
                        pltpu.VMEM((n_sub, GDN_HEADS, 2 * GDN_CHUNK, GDN_DK), BF16),
                        pltpu.VMEM((n_sub, GDN_HEADS, GDN_CHUNK, GDN_DV), F32),
                        pltpu.VMEM((n_sub, GDN_HEADS, GDN_CHUNK + GDN_DK, GDN_CHUNK), BF16)],
        compiler_params=_params(2),
        name="gdn_chunk",
    )(qkv3, qkv3, qkv3, gate3, g3, gt3, beta3, norm_w.astype(F32).reshape(1, GDN_DV))
    return out.reshape(b * t, hv)


def _proj_ln_kernel(y_ref, w_ref, x_ref, g_ref, b_ref, o_ref):
    m = _dot(y_ref[...], w_ref[...])
    z = DEEPNORM_ALPHA * x_ref[...] + m
    o_ref[...] = _layer_norm_rows(z, g_ref[...], b_ref[...])


def _proj_ln(y, w, x2, g, b, tm):
    n, d = x2.shape
    k = y.shape[1]
    row = lambda w_: pl.BlockSpec((tm, w_), lambda i: (i, 0))
    return pl.pallas_call(
        _proj_ln_kernel,
        grid=(n // tm,),
        in_specs=[row(k), _const_spec((k, d)), row(d), _const_spec((1, d)), _const_spec((1, d))],
        out_specs=row(d),
        out_shape=jax.ShapeDtypeStruct((n, d), F32),
        compiler_params=_params(1),
        name="proj_ln",
    )(y, w.astype(BF16), x2, g.astype(F32).reshape(1, d), b.astype(F32).reshape(1, d))


def _ffn_kernel(x_ref, xh_ref, wup_ref, cw_ref, cb_ref, wdn_ref, g_ref, b_ref, o_ref,
                xb_ref, act_ref, *, seq_tiles, hidden):
    _stage_rows(x_ref, xh_ref, xb_ref, seq_tiles)
    xb = xb_ref[...]
    for c in range(hidden // MXU_N):
        halves = []
        for base in (0, hidden):
            cols = slice(base + c * MXU_N, base + (c + 1) * MXU_N)
            h = _dot(xb, wup_ref[:, cols])
            halves.append(_causal_conv(h, cw_ref[:, cols], FFN_CONV) + cb_ref[:, cols])
        act_ref[:, c * MXU_N:(c + 1) * MXU_N] = (_silu(halves[0]) * halves[1]).astype(BF16)
    f = _dot(act_ref[...], wdn_ref[...])
    z = DEEPNORM_ALPHA * x_ref[...] + f
    o_ref[...] = _layer_norm_rows(z, g_ref[...], b_ref[...])


def _ffn(x2, w_up, conv_w, conv_b, w_down, g, b, tm, seq_tiles):
    n, d = x2.shape
    hidden = w_down.shape[0]
    kern = functools.partial(_ffn_kernel, seq_tiles=seq_tiles, hidden=hidden)
    row = pl.BlockSpec((tm, d), lambda i: (i, 0))
    single = lambda shape: pl.BlockSpec(shape, lambda *_: (0,) * len(shape),
                                        pipeline_mode=pl.Buffered(1))
    return pl.pallas_call(
        kern,
        grid=(n // tm,),
        in_specs=[row, _halo_spec(tm, d), single((d, 2 * hidden)),
                  _const_spec((FFN_CONV, 2 * hidden)), _const_spec((1, 2 * hidden)),
                  single((hidden, d)), _const_spec((1, d)), _const_spec((1, d))],
        out_specs=row,
        out_shape=jax.ShapeDtypeStruct((n, d), F32),
        scratch_shapes=[pltpu.VMEM((SUBLANES + tm, d), BF16), pltpu.VMEM((tm, hidden), BF16)],
        compiler_params=_params(1),
        name="conv_ffn",
    )(x2, x2, w_up.astype(BF16), conv_w.astype(F32), conv_b.astype(F32).reshape(1, 2 * hidden),
      w_down.astype(BF16), g.astype(F32).reshape(1, d), b.astype(F32).reshape(1, d))


def _diff_in_kernel(x_ref, w_ref, c_ref, s1_ref, s2_ref, qk_ref, v_ref, *, n_qk, n_v, q_cols):
    xb = x_ref[...].astype(BF16)
    cos = c_ref[...]
    s1 = s1_ref[...]
    s2 = s2_ref[...]
    half = DIFF_HD // 2
    for c in range(n_qk):
        h = _dot(xb, w_ref[:, c * MXU_N:(c + 1) * MXU_N])
        for j in range(MXU_N // LANES):
            hs = h[:, j * LANES:(j + 1) * LANES]
            rot = hs * cos + pltpu.roll(hs, LANES - half, 1) * s1 + pltpu.roll(hs, half, 1) * s2
            lo = c * MXU_N + j * LANES
            if lo < q_cols:
                rot = rot * (DIFF_HD ** -0.5)
            qk_ref[:, lo:lo + LANES] = rot.astype(BF16)
    base = n_qk * MXU_N
    for c in range(n_v):
        v_ref[:, c * MXU_N:(c + 1) * MXU_N] = _dot(
            xb, w_ref[:, base + c * MXU_N:base + (c + 1) * MXU_N]).astype(BF16)


def _diff_in(x2, w_in, tables, tm):
    n, d = x2.shape
    hw = DIFF_HEADS * 2 * DIFF_HD
    kern = functools.partial(_diff_in_kernel, n_qk=2 * hw // MXU_N, n_v=hw // MXU_N, q_cols=hw)
    row = lambda w: pl.BlockSpec((tm, w), lambda i: (i, 0))
    return pl.pallas_call(
        kern,
        grid=(n // tm,),
        in_specs=[row(d), _const_spec((d, 3 * hw)), row(LANES), row(LANES), row(LANES)],
        out_specs=[row(2 * hw), row(hw)],
        out_shape=[jax.ShapeDtypeStruct((n, 2 * hw), BF16), jax.ShapeDtypeStruct((n, hw), BF16)],
        compiler_params=_params(1),
        name="diff_in",
    )(x2, w_in.astype(BF16), *tables)


def _diff_attn_kernel(q_ref, k_ref, v_ref, lq1_ref, lk1_ref, lq2_ref, lk2_ref, sw_ref, o_ref,
                      *, t, blk, lambda_init):
    lam = (jnp.exp(jnp.sum(lq1_ref[...] * lk1_ref[...], axis=-1, keepdims=True))
           - jnp.exp(jnp.sum(lq2_ref[...] * lk2_ref[...], axis=-1, keepdims=True))
           + lambda_init)
    lane = lax.broadcasted_iota(jnp.int32, (blk, 2 * DIFF_HD), 1)
    first_map = lane < DIFF_HD
    r = lax.broadcasted_iota(jnp.int32, (2 * blk, blk), 0)
    s = lax.broadcasted_iota(jnp.int32, (2 * blk, blk), 1)
    diag_ok = s <= (r % blk)
    sw = sw_ref[...]
    for qi in range(t // blk):
        q = q_ref[0, qi * blk:(qi + 1) * blk, :]
        zero = jnp.zeros_like(q)
        qs = jnp.concatenate([jnp.where(first_map, q, zero), jnp.where(first_map, zero, q)], axis=0)
        m = l = acc = None
        for kj in range(qi + 1):
            kb = k_ref[0, kj * blk:(kj + 1) * blk, :]
            vb = v_ref[0, kj * blk:(kj + 1) * blk, :]
            sc = _dot_nt(qs, kb)
            if kj == qi:
                sc = jnp.where(diag_ok, sc, -jnp.inf)
            m_blk = jnp.max(sc, axis=-1, keepdims=True)
            if kj == 0:
                m = m_blk
                p = jnp.exp(sc - m)
                l = jnp.sum(p, axis=-1, keepdims=True)
                acc = _dot(p.astype(BF16), vb)
            else:
                m_new = jnp.maximum(m, m_blk)
                alpha = jnp.exp(m - m_new)
                p = jnp.exp(sc - m_new)
                l = alpha * l + jnp.sum(p, axis=-1, keepdims=True)
                acc = alpha * acc + _dot(p.astype(BF16), vb)
                m = m_new
        on = acc / l
        o = on[:blk] - lam * on[blk:]
        ms = jnp.mean(o * o, axis=-1, keepdims=True)
        o = o * lax.rsqrt(ms + RMS_EPS) * sw * (1.0 - lambda_init)
        o_ref[0, qi * blk:(qi + 1) * blk, :] = o.astype(BF16)


def _diff_attn(qk, v, lam_q1, lam_k1, lam_q2, lam_k2, subln_w, lambda_init, b, t):
    hw = DIFF_HEADS * 2 * DIFF_HD
    hd2 = 2 * DIFF_HD
    blk = ATTN_BLOCK if t % ATTN_BLOCK == 0 else t
    qk3 = qk.reshape(b, t, 2 * hw)
    v3 = v.reshape(b, t, hw)
    kern = functools.partial(_diff_attn_kernel, t=t, blk=blk, lambda_init=lambda_init)
    head = lambda off: pl.BlockSpec((1, t, hd2), lambda bi, hi: (bi, 0, hi + off))
    vec = lambda a: a.astype(F32).reshape(1, -1)
    out = pl.pallas_call(
        kern,
        grid=(b, DIFF_HEADS),
        in_specs=[head(0), head(DIFF_HEADS), head(0)] + [_const_spec((1, DIFF_HD))] * 4
                 + [_const_spec((1, hd2))],
        out_specs=head(0),
        out_shape=jax.ShapeDtypeStruct((b, t, hw), BF16),
        compiler_params=_params(2),
        name="diff_attn",
    )(qk3, qk3, v3, vec(lam_q1), vec(lam_k1), vec(lam_q2), vec(lam_k2), vec(subln_w))
    return out.reshape(b * t, hw)


def kernel(x, positions, gdn_w_in, gdn_conv_w, gdn_a_log, gdn_dt_bias, gdn_norm_w, gdn_w_out,
           diff_w_in, diff_lam_q1, diff_lam_k1, diff_lam_q2, diff_lam_k2, diff_subln_w, diff_w_out,
           ffn_w_up, ffn_conv_w, ffn_conv_b, ffn_w_down, ln_mix_g, ln_mix_b, ln_ffn_g, ln_ffn_b):
    b, t, d = x.shape
    tm = _row_tile(t)
    seq_tiles = t // tm
    x2 = x.reshape(b * t, d).astype(F32)
    tables = _rope_tables(positions, tm)
    for i in range(DEPTH):
        j = i // 2
        if i % 2 == 0:
            qkv, gate, gb = _gdn_in(x2, gdn_w_in[j], gdn_conv_w[j], gdn_a_log[j], gdn_dt_bias[j],
                                    tm, seq_tiles)
            g = gb[:, :GDN_HEADS]
            beta = gb[:, GDN_HEADS:2 * GDN_HEADS]
            mixed = _gdn_chunk(qkv, gate, g, beta, gdn_norm_w[j], b, t)
            w_out = gdn_w_out[j]
        else:
            lambda_init = 0.8 - 0.6 * math.exp(-0.3 * i)
            qk, v = _diff_in(x2, diff_w_in[j], tables, tm)
            mixed = _diff_attn(qk, v, diff_lam_q1[j], diff_lam_k1[j], diff_lam_q2[j], diff_lam_k2[j],
                               diff_subln_w[j], lambda_init, b, t)
            w_out = diff_w_out[j]
        x2 = _proj_ln(mixed, w_out, x2, ln_mix_g[i], ln_mix_b[i], tm)
        x2 = _ffn(x2, ffn_w_up[i], ffn_conv_w[i], ffn_conv_b[i], ffn_w_down[i],
                  ln_ffn_g[i], ln_ffn_b[i], tm, seq_tiles)
    return x2.reshape(b, t, d)
```

```python
import functools
import math

import jax
import jax.numpy as jnp
from jax import lax
from jax.experimental import pallas as pl
from jax.experimental.pallas import tpu as pltpu

F32 = jnp.float32
BF16 = jnp.bfloat16

DEPTH = 4
GDN_HEADS = 8
GDN_DK = 128
GDN_DV = 128
GDN_CONV = 4
GDN_CHUNK = 64
DIFF_HEADS = 8
DIFF_HD = 64
ROPE_THETA = 10000.0
FFN_CONV = 3
DEEPNORM_ALPHA = (2.0 * DEPTH) ** 0.25
LN_EPS = 1e-5
RMS_EPS = 1e-6

LANES = 128
SUBLANES = 8
MXU_N = 256
VMEM_LIMIT = 56 * 1024 * 1024

ROW_TILE = 512
ATTN_BLOCK = 512
GDN_STEP = 256
GDN_WAVE = 4


def _row_tile(t):
    return ROW_TILE if t % ROW_TILE == 0 else t


def _sigmoid(x):
    return 1.0 / (1.0 + jnp.exp(-x))


def _silu(x):
    return x * _sigmoid(x)


def _dot(a, b):
    return jnp.dot(a, b, preferred_element_type=F32)


def _dot_nt(a, b):
    return lax.dot_general(a, b, (((1,), (1,)), ((), ())), preferred_element_type=F32)


def _dot_tn(a, b):
    return lax.dot_general(a, b, (((0,), (0,)), ((), ())), preferred_element_type=F32)


def _shift_rows(h, s):
    return pltpu.roll(h, s, 0)


def _causal_conv(h, cw, width):
    y = h * cw[width - 1:width, :]
    for s in range(1, width):
        y = y + _shift_rows(h, s) * cw[width - 1 - s:width - s, :]
    return y[SUBLANES:, :]


def _layer_norm_rows(z, g, b):
    mu = jnp.mean(z, axis=-1, keepdims=True)
    zc = z - mu
    var = jnp.mean(zc * zc, axis=-1, keepdims=True)
    return zc * lax.rsqrt(var + LN_EPS) * g + b


def _stage_rows(x_ref, xh_ref, xb_ref, seq_tiles):
    first = (pl.program_id(0) % seq_tiles) == 0
    halo = jnp.where(first, 0.0, xh_ref[...])
    xb_ref[0:SUBLANES, :] = halo.astype(BF16)
    xb_ref[SUBLANES:, :] = x_ref[...].astype(BF16)


def _halo_spec(tm, d):
    return pl.BlockSpec((SUBLANES, d), lambda i: (jnp.maximum(i * (tm // SUBLANES) - 1, 0), 0))


def _const_spec(shape):
    return pl.BlockSpec(shape, lambda *_: (0,) * len(shape))


def _params(n_axes):
    return pltpu.CompilerParams(
        dimension_semantics=("arbitrary",) * n_axes, vmem_limit_bytes=VMEM_LIMIT)


def _rope_kernel(pos_ref, inv_ref, c_ref, s1_ref, s2_ref):
    ang = pos_ref[...].astype(F32) * inv_ref[...]
    cos = jnp.cos(ang)
    sin = jnp.sin(ang)
    lane = lax.broadcasted_iota(jnp.int32, ang.shape, 1)
    lower = (lane % DIFF_HD) < (DIFF_HD // 2)
    c_ref[...] = cos
    s1_ref[...] = jnp.where(lower, -sin, 0.0)
    s2_ref[...] = jnp.where(lower, 0.0, sin)


def _rope_tables(positions, tm):
    n = positions.size
    inv_freq = ROPE_THETA ** (-jnp.arange(0, DIFF_HD, 2, dtype=F32) / DIFF_HD)
    inv = jnp.tile(inv_freq, LANES // (DIFF_HD // 2)).reshape(1, LANES)
    pos = positions.reshape(n, 1)
    out = jax.ShapeDtypeStruct((n, LANES), F32)
    return pl.pallas_call(
        _rope_kernel,
        grid=(n // tm,),
        in_specs=[pl.BlockSpec((tm, 1), lambda i: (i, 0)), _const_spec((1, LANES))],
        out_specs=[pl.BlockSpec((tm, LANES), lambda i: (i, 0))] * 3,
        out_shape=[out] * 3,
        compiler_params=_params(1),
        name="rope_tables",
    )(pos, inv)


def _gdn_in_kernel(x_ref, xh_ref, w_ref, wab_ref, cw_ref, alog_ref, dtb_ref,
                   qkv_ref, gate_ref, gb_ref, xb_ref, *, seq_tiles, n_qk, n_v, n_gate):
    _stage_rows(x_ref, xh_ref, xb_ref, seq_tiles)
    xb = xb_ref[...]
    for c in range(n_qk + n_v):
        cols = slice(c * MXU_N, (c + 1) * MXU_N)
        h = _dot(xb, w_ref[:, cols])
        y = _silu(_causal_conv(h, cw_ref[:, cols], GDN_CONV))
        if c < n_qk:
            parts = []
            for j in range(MXU_N // GDN_DK):
                yh = y[:, j * GDN_DK:(j + 1) * GDN_DK]
                ss = jnp.sum(yh * yh, axis=-1, keepdims=True)
                parts.append(yh * lax.rsqrt(ss + RMS_EPS))
            y = jnp.concatenate(parts, axis=1)
        qkv_ref[:, cols] = y.astype(BF16)
    xt = xb[SUBLANES:, :]
    base = (n_qk + n_v) * MXU_N
    for c in range(n_gate):
        gate_ref[:, c * MXU_N:(c + 1) * MXU_N] = _dot(
            xt, w_ref[:, base + c * MXU_N:base + (c + 1) * MXU_N]).astype(BF16)
    hab = _dot(xt, wab_ref[...])
    z = hab + dtb_ref[...]
    softplus = jnp.maximum(z, 0.0) + jnp.log(1.0 + jnp.exp(-jnp.abs(z)))
    g = -jnp.exp(alog_ref[...]) * softplus
    lane = lax.broadcasted_iota(jnp.int32, hab.shape, 1)
    gb_ref[...] = jnp.where(lane < GDN_HEADS, g, _sigmoid(hab))


def _gdn_in(x2, w_in, conv_w, a_log, dt_bias, tm, seq_tiles):
    n, d = x2.shape
    qk_w = 2 * GDN_HEADS * GDN_DK
    v_w = GDN_HEADS * GDN_DV
    main = qk_w + 2 * v_w
    w_main = w_in[:, :main].astype(BF16)
    w_ab = jnp.pad(w_in[:, main:], ((0, 0), (0, LANES - 2 * GDN_HEADS))).astype(BF16)
    pad = (0, LANES - GDN_HEADS)
    alog = jnp.pad(a_log.astype(F32), pad).reshape(1, LANES)
    dtb = jnp.pad(dt_bias.astype(F32), pad).reshape(1, LANES)
    kern = functools.partial(_gdn_in_kernel, seq_tiles=seq_tiles, n_qk=qk_w // MXU_N,
                             n_v=v_w // MXU_N, n_gate=v_w // MXU_N)
    row = lambda w: pl.BlockSpec((tm, w), lambda i: (i, 0))
    return pl.pallas_call(
        kern,
        grid=(n // tm,),
        in_specs=[row(d), _halo_spec(tm, d), _const_spec((d, main)), _const_spec((d, LANES)),
                  _const_spec((GDN_CONV, qk_w + v_w)), _const_spec((1, LANES)),
                  _const_spec((1, LANES))],
        out_specs=[row(qk_w + v_w), row(v_w), row(LANES)],
        out_shape=[jax.ShapeDtypeStruct((n, qk_w + v_w), BF16),
                   jax.ShapeDtypeStruct((n, v_w), BF16),
                   jax.ShapeDtypeStruct((n, LANES), F32)],
        scratch_shapes=[pltpu.VMEM((SUBLANES + tm, d), BF16)],
        compiler_params=_params(1),
        name="gdn_in",
    )(x2, x2, w_main, w_ab, conv_w.astype(F32), alog, dtb)


def _unit_lower_inverses(a_list, eye, blk16, off32, off64):
    c = GDN_CHUNK
    n = [jnp.where(blk16, -a, 0.0) for a in a_list]
    p = [eye + x for x in n]
    nb = [x.astype(BF16) for x in n]
    n = [_dot(x, x) for x in nb]
    for _ in range(2):
        nb = [x.astype(BF16) for x in n]
        r = [_dot(jnp.concatenate([pi.astype(BF16), ni], axis=0), ni) for pi, ni in zip(p, nb)]
        p = [pi + ri[:c] for pi, ri in zip(p, r)]
        n = [ri[c:] for ri in r]
    p = [pi + _dot(pi.astype(BF16), ni.astype(BF16)) for pi, ni in zip(p, n)]
    for off in (off32, off64):
        pb = [pi.astype(BF16) for pi in p]
        t = [_dot(jnp.where(off, a, 0.0).astype(BF16), pbi) for a, pbi in zip(a_list, pb)]
        p = [pi - _dot(pbi, ti.astype(BF16)) for pi, pbi, ti in zip(p, pb, t)]
    return p


def _gdn_chunk_kernel(q_ref, k_ref, v_ref, gate_ref, g_ref, gt_ref, beta_ref, nw_ref,
                      o_ref, s_ref, wq_ref, u_ref, ak_ref, *, n_sub):
    c = GDN_CHUNK

    @pl.when(pl.program_id(1) == 0)
    def _():
        s_ref[...] = jnp.zeros_like(s_ref)

    step = n_sub * c
    row = lax.broadcasted_iota(jnp.int32, (step, step), 0)
    col = lax.broadcasted_iota(jnp.int32, (step, step), 1)
    same = (row // c) == (col // c)
    tri = jnp.where(same & (col <= row), 1.0, 0.0)
    tri_t = jnp.where(same & (row <= col), 1.0, 0.0)
    gc_cols = jnp.dot(tri, g_ref[0], precision=lax.Precision.HIGHEST, preferred_element_type=F32)
    gc_rows = jnp.dot(gt_ref[0], tri_t, precision=lax.Precision.HIGHEST, preferred_element_type=F32)
    beta = beta_ref[0]
    egc_cols = jnp.exp(gc_cols)

    r = lax.broadcasted_iota(jnp.int32, (c, c), 0)
    s = lax.broadcasted_iota(jnp.int32, (c, c), 1)
    causal = s <= r
    strict = s < r
    eye = jnp.where(r == s, 1.0, 0.0)
    blk16 = (r // 16) == (s // 16)
    off32 = ((r // 32) == (s // 32)) & ((r // 16) != (s // 16))
    off64 = (r // 32) != (s // 32)
    scale = GDN_DK ** -0.5
    heads = range(GDN_HEADS)
    hcols = [slice(h * GDN_DK, (h + 1) * GDN_DK) for h in heads]

    for wave in range(0, n_sub, GDN_WAVE):
        pairs = [(sub, h) for sub in range(wave, min(wave + GDN_WAVE, n_sub)) for h in heads]
        rws = [slice(sub * c, (sub + 1) * c) for sub, _ in pairs]
        qs = [q_ref[0, rw, hcols[h]] for rw, (_, h) in zip(rws, pairs)]
        ks = [k_ref[0, rw, hcols[h]] for rw, (_, h) in zip(rws, pairs)]
        qkk = [_dot_nt(jnp.concatenate([q, k], axis=0), k) for q, k in zip(qs, ks)]
        gcc = [gc_cols[rw, h:h + 1] for rw, (_, h) in zip(rws, pairs)]
        bcol = [beta[rw, h:h + 1] for rw, (_, h) in zip(rws, pairs)]
        egc = [egc_cols[rw, h:h + 1] for rw, (_, h) in zip(rws, pairs)]
        decay = []
        for i, (rw, (_, h)) in enumerate(zip(rws, pairs)):
            diff = jnp.where(causal, gcc[i] - gc_rows[h:h + 1, rw], 0.0)
            decay.append(jnp.where(causal, jnp.exp(diff), 0.0))
        a_list = [jnp.where(strict, x[c:] * dc * bc, 0.0) for x, dc, bc in zip(qkk, decay, bcol)]
        attn = [(x[:c] * scale * dc).astype(BF16) for x, dc in zip(qkk, decay)]
        inv = _unit_lower_inverses(a_list, eye, blk16, off32, off64)
        kf = [k.astype(F32) for k in ks]
        rhs = [jnp.concatenate([v_ref[0, rw, hcols[h]].astype(F32) * bcol[i],
                                kf[i] * (bcol[i] * egc[i])], axis=1).astype(BF16)
               for i, (rw, (_, h)) in enumerate(zip(rws, pairs))]
        sol = [_dot(t.astype(BF16), x) for t, x in zip(inv, rhs)]
        for i, (sub, h) in enumerate(pairs):
            g_last = gc_cols[(sub + 1) * c - 1:(sub + 1) * c, h:h + 1]
            u_ref[sub, h] = sol[i][:, :GDN_DV]
            qd = (qs[i].astype(F32) * (egc[i] * scale)).astype(BF16)
            wq_ref[sub, h] = jnp.concatenate([sol[i][:, GDN_DV:].astype(BF16), qd], axis=0)
            kd = kf[i] * jnp.exp(g_last - gcc[i])
            ak_ref[sub, h] = jnp.concatenate([attn[i], kd.T.astype(BF16)], axis=0)

    nw = nw_ref[...]
    for sub in range(n_sub):
        rows = slice(sub * c, (sub + 1) * c)
        g_last = gc_cols[(sub + 1) * c - 1:(sub + 1) * c, :]
        st = [s_ref[h] for h in heads]
        r1 = [_dot(wq_ref[sub, h], st[h].astype(BF16)) for h in heads]
        vb = [(u_ref[sub, h] - r1[h][:c]).astype(BF16) for h in heads]
        r2 = [_dot(ak_ref[sub, h], vb[h]) for h in heads]
        for h in heads:
            s_ref[h] = st[h] * jnp.exp(g_last[:, h:h + 1]) + r2[h][c:]
            o = r1[h][c:] + r2[h][:c]
            ms = jnp.mean(o * o, axis=-1, keepdims=True)
            on = o * lax.rsqrt(ms + RMS_EPS) * nw
            gt = gate_ref[0, rows, hcols[h]].astype(F32)
            o_ref[0, rows, hcols[h]] = (on * _silu(gt)).astype(BF16)


def _gdn_chunk(qkv, gate, g, beta, norm_w, b, t):
    hq = GDN_HEADS * GDN_DK
    hv = GDN_HEADS * GDN_DV
    step = GDN_STEP if t % GDN_STEP == 0 else t
    qkv3 = qkv.reshape(b, t, 2 * hq + hv)
    gate3 = gate.reshape(b, t, hv)
    g3 = g.reshape(b, t, GDN_HEADS)
    gt3 = jnp.swapaxes(g3, 1, 2)
    beta3 = beta.reshape(b, t, GDN_HEADS)
    n_sub = step // GDN_CHUNK
    kern = functools.partial(_gdn_chunk_kernel, n_sub=n_sub)
    col = lambda j: pl.BlockSpec((1, step, hq), lambda bi, ci: (bi, ci, j))
    small = pl.BlockSpec((1, step, GDN_HEADS), lambda bi, ci: (bi, ci, 0))
    out = pl.pallas_call(
        kern,
        grid=(b, t // step),
        in_specs=[col(0), col(1), col(2), col(0), small,
                  pl.BlockSpec((1, GDN_HEADS, step), lambda bi, ci: (bi, 0, ci)), small,
                  _const_spec((1, GDN_DV))],
        out_specs=col(0),
        out_shape=jax.ShapeDtypeStruct((b, t, hv), BF16),
        scratch_shapes=[pltpu.VMEM((GDN_HEADS, GDN_DK, GDN_DV), F32),
                        pltpu.VMEM((n_sub, GDN_HEADS, 2 * GDN_CHUNK, GDN_DK), BF16),
                        pltpu.VMEM((n_sub, GDN_HEADS, GDN_CHUNK, GDN_DV), F32),
                        pltpu.VMEM((n_sub, GDN_HEADS, GDN_CHUNK + GDN_DK, GDN_CHUNK), BF16)],
        compiler_params=_params(2),
        name="gdn_chunk",
    )(qkv3, qkv3, qkv3, gate3, g3, gt3, beta3, norm_w.astype(F32).reshape(1, GDN_DV))
    return out.reshape(b * t, hv)


def _proj_ln_kernel(y_ref, w_ref, x_ref, g_ref, b_ref, o_ref):
    m = _dot(y_ref[...], w_ref[...])
    z = DEEPNORM_ALPHA * x_ref[...] + m
    o_ref[...] = _layer_norm_rows(z, g_ref[...], b_ref[...])


def _proj_ln(y, w, x2, g, b, tm):
    n, d = x2.shape
    k = y.shape[1]
    row = lambda w_: pl.BlockSpec((tm, w_), lambda i: (i, 0))
    return pl.pallas_call(
        _proj_ln_kernel,
        grid=(n // tm,),
        in_specs=[row(k), _const_spec((k, d)), row(d), _const_spec((1, d)), _const_spec((1, d))],
        out_specs=row(d),
        out_shape=jax.ShapeDtypeStruct((n, d), F32),
        compiler_params=_params(1),
        name="proj_ln",
    )(y, w.astype(BF16), x2, g.astype(F32).reshape(1, d), b.astype(F32).reshape(1, d))


def _ffn_kernel(x_ref, xh_ref, wup_ref, cw_ref, cb_ref, wdn_ref, g_ref, b_ref, o_ref,
                xb_ref, act_ref, *, seq_tiles, hidden):
    _stage_rows(x_ref, xh_ref, xb_ref, seq_tiles)
    xb = xb_ref[...]
    for c in range(hidden // MXU_N):
        halves = []
        for base in (0, hidden):
            cols = slice(base + c * MXU_N, base + (c + 1) * MXU_N)
            h = _dot(xb, wup_ref[:, cols])
            halves.append(_causal_conv(h, cw_ref[:, cols], FFN_CONV) + cb_ref[:, cols])
        act_ref[:, c * MXU_N:(c + 1) * MXU_N] = (_silu(halves[0]) * halves[1]).astype(BF16)
    f = _dot(act_ref[...], wdn_ref[...])
    z = DEEPNORM_ALPHA * x_ref[...] + f
    o_ref[...] = _layer_norm_rows(z, g_ref[...], b_ref[...])


def _ffn(x2, w_up, conv_w, conv_b, w_down, g, b, tm, seq_tiles):
    n, d = x2.shape
    hidden = w_down.shape[0]
    kern = functools.partial(_ffn_kernel, seq_tiles=seq_tiles, hidden=hidden)
    row = pl.BlockSpec((tm, d), lambda i: (i, 0))
    single = lambda shape: pl.BlockSpec(shape, lambda *_: (0,) * len(shape),
                                        pipeline_mode=pl.Buffered(1))
    return pl.pallas_call(
        kern,
        grid=(n // tm,),
        in_specs=[row, _halo_spec(tm, d), single((d, 2 * hidden)),
                  _const_spec((FFN_CONV, 2 * hidden)), _const_spec((1, 2 * hidden)),
                  single((hidden, d)), _const_spec((1, d)), _const_spec((1, d))],
        out_specs=row,
        out_shape=jax.ShapeDtypeStruct((n, d), F32),
        scratch_shapes=[pltpu.VMEM((SUBLANES + tm, d), BF16), pltpu.VMEM((tm, hidden), BF16)],
        compiler_params=_params(1),
        name="conv_ffn",
    )(x2, x2, w_up.astype(BF16), conv_w.astype(F32), conv_b.astype(F32).reshape(1, 2 * hidden),
      w_down.astype(BF16), g.astype(F32).reshape(1, d), b.astype(F32).reshape(1, d))


def _diff_in_kernel(x_ref, w_ref, c_ref, s1_ref, s2_ref, qk_ref, v_ref, *, n_qk, n_v, q_cols):
    xb = x_ref[...].astype(BF16)
    cos = c_ref[...]
    s1 = s1_ref[...]
    s2 = s2_ref[...]
    half = DIFF_HD // 2
    for c in range(n_qk):
        h = _dot(xb, w_ref[:, c * MXU_N:(c + 1) * MXU_N])
        for j in range(MXU_N // LANES):
            hs = h[:, j * LANES:(j + 1) * LANES]
            rot = hs * cos + pltpu.roll(hs, LANES - half, 1) * s1 + pltpu.roll(hs, half, 1) * s2
            lo = c * MXU_N + j * LANES
            if lo < q_cols:
                rot = rot * (DIFF_HD ** -0.5)
            qk_ref[:, lo:lo + LANES] = rot.astype(BF16)
    base = n_qk * MXU_N
    for c in range(n_v):
        v_ref[:, c * MXU_N:(c + 1) * MXU_N] = _dot(
            xb, w_ref[:, base + c * MXU_N:base + (c + 1) * MXU_N]).astype(BF16)


def _diff_in(x2, w_in, tables, tm):
    n, d = x2.shape
    hw = DIFF_HEADS * 2 * DIFF_HD
    kern = functools.partial(_diff_in_kernel, n_qk=2 * hw // MXU_N, n_v=hw // MXU_N, q_cols=hw)
    row = lambda w: pl.BlockSpec((tm, w), lambda i: (i, 0))
    return pl.pallas_call(
        kern,
        grid=(n // tm,),
        in_specs=[row(d), _const_spec((d, 3 * hw)), row(LANES), row(LANES), row(LANES)],
        out_specs=[row(2 * hw), row(hw)],
        out_shape=[jax.ShapeDtypeStruct((n, 2 * hw), BF16), jax.ShapeDtypeStruct((n, hw), BF16)],
        compiler_params=_params(1),
        name="diff_in",
    )(x2, w_in.astype(BF16), *tables)


def _diff_attn_kernel(q_ref, k_ref, v_ref, lq1_ref, lk1_ref, lq2_ref, lk2_ref, sw_ref, o_ref,
                      *, t, blk, lambda_init):
    lam = (jnp.exp(jnp.sum(lq1_ref[...] * lk1_ref[...], axis=-1, keepdims=True))
           - jnp.exp(jnp.sum(lq2_ref[...] * lk2_ref[...], axis=-1, keepdims=True))
           + lambda_init)
    lane = lax.broadcasted_iota(jnp.int32, (blk, 2 * DIFF_HD), 1)
    first_map = lane < DIFF_HD
    r = lax.broadcasted_iota(jnp.int32, (2 * blk, blk), 0)
    s = lax.broadcasted_iota(jnp.int32, (2 * blk, blk), 1)
    diag_ok = s <= (r % blk)
    sw = sw_ref[...]
    for qi in range(t // blk):
        q = q_ref[0, qi * blk:(qi + 1) * blk, :]
        zero = jnp.zeros_like(q)
        qs = jnp.concatenate([jnp.where(first_map, q, zero), jnp.where(first_map, zero, q)], axis=0)
        m = l = acc = None
        for kj in range(qi + 1):
            kb = k_ref[0, kj * blk:(kj + 1) * blk, :]
            vb = v_ref[0, kj * blk:(kj + 1) * blk, :]
            sc = _dot_nt(qs, kb)
            if kj == qi:
                sc = jnp.where(diag_ok, sc, -jnp.inf)
            m_blk = jnp.max(sc, axis=-1, keepdims=True)
            if kj == 0:
                m = m_blk
                p = jnp.exp(sc - m)
                l = jnp.sum(p, axis=-1, keepdims=True)
                acc = _dot(p.astype(BF16), vb)
            else:
                m_new = jnp.maximum(m, m_blk)
                alpha = jnp.exp(m - m_new)
                p = jnp.exp(sc - m_new)
                l = alpha * l + jnp.sum(p, axis=-1, keepdims=True)
                acc = alpha * acc + _dot(p.astype(BF16), vb)
                m = m_new
        on = acc / l
        o = on[:blk] - lam * on[blk:]
        ms = jnp.mean(o * o, axis=-1, keepdims=True)
        o = o * lax.rsqrt(ms + RMS_EPS) * sw * (1.0 - lambda_init)
        o_ref[0, qi * blk:(qi + 1) * blk, :] = o.astype(BF16)


def _diff_attn(qk, v, lam_q1, lam_k1, lam_q2, lam_k2, subln_w, lambda_init, b, t):
    hw = DIFF_HEADS * 2 * DIFF_HD
    hd2 = 2 * DIFF_HD
    blk = ATTN_BLOCK if t % ATTN_BLOCK == 0 else t
    qk3 = qk.reshape(b, t, 2 * hw)
    v3 = v.reshape(b, t, hw)
    kern = functools.partial(_diff_attn_kernel, t=t, blk=blk, lambda_init=lambda_init)
    head = lambda off: pl.BlockSpec((1, t, hd2), lambda bi, hi: (bi, 0, hi + off))
    vec = lambda a: a.astype(F32).reshape(1, -1)
    out = pl.pallas_call(
        kern,
        grid=(b, DIFF_HEADS),
        in_specs=[head(0), head(DIFF_HEADS), head(0)] + [_const_spec((1, DIFF_HD))] * 4
                 + [_const_spec((1, hd2))],
        out_specs=head(0),
        out_shape=jax.ShapeDtypeStruct((b, t, hw), BF16),
        compiler_params=_params(2),
        name="diff_attn",
    )(qk3, qk3, v3, vec(lam_q1), vec(lam_k1), vec(lam_q2), vec(lam_k2), vec(subln_w))
    return out.reshape(b * t, hw)


def kernel(x, positions, gdn_w_in, gdn_conv_w, gdn_a_log, gdn_dt_bias, gdn_norm_w, gdn_w_out,
           diff_w_in, diff_lam_q1, diff_lam_k1, diff_lam_q2, diff_lam_k2, diff_subln_w, diff_w_out,
           ffn_w_up, ffn_conv_w, ffn_conv_b, ffn_w_down, ln_mix_g, ln_mix_b, ln_ffn_g, ln_ffn_b):
    b, t, d = x.shape
    tm = _row_tile(t)
    seq_tiles = t // tm
    x2 = x.reshape(b * t, d).astype(F32)
    tables = _rope_tables(positions, tm)
    for i in range(DEPTH):
        j = i // 2
        if i % 2 == 0:
            qkv, gate, gb = _gdn_in(x2, gdn_w_in[j], gdn_conv_w[j], gdn_a_log[j], gdn_dt_bias[j],
                                    tm, seq_tiles)
            g = gb[:, :GDN_HEADS]
            beta = gb[:, GDN_HEADS:2 * GDN_HEADS]
            mixed = _gdn_chunk(qkv, gate, g, beta, gdn_norm_w[j], b, t)
            w_out = gdn_w_out[j]
        else:
            lambda_init = 0.8 - 0.6 * math.exp(-0.3 * i)
            qk, v = _diff_in(x2, diff_w_in[j], tables, tm)
            mixed = _diff_attn(qk, v, diff_lam_q1[j], diff_lam_k1[j], diff_lam_q2[j], diff_lam_k2[j],
                               diff_subln_w[j], lambda_init, b, t)
            w_out = diff_w_out[j]
        x2 = _proj_ln(mixed, w_out, x2, ln_mix_g[i], ln_mix_b[i], tm)
        x2 = _ffn(x2, ffn_w_up[i], ffn_conv_w[i], ffn_conv_b[i], ffn_w_down[i],
                  ln_ffn_g[i], ln_ffn_b[i], tm, seq_tiles)
    return x2.reshape(b, t, d)
```

```python
import functools
import math

import jax
import jax.numpy as jnp
from jax import lax
from jax.experimental import pallas as pl
from jax.experimental.pallas import tpu as pltpu

F32 = jnp.float32
BF16 = jnp.bfloat16

DEPTH = 4
GDN_HEADS = 8
GDN_DK = 128
GDN_DV = 128
GDN_CONV = 4
GDN_CHUNK = 64
DIFF_HEADS = 8
DIFF_HD = 64
ROPE_THETA = 10000.0
FFN_CONV = 3
DEEPNORM_ALPHA = (2.0 * DEPTH) ** 0.25
LN_EPS = 1e-5
RMS_EPS = 1e-6

LANES = 128
SUBLANES = 8
MXU_N = 256
VMEM_LIMIT = 56 * 1024 * 1024

ROW_TILE = 512
ATTN_BLOCK = 256
GDN_STEP = 256
GDN_WAVE = 4


def _row_tile(t):
    return ROW_TILE if t % ROW_TILE == 0 else t


def _sigmoid(x):
    return 1.0 / (1.0 + jnp.exp(-x))


def _silu(x):
    return x * _sigmoid(x)


def _dot(a, b):
    return jnp.dot(a, b, preferred_element_type=F32)


def _dot_nt(a, b):
    return lax.dot_general(a, b, (((1,), (1,)), ((), ())), preferred_element_type=F32)


def _dot_tn(a, b):
    return lax.dot_general(a, b, (((0,), (0,)), ((), ())), preferred_element_type=F32)


def _shift_rows(h, s):
    return pltpu.roll(h, s, 0)


def _causal_conv(h, cw, width):
    y = h * cw[width - 1:width, :]
    for s in range(1, width):
        y = y + _shift_rows(h, s) * cw[width - 1 - s:width - s, :]
    return y[SUBLANES:, :]


def _layer_norm_rows(z, g, b):
    mu = jnp.mean(z, axis=-1, keepdims=True)
    zc = z - mu
    var = jnp.mean(zc * zc, axis=-1, keepdims=True)
    return zc * lax.rsqrt(var + LN_EPS) * g + b


def _stage_rows(x_ref, xh_ref, xb_ref, seq_tiles):
    first = (pl.program_id(0) % seq_tiles) == 0
    halo = jnp.where(first, 0.0, xh_ref[...])
    xb_ref[0:SUBLANES, :] = halo.astype(BF16)
    xb_ref[SUBLANES:, :] = x_ref[...].astype(BF16)


def _halo_spec(tm, d):
    return pl.BlockSpec((SUBLANES, d), lambda i: (jnp.maximum(i * (tm // SUBLANES) - 1, 0), 0))


def _const_spec(shape):
    return pl.BlockSpec(shape, lambda *_: (0,) * len(shape))


def _params(n_axes):
    return pltpu.CompilerParams(
        dimension_semantics=("arbitrary",) * n_axes, vmem_limit_bytes=VMEM_LIMIT)


def _rope_kernel(pos_ref, inv_ref, c_ref, s1_ref, s2_ref):
    ang = pos_ref[...].astype(F32) * inv_ref[...]
    cos = jnp.cos(ang)
    sin = jnp.sin(ang)
    lane = lax.broadcasted_iota(jnp.int32, ang.shape, 1)
    lower = (lane % DIFF_HD) < (DIFF_HD // 2)
    c_ref[...] = cos
    s1_ref[...] = jnp.where(lower, -sin, 0.0)
    s2_ref[...] = jnp.where(lower, 0.0, sin)


def _rope_tables(positions, tm):
    n = positions.size
    inv_freq = ROPE_THETA ** (-jnp.arange(0, DIFF_HD, 2, dtype=F32) / DIFF_HD)
    inv = jnp.tile(inv_freq, LANES // (DIFF_HD // 2)).reshape(1, LANES)
    pos = positions.reshape(n, 1)
    out = jax.ShapeDtypeStruct((n, LANES), F32)
    return pl.pallas_call(
        _rope_kernel,
        grid=(n // tm,),
        in_specs=[pl.BlockSpec((tm, 1), lambda i: (i, 0)), _const_spec((1, LANES))],
        out_specs=[pl.BlockSpec((tm, LANES), lambda i: (i, 0))] * 3,
        out_shape=[out] * 3,
        compiler_params=_params(1),
        name="rope_tables",
    )(pos, inv)


def _gdn_in_kernel(x_ref, xh_ref, w_ref, wab_ref, cw_ref, alog_ref, dtb_ref,
                   qkv_ref, gate_ref, gb_ref, xb_ref, *, seq_tiles, n_qk, n_v, n_gate):
    _stage_rows(x_ref, xh_ref, xb_ref, seq_tiles)
    xb = xb_ref[...]
    xt = xb[SUBLANES:, :]
    base = (n_qk + n_v) * MXU_N
    conv_per_gate = (n_qk + n_v) // n_gate
    jobs = []
    for c in range(n_qk + n_v):
        jobs.append(("conv", c))
        if c % conv_per_gate == conv_per_gate - 1:
            jobs.append(("gate", c // conv_per_gate))

    def project(job):
        kind, c = job
        if kind == "conv":
            return _dot(xb, w_ref[:, c * MXU_N:(c + 1) * MXU_N])
        return _dot(xt, w_ref[:, base + c * MXU_N:base + (c + 1) * MXU_N])

    def finish(job, h):
        kind, c = job
        cols = slice(c * MXU_N, (c + 1) * MXU_N)
        if kind == "gate":
            gate_ref[:, cols] = h.astype(BF16)
            return
        y = _silu(_causal_conv(h, cw_ref[:, cols], GDN_CONV))
        if c < n_qk:
            parts = []
            for j in range(MXU_N // GDN_DK):
                yh = y[:, j * GDN_DK:(j + 1) * GDN_DK]
                ss = jnp.sum(yh * yh, axis=-1, keepdims=True)
                parts.append(yh * lax.rsqrt(ss + RMS_EPS))
            y = jnp.concatenate(parts, axis=1)
        qkv_ref[:, cols] = y.astype(BF16)

    h_next = project(jobs[0])
    for i, job in enumerate(jobs):
        h = h_next
        if i + 1 < len(jobs):
            h_next = project(jobs[i + 1])
        finish(job, h)
    hab = _dot(xt, wab_ref[...])
    z = hab + dtb_ref[...]
    softplus = jnp.maximum(z, 0.0) + jnp.log(1.0 + jnp.exp(-jnp.abs(z)))
    g = -jnp.exp(alog_ref[...]) * softplus
    lane = lax.broadcasted_iota(jnp.int32, hab.shape, 1)
    gb_ref[...] = jnp.where(lane < GDN_HEADS, g, _sigmoid(hab))


def _gdn_in(x2, w_in, conv_w, a_log, dt_bias, tm, seq_tiles):
    n, d = x2.shape
    qk_w = 2 * GDN_HEADS * GDN_DK
    v_w = GDN_HEADS * GDN_DV
    main = qk_w + 2 * v_w
    w_main = w_in[:, :main].astype(BF16)
    w_ab = jnp.pad(w_in[:, main:], ((0, 0), (0, LANES - 2 * GDN_HEADS))).astype(BF16)
    pad = (0, LANES - GDN_HEADS)
    alog = jnp.pad(a_log.astype(F32), pad).reshape(1, LANES)
    dtb = jnp.pad(dt_bias.astype(F32), pad).reshape(1, LANES)
    kern = functools.partial(_gdn_in_kernel, seq_tiles=seq_tiles, n_qk=qk_w // MXU_N,
                             n_v=v_w // MXU_N, n_gate=v_w // MXU_N)
    row = lambda w: pl.BlockSpec((tm, w), lambda i: (i, 0))
    return pl.pallas_call(
        kern,
        grid=(n // tm,),
        in_specs=[row(d), _halo_spec(tm, d), _const_spec((d, main)), _const_spec((d, LANES)),
                  _const_spec((GDN_CONV, qk_w + v_w)), _const_spec((1, LANES)),
                  _const_spec((1, LANES))],
        out_specs=[row(qk_w + v_w), row(v_w), row(LANES)],
        out_shape=[jax.ShapeDtypeStruct((n, qk_w + v_w), BF16),
                   jax.ShapeDtypeStruct((n, v_w), BF16),
                   jax.ShapeDtypeStruct((n, LANES), F32)],
        scratch_shapes=[pltpu.VMEM((SUBLANES + tm, d), BF16)],
        compiler_params=_params(1),
        name="gdn_in",
    )(x2, x2, w_main, w_ab, conv_w.astype(F32), alog, dtb)


def _unit_lower_inverses(a_list, eye, blk16, off32, off64):
    c = GDN_CHUNK
    n = [jnp.where(blk16, -a, 0.0) for a in a_list]
    p = [eye + x for x in n]
    nb = [x.astype(BF16) for x in n]
    n = [_dot(x, x) for x in nb]
    for _ in range(2):
        nb = [x.astype(BF16) for x in n]
        r = [_dot(jnp.concatenate([pi.astype(BF16), ni], axis=0), ni) for pi, ni in zip(p, nb)]
        p = [pi + ri[:c] for pi, ri in zip(p, r)]
        n = [ri[c:] for ri in r]
    p = [pi + _dot(pi.astype(BF16), ni.astype(BF16)) for pi, ni in zip(p, n)]
    for off in (off32, off64):
        pb = [pi.astype(BF16) for pi in p]
        t = [_dot(jnp.where(off, a, 0.0).astype(BF16), pbi) for a, pbi in zip(a_list, pb)]
        p = [pi - _dot(pbi, ti.astype(BF16)) for pi, pbi, ti in zip(p, pb, t)]
    return p


def _gdn_chunk_kernel(q_ref, k_ref, v_ref, gate_ref, g_ref, gt_ref, beta_ref, nw_ref,
                      o_ref, s_ref, wq_ref, u_ref, ak_ref, *, n_sub):
    c = GDN_CHUNK

    @pl.when(pl.program_id(1) == 0)
    def _():
        s_ref[...] = jnp.zeros_like(s_ref)

    step = n_sub * c
    row = lax.broadcasted_iota(jnp.int32, (step, step), 0)
    col = lax.broadcasted_iota(jnp.int32, (step, step), 1)
    same = (row // c) == (col // c)
    tri = jnp.where(same & (col <= row), 1.0, 0.0)
    tri_t = jnp.where(same & (row <= col), 1.0, 0.0)
    gc_cols = jnp.dot(tri, g_ref[0], precision=lax.Precision.HIGHEST, preferred_element_type=F32)
    gc_rows = jnp.dot(gt_ref[0], tri_t, precision=lax.Precision.HIGHEST, preferred_element_type=F32)
    beta = beta_ref[0]
    egc_cols = jnp.exp(gc_cols)

    r = lax.broadcasted_iota(jnp.int32, (c, c), 0)
    s = lax.broadcasted_iota(jnp.int32, (c, c), 1)
    causal = s <= r
    strict = s < r
    eye = jnp.where(r == s, 1.0, 0.0)
    blk16 = (r // 16) == (s // 16)
    off32 = ((r // 32) == (s // 32)) & ((r // 16) != (s // 16))
    off64 = (r // 32) != (s // 32)
    scale = GDN_DK ** -0.5
    heads = range(GDN_HEADS)
    hcols = [slice(h * GDN_DK, (h + 1) * GDN_DK) for h in heads]

    for wave in range(0, n_sub, GDN_WAVE):
        pairs = [(sub, h) for sub in range(wave, min(wave + GDN_WAVE, n_sub)) for h in heads]
        rws = [slice(sub * c, (sub + 1) * c) for sub, _ in pairs]
        qs = [q_ref[0, rw, hcols[h]] for rw, (_, h) in zip(rws, pairs)]
        ks = [k_ref[0, rw, hcols[h]] for rw, (_, h) in zip(rws, pairs)]
        qkk = [_dot_nt(jnp.concatenate([q, k], axis=0), k) for q, k in zip(qs, ks)]
        gcc = [gc_cols[rw, h:h + 1] for rw, (_, h) in zip(rws, pairs)]
        bcol = [beta[rw, h:h + 1] for rw, (_, h) in zip(rws, pairs)]
        egc = [egc_cols[rw, h:h + 1] for rw, (_, h) in zip(rws, pairs)]
        decay = []
        for i, (rw, (_, h)) in enumerate(zip(rws, pairs)):
            diff = jnp.where(causal, gcc[i] - gc_rows[h:h + 1, rw], 0.0)
            decay.append(jnp.where(causal, jnp.exp(diff), 0.0))
        a_list = [jnp.where(strict, x[c:] * dc * bc, 0.0) for x, dc, bc in zip(qkk, decay, bcol)]
        attn = [(x[:c] * scale * dc).astype(BF16) for x, dc in zip(qkk, decay)]
        inv = _unit_lower_inverses(a_list, eye, blk16, off32, off64)
        kf = [k.astype(F32) for k in ks]
        rhs = [jnp.concatenate([v_ref[0, rw, hcols[h]].astype(F32) * bcol[i],
                                kf[i] * (bcol[i] * egc[i])], axis=1).astype(BF16)
               for i, (rw, (_, h)) in enumerate(zip(rws, pairs))]
        sol = [_dot(t.astype(BF16), x) for t, x in zip(inv, rhs)]
        for i, (sub, h) in enumerate(pairs):
            g_last = gc_cols[(sub + 1) * c - 1:(sub + 1) * c, h:h + 1]
            u_ref[sub, h] = sol[i][:, :GDN_DV]
            qd = (qs[i].astype(F32) * (egc[i] * scale)).astype(BF16)
            wq_ref[sub, h] = jnp.concatenate([sol[i][:, GDN_DV:].astype(BF16), qd], axis=0)
            kd = kf[i] * jnp.exp(g_last - gcc[i])
            ak_ref[sub, h] = jnp.concatenate([attn[i], kd.T.astype(BF16)], axis=0)

    nw = nw_ref[...]
    for sub in range(n_sub):
        rows = slice(sub * c, (sub + 1) * c)
        g_last = gc_cols[(sub + 1) * c - 1:(sub + 1) * c, :]
        st = [s_ref[h] for h in heads]
        r1 = [_dot(wq_ref[sub, h], st[h].astype(BF16)) for h in heads]
        vb = [(u_ref[sub, h] - r1[h][:c]).astype(BF16) for h in heads]
        r2 = [_dot(ak_ref[sub, h], vb[h]) for h in heads]
        for h in heads:
            s_ref[h] = st[h] * jnp.exp(g_last[:, h:h + 1]) + r2[h][c:]
            o = r1[h][c:] + r2[h][:c]
            ms = jnp.mean(o * o, axis=-1, keepdims=True)
            on = o * lax.rsqrt(ms + RMS_EPS) * nw
            gt = gate_ref[0, rows, hcols[h]].astype(F32)
            o_ref[0, rows, hcols[h]] = (on * _silu(gt)).astype(BF16)


def _gdn_chunk(qkv, gate, g, beta, norm_w, b, t):
    hq = GDN_HEADS * GDN_DK
    hv = GDN_HEADS * GDN_DV
    step = GDN_STEP if t % GDN_STEP == 0 else t
    qkv3 = qkv.reshape(b, t, 2 * hq + hv)
    gate3 = gate.reshape(b, t, hv)
    g3 = g.reshape(b, t, GDN_HEADS)
    gt3 = jnp.swapaxes(g3, 1, 2)
    beta3 = beta.reshape(b, t, GDN_HEADS)
    n_sub = step // GDN_CHUNK
    kern = functools.partial(_gdn_chunk_kernel, n_sub=n_sub)
    col = lambda j: pl.BlockSpec((1, step, hq), lambda bi, ci: (bi, ci, j))
    small = pl.BlockSpec((1, step, GDN_HEADS), lambda bi, ci: (bi, ci, 0))
    out = pl.pallas_call(
        kern,
        grid=(b, t // step),
        in_specs=[col(0), col(1), col(2), col(0), small,
                  pl.BlockSpec((1, GDN_HEADS, step), lambda bi, ci: (bi, 0, ci)), small,
                  _const_spec((1, GDN_DV))],
        out_specs=col(0),
        out_shape=jax.ShapeDtypeStruct((b, t, hv), BF16),
        scratch_shapes=[pltpu.VMEM((GDN_HEADS, GDN_DK, GDN_DV), F32),
                        pltpu.VMEM((n_sub, GDN_HEADS, 2 * GDN_CHUNK, GDN_DK), BF16),
                        pltpu.VMEM((n_sub, GDN_HEADS, GDN_CHUNK, GDN_DV), F32),
                        pltpu.VMEM((n_sub, GDN_HEADS, GDN_CHUNK + GDN_DK, GDN_CHUNK), BF16)],
        compiler_params=_params(2),
        name="gdn_chunk",
    )(qkv3, qkv3, qkv3, gate3, g3, gt3, beta3, norm_w.astype(F32).reshape(1, GDN_DV))
    return out.reshape(b * t, hv)


def _proj_ln_kernel(y_ref, w_ref, x_ref, g_ref, b_ref, o_ref):
    m = _dot(y_ref[...], w_ref[...])
    z = DEEPNORM_ALPHA * x_ref[...] + m
    o_ref[...] = _layer_norm_rows(z, g_ref[...], b_ref[...])


def _proj_ln(y, w, x2, g, b, tm):
    n, d = x2.shape
    k = y.shape[1]
    row = lambda w_: pl.BlockSpec((tm, w_), lambda i: (i, 0))
    return pl.pallas_call(
        _proj_ln_kernel,
        grid=(n // tm,),
        in_specs=[row(k), _const_spec((k, d)), row(d), _const_spec((1, d)), _const_spec((1, d))],
        out_specs=row(d),
        out_shape=jax.ShapeDtypeStruct((n, d), F32),
        compiler_params=_params(1),
        name="proj_ln",
    )(y, w.astype(BF16), x2, g.astype(F32).reshape(1, d), b.astype(F32).reshape(1, d))


def _ffn_kernel(x_ref, xh_ref, wup_ref, cw_ref, cb_ref, wdn_ref, g_ref, b_ref, o_ref,
                xb_ref, act_ref, *, seq_tiles, hidden):
    _stage_rows(x_ref, xh_ref, xb_ref, seq_tiles)
    xb = xb_ref[...]
    n_chunks = hidden // MXU_N

    def project(c):
        return [_dot(xb, wup_ref[:, base + c * MXU_N:base + (c + 1) * MXU_N]) for base in (0, hidden)]

    def finish(c, hs):
        halves = []
        for base, h in zip((0, hidden), hs):
            cols = slice(base + c * MXU_N, base + (c + 1) * MXU_N)
            halves.append(_causal_conv(h, cw_ref[:, cols], FFN_CONV) + cb_ref[:, cols])
        act_ref[:, c * MXU_N:(c + 1) * MXU_N] = (_silu(halves[0]) * halves[1]).astype(BF16)

    hs_next = project(0)
    for c in range(n_chunks):
        hs = hs_next
        if c + 1 < n_chunks:
            hs_next = project(c + 1)
        finish(c, hs)
    f = _dot(act_ref[...], wdn_ref[...])
    z = DEEPNORM_ALPHA * x_ref[...] + f
    o_ref[...] = _layer_norm_rows(z, g_ref[...], b_ref[...])


def _ffn(x2, w_up, conv_w, conv_b, w_down, g, b, tm, seq_tiles):
    n, d = x2.shape
    hidden = w_down.shape[0]
    kern = functools.partial(_ffn_kernel, seq_tiles=seq_tiles, hidden=hidden)
    row = pl.BlockSpec((tm, d), lambda i: (i, 0))
    single = lambda shape: pl.BlockSpec(shape, lambda *_: (0,) * len(shape),
                                        pipeline_mode=pl.Buffered(1))
    return pl.pallas_call(
        kern,
        grid=(n // tm,),
        in_specs=[row, _halo_spec(tm, d), single((d, 2 * hidden)),
                  _const_spec((FFN_CONV, 2 * hidden)), _const_spec((1, 2 * hidden)),
                  single((hidden, d)), _const_spec((1, d)), _const_spec((1, d))],
        out_specs=row,
        out_shape=jax.ShapeDtypeStruct((n, d), F32),
        scratch_shapes=[pltpu.VMEM((SUBLANES + tm, d), BF16), pltpu.VMEM((tm, hidden), BF16)],
        compiler_params=_params(1),
        name="conv_ffn",
    )(x2, x2, w_up.astype(BF16), conv_w.astype(F32), conv_b.astype(F32).reshape(1, 2 * hidden),
      w_down.astype(BF16), g.astype(F32).reshape(1, d), b.astype(F32).reshape(1, d))


def _diff_in_kernel(x_ref, w_ref, c_ref, s1_ref, s2_ref, qk_ref, v_ref, *, n_qk, n_v, q_cols):
    xb = x_ref[...].astype(BF16)
    cos = c_ref[...]
    s1 = s1_ref[...]
    s2 = s2_ref[...]
    half = DIFF_HD // 2
    for c in range(n_qk):
        h = _dot(xb, w_ref[:, c * MXU_N:(c + 1) * MXU_N])
        for j in range(MXU_N // LANES):
            hs = h[:, j * LANES:(j + 1) * LANES]
            rot = hs * cos + pltpu.roll(hs, LANES - half, 1) * s1 + pltpu.roll(hs, half, 1) * s2
            lo = c * MXU_N + j * LANES
            if lo < q_cols:
                rot = rot * (DIFF_HD ** -0.5)
            qk_ref[:, lo:lo + LANES] = rot.astype(BF16)
    base = n_qk * MXU_N
    for c in range(n_v):
        v_ref[:, c * MXU_N:(c + 1) * MXU_N] = _dot(
            xb, w_ref[:, base + c * MXU_N:base + (c + 1) * MXU_N]).astype(BF16)


def _diff_in(x2, w_in, tables, tm):
    n, d = x2.shape
    hw = DIFF_HEADS * 2 * DIFF_HD
    kern = functools.partial(_diff_in_kernel, n_qk=2 * hw // MXU_N, n_v=hw // MXU_N, q_cols=hw)
    row = lambda w: pl.BlockSpec((tm, w), lambda i: (i, 0))
    return pl.pallas_call(
        kern,
        grid=(n // tm,),
        in_specs=[row(d), _const_spec((d, 3 * hw)), row(LANES), row(LANES), row(LANES)],
        out_specs=[row(2 * hw), row(hw)],
        out_shape=[jax.ShapeDtypeStruct((n, 2 * hw), BF16), jax.ShapeDtypeStruct((n, hw), BF16)],
        compiler_params=_params(1),
        name="diff_in",
    )(x2, w_in.astype(BF16), *tables)


def _diff_attn_kernel(q_ref, k_ref, v_ref, lq1_ref, lk1_ref, lq2_ref, lk2_ref, sw_ref, o_ref,
                      *, t, blk, lambda_init):
    lam = (jnp.exp(jnp.sum(lq1_ref[...] * lk1_ref[...], axis=-1, keepdims=True))
           - jnp.exp(jnp.sum(lq2_ref[...] * lk2_ref[...], axis=-1, keepdims=True))
           + lambda_init)
    lane = lax.broadcasted_iota(jnp.int32, (blk, 2 * DIFF_HD), 1)
    first_map = lane < DIFF_HD

    def causal_mask(width):
        r = lax.broadcasted_iota(jnp.int32, (2 * blk, width), 0)
        s = lax.broadcasted_iota(jnp.int32, (2 * blk, width), 1)
        return s - (width - blk) <= (r % blk)

    sw = sw_ref[...]

    def scores(qi):
        q = q_ref[0, qi * blk:(qi + 1) * blk, :]
        zero = jnp.zeros_like(q)
        qs = jnp.concatenate([jnp.where(first_map, q, zero), jnp.where(first_map, zero, q)], axis=0)
        d0 = qi * blk
        sc_d = jnp.where(causal_mask(blk), _dot_nt(qs, k_ref[0, d0:d0 + blk, :]), -jnp.inf)
        sc_f = _dot_nt(qs, k_ref[0, 0:d0, :]) if qi > 0 else None
        return sc_d, sc_f

    def weighted_values(qi, p_d, p_f, l):
        d0 = qi * blk
        acc = _dot(p_d, v_ref[0, d0:d0 + blk, :])
        if qi > 0:
            acc = acc + _dot(p_f, v_ref[0, 0:d0, :])
        on = acc / l
        o = on[:blk] - lam * on[blk:]
        ms = jnp.mean(o * o, axis=-1, keepdims=True)
        o = o * lax.rsqrt(ms + RMS_EPS) * sw * (1.0 - lambda_init)
        o_ref[0, qi * blk:(qi + 1) * blk, :] = o.astype(BF16)

    n_q = t // blk
    nxt = scores(0)
    pending = None
    for qi in range(n_q):
        sc_d, sc_f = nxt
        if pending is not None:
            weighted_values(*pending)
        if qi + 1 < n_q:
            nxt = scores(qi + 1)
        m = jnp.max(sc_d, axis=-1, keepdims=True)
        if qi > 0:
            m = jnp.maximum(m, jnp.max(sc_f, axis=-1, keepdims=True))
        p_d = jnp.exp(sc_d - m)
        l = jnp.sum(p_d, axis=-1, keepdims=True)
        p_f = None
        if qi > 0:
            p_f = jnp.exp(sc_f - m)
            l = l + jnp.sum(p_f, axis=-1, keepdims=True)
            p_f = p_f.astype(BF16)
        pending = (qi, p_d.astype(BF16), p_f, l)
    weighted_values(*pending)


def _diff_attn(qk, v, lam_q1, lam_k1, lam_q2, lam_k2, subln_w, lambda_init, b, t):
    hw = DIFF_HEADS * 2 * DIFF_HD
    hd2 = 2 * DIFF_HD
    blk = ATTN_BLOCK if t % ATTN_BLOCK == 0 else t
    qk3 = qk.reshape(b, t, 2 * hw)
    v3 = v.reshape(b, t, hw)
    kern = functools.partial(_diff_attn_kernel, t=t, blk=blk, lambda_init=lambda_init)
    head = lambda off: pl.BlockSpec((1, t, hd2), lambda bi, hi: (bi, 0, hi + off))
    vec = lambda a: a.astype(F32).reshape(1, -1)
    out = pl.pallas_call(
        kern,
        grid=(b, DIFF_HEADS),
        in_specs=[head(0), head(DIFF_HEADS), head(0)] + [_const_spec((1, DIFF_HD))] * 4
                 + [_const_spec((1, hd2))],
        out_specs=head(0),
        out_shape=jax.ShapeDtypeStruct((b, t, hw), BF16),
        compiler_params=_params(2),
        name="diff_attn",
    )(qk3, qk3, v3, vec(lam_q1), vec(lam_k1), vec(lam_q2), vec(lam_k2), vec(subln_w))
    return out.reshape(b * t, hw)


def kernel(x, positions, gdn_w_in, gdn_conv_w, gdn_a_log, gdn_dt_bias, gdn_norm_w, gdn_w_out,
           diff_w_in, diff_lam_q1, diff_lam_k1, diff_lam_q2, diff_lam_k2, diff_subln_w, diff_w_out,
           ffn_w_up, ffn_conv_w, ffn_conv_b, ffn_w_down, ln_mix_g, ln_mix_b, ln_ffn_g, ln_ffn_b):
    b, t, d = x.shape
    tm = _row_tile(t)
    seq_tiles = t // tm
    x2 = x.reshape(b * t, d).astype(F32)
    tables = _rope_tables(positions, tm)
    for i in range(DEPTH):
        j = i // 2
        if i % 2 == 0:
            qkv, gate, gb = _gdn_in(x2, gdn_w_in[j], gdn_conv_w[j], gdn_a_log[j], gdn_dt_bias[j],
                                    tm, seq_tiles)
            g = gb[:, :GDN_HEADS]
            beta = gb[:, GDN_HEADS:2 * GDN_HEADS]
            mixed = _gdn_chunk(qkv, gate, g, beta, gdn_norm_w[j], b, t)
            w_out = gdn_w_out[j]
        else:
            lambda_init = 0.8 - 0.6 * math.exp(-0.3 * i)
            qk, v = _diff_in(x2, diff_w_in[j], tables, tm)
            mixed = _diff_attn(qk, v, diff_lam_q1[j], diff_lam_k1[j], diff_lam_q2[j], diff_lam_k2[j],
                               diff_subln_w[j], lambda_init, b, t)
            w_out = diff_w_out[j]
        x2 = _proj_ln(mixed, w_out, x2, ln_mix_g[i], ln_mix_b[i], tm)
        x2 = _ffn(x2, ffn_w_up[i], ffn_conv_w[i], ffn_conv_b[i], ffn_w_down[i],
                  ln_ffn_g[i], ln_ffn_b[i], tm, seq_tiles)
    return x2.reshape(b, t, d)
```

```python
import functools
import math

import jax
import jax.numpy as jnp
from jax import lax
from jax.experimental import pallas as pl
from jax.experimental.pallas import tpu as pltpu

F32 = jnp.float32
BF16 = jnp.bfloat16

DEPTH = 4
GDN_HEADS = 8
GDN_DK = 128
GDN_DV = 128
GDN_CONV = 4
GDN_CHUNK = 64
DIFF_HEADS = 8
DIFF_HD = 64
ROPE_THETA = 10000.0
FFN_CONV = 3
DEEPNORM_ALPHA = (2.0 * DEPTH) ** 0.25
LN_EPS = 1e-5
RMS_EPS = 1e-6

LANES = 128
SUBLANES = 8
MXU_N = 256
VMEM_LIMIT = 56 * 1024 * 1024

ROW_TILE = 512
PROJ_TILE = 1024
ATTN_HEADS_PER_STEP = 1
ATTN_LEAD = 1
ATTN_BLOCK = 256
GDN_STEP = 256


def _row_tile(t):
    return ROW_TILE if t % ROW_TILE == 0 else t


def _sigmoid(x):
    return 1.0 / (1.0 + jnp.exp(-x))


def _silu(x):
    return x * _sigmoid(x)


def _dot(a, b):
    return jnp.dot(a, b, preferred_element_type=F32)


def _dot_nt(a, b):
    return lax.dot_general(a, b, (((1,), (1,)), ((), ())), preferred_element_type=F32)


def _dot_tn(a, b):
    return lax.dot_general(a, b, (((0,), (0,)), ((), ())), preferred_element_type=F32)


def _shift_rows(h, s):
    return pltpu.roll(h, s, 0)


def _causal_conv(h, cw, width):
    y = h * cw[width - 1:width, :]
    for s in range(1, width):
        y = y + _shift_rows(h, s) * cw[width - 1 - s:width - s, :]
    return y[SUBLANES:, :]


def _layer_norm_rows(z, g, b):
    mu = jnp.mean(z, axis=-1, keepdims=True)
    zc = z - mu
    var = jnp.mean(zc * zc, axis=-1, keepdims=True)
    return zc * lax.rsqrt(var + LN_EPS) * g + b


def _stage_rows(x_ref, xh_ref, xb_ref, seq_tiles):
    first = (pl.program_id(0) % seq_tiles) == 0
    halo = jnp.where(first, 0.0, xh_ref[...])
    xb_ref[0:SUBLANES, :] = halo.astype(BF16)
    xb_ref[SUBLANES:, :] = x_ref[...].astype(BF16)


def _halo_spec(tm, d):
    return pl.BlockSpec((SUBLANES, d), lambda i: (jnp.maximum(i * (tm // SUBLANES) - 1, 0), 0))


def _const_spec(shape):
    return pl.BlockSpec(shape, lambda *_: (0,) * len(shape))


def _params(n_axes):
    return pltpu.CompilerParams(
        dimension_semantics=("arbitrary",) * n_axes, vmem_limit_bytes=VMEM_LIMIT)


def _rope_kernel(pos_ref, inv_ref, c_ref, s1_ref, s2_ref):
    ang = pos_ref[...].astype(F32) * inv_ref[...]
    cos = jnp.cos(ang)
    sin = jnp.sin(ang)
    lane = lax.broadcasted_iota(jnp.int32, ang.shape, 1)
    lower = (lane % DIFF_HD) < (DIFF_HD // 2)
    c_ref[...] = cos
    s1_ref[...] = jnp.where(lower, -sin, 0.0)
    s2_ref[...] = jnp.where(lower, 0.0, sin)


def _rope_tables(positions, tm):
    n = positions.size
    inv_freq = ROPE_THETA ** (-jnp.arange(0, DIFF_HD, 2, dtype=F32) / DIFF_HD)
    inv = jnp.tile(inv_freq, LANES // (DIFF_HD // 2)).reshape(1, LANES)
    pos = positions.reshape(n, 1)
    out = jax.ShapeDtypeStruct((n, LANES), F32)
    return pl.pallas_call(
        _rope_kernel,
        grid=(n // tm,),
        in_specs=[pl.BlockSpec((tm, 1), lambda i: (i, 0)), _const_spec((1, LANES))],
        out_specs=[pl.BlockSpec((tm, LANES), lambda i: (i, 0))] * 3,
        out_shape=[out] * 3,
        compiler_params=_params(1),
        name="rope_tables",
    )(pos, inv)


def _gdn_in_kernel(x_ref, xh_ref, w_ref, wab_ref, cw_ref, alog_ref, dtb_ref,
                   qkv_ref, gate_ref, gb_ref, xb_ref, *, seq_tiles, n_qk, n_v, n_gate):
    _stage_rows(x_ref, xh_ref, xb_ref, seq_tiles)
    xb = xb_ref[...]
    xt = xb[SUBLANES:, :]
    base = (n_qk + n_v) * MXU_N
    conv_per_gate = (n_qk + n_v) // n_gate
    jobs = []
    for c in range(n_qk + n_v):
        jobs.append(("conv", c))
        if c % conv_per_gate == conv_per_gate - 1:
            jobs.append(("gate", c // conv_per_gate))

    def project(job):
        kind, c = job
        if kind == "conv":
            return _dot(xb, w_ref[:, c * MXU_N:(c + 1) * MXU_N])
        return _dot(xt, w_ref[:, base + c * MXU_N:base + (c + 1) * MXU_N])

    def finish(job, h):
        kind, c = job
        cols = slice(c * MXU_N, (c + 1) * MXU_N)
        if kind == "gate":
            gate_ref[:, cols] = h.astype(BF16)
            return
        y = _silu(_causal_conv(h, cw_ref[:, cols], GDN_CONV))
        if c < n_qk:
            parts = []
            for j in range(MXU_N // GDN_DK):
                yh = y[:, j * GDN_DK:(j + 1) * GDN_DK]
                ss = jnp.sum(yh * yh, axis=-1, keepdims=True)
                parts.append(yh * lax.rsqrt(ss + RMS_EPS))
            y = jnp.concatenate(parts, axis=1)
        qkv_ref[:, cols] = y.astype(BF16)

    for job in jobs:
        finish(job, project(job))
    hab = _dot(xt, wab_ref[...])
    z = hab + dtb_ref[...]
    softplus = jnp.maximum(z, 0.0) + jnp.log(1.0 + jnp.exp(-jnp.abs(z)))
    g = -jnp.exp(alog_ref[...]) * softplus
    lane = lax.broadcasted_iota(jnp.int32, hab.shape, 1)
    gb_ref[...] = jnp.where(lane < GDN_HEADS, g, _sigmoid(hab))


def _gdn_in(x2, w_in, conv_w, a_log, dt_bias, tm, seq_tiles):
    n, d = x2.shape
    qk_w = 2 * GDN_HEADS * GDN_DK
    v_w = GDN_HEADS * GDN_DV
    main = qk_w + 2 * v_w
    w_main = w_in[:, :main].astype(BF16)
    w_ab = jnp.pad(w_in[:, main:], ((0, 0), (0, LANES - 2 * GDN_HEADS))).astype(BF16)
    pad = (0, LANES - GDN_HEADS)
    alog = jnp.pad(a_log.astype(F32), pad).reshape(1, LANES)
    dtb = jnp.pad(dt_bias.astype(F32), pad).reshape(1, LANES)
    kern = functools.partial(_gdn_in_kernel, seq_tiles=seq_tiles, n_qk=qk_w // MXU_N,
                             n_v=v_w // MXU_N, n_gate=v_w // MXU_N)
    row = lambda w: pl.BlockSpec((tm, w), lambda i: (i, 0))
    return pl.pallas_call(
        kern,
        grid=(n // tm,),
        in_specs=[row(d), _halo_spec(tm, d), _const_spec((d, main)), _const_spec((d, LANES)),
                  _const_spec((GDN_CONV, qk_w + v_w)), _const_spec((1, LANES)),
                  _const_spec((1, LANES))],
        out_specs=[row(qk_w + v_w), row(v_w), row(LANES)],
        out_shape=[jax.ShapeDtypeStruct((n, qk_w + v_w), BF16),
                   jax.ShapeDtypeStruct((n, v_w), BF16),
                   jax.ShapeDtypeStruct((n, LANES), F32)],
        scratch_shapes=[pltpu.VMEM((SUBLANES + tm, d), BF16)],
        compiler_params=_params(1),
        name="gdn_in",
    )(x2, x2, w_main, w_ab, conv_w.astype(F32), alog, dtb)


def _unit_lower_inverses(a_list, eye, blk16, off32, off64, bdmask):
    c = GDN_CHUNK

    def bd(y):
        yy = jnp.concatenate([y, y], axis=0)
        return jnp.where(bdmask, yy, jnp.zeros_like(yy))

    n = [jnp.where(blk16, -a, 0.0) for a in a_list]
    p = [eye + x for x in n]
    nb = [x.astype(BF16) for x in n]
    n = [_dot(x, bd(x)) for x in nb]
    for _ in range(2):
        nb = [x.astype(BF16) for x in n]
        r = [_dot(jnp.concatenate([pi.astype(BF16), ni], axis=0), bd(ni)) for pi, ni in zip(p, nb)]
        p = [pi + ri[:c] for pi, ri in zip(p, r)]
        n = [ri[c:] for ri in r]
    p = [pi + _dot(pi.astype(BF16), bd(ni.astype(BF16))) for pi, ni in zip(p, n)]
    for off in (off32, off64):
        pb = [pi.astype(BF16) for pi in p]
        t = [_dot(jnp.where(off, a, 0.0).astype(BF16), bd(pbi)) for a, pbi in zip(a_list, pb)]
        p = [pi - _dot(pbi, bd(ti.astype(BF16))) for pi, pbi, ti in zip(p, pb, t)]
    return p


def _gdn_chunk_kernel(q_ref, k_ref, v_ref, gate_ref, g_ref, gt_ref, beta_ref, nw_ref,
                      o_ref, s_ref, wq_ref, u_ref, ak_ref, *, n_sub):
    c = GDN_CHUNK

    @pl.when(pl.program_id(1) == 0)
    def _():
        s_ref[...] = jnp.zeros_like(s_ref)

    step = n_sub * c
    row = lax.broadcasted_iota(jnp.int32, (step, step), 0)
    col = lax.broadcasted_iota(jnp.int32, (step, step), 1)
    same = (row // c) == (col // c)
    tri = jnp.where(same & (col <= row), 1.0, 0.0)
    tri_t = jnp.where(same & (row <= col), 1.0, 0.0)
    gc_cols = jnp.dot(tri, g_ref[0], precision=lax.Precision.HIGHEST, preferred_element_type=F32)
    gc_rows = jnp.dot(gt_ref[0], tri_t, precision=lax.Precision.HIGHEST, preferred_element_type=F32)
    beta = beta_ref[0]
    egc_cols = jnp.exp(gc_cols)

    c2 = 2 * c
    r = lax.broadcasted_iota(jnp.int32, (c, c2), 0)
    lane = lax.broadcasted_iota(jnp.int32, (c, c2), 1)
    s = lane % c
    first = lane < c
    causal = s <= r
    strict = s < r
    eye = jnp.where(r == s, 1.0, 0.0)
    blk16 = (r // 16) == (s // 16)
    off32 = ((r // 32) == (s // 32)) & ((r // 16) != (s // 16))
    off64 = (r // 32) != (s // 32)
    bdmask = (lax.broadcasted_iota(jnp.int32, (c2, c2), 0) // c
              == lax.broadcasted_iota(jnp.int32, (c2, c2), 1) // c)
    kmask = (lax.broadcasted_iota(jnp.int32, (c2, 2 * GDN_DK), 0) // c
             == lax.broadcasted_iota(jnp.int32, (c2, 2 * GDN_DK), 1) // GDN_DK)
    scale = GDN_DK ** -0.5
    heads = range(GDN_HEADS)
    hcols = [slice(h * GDN_DK, (h + 1) * GDN_DK) for h in heads]
    n_hp = GDN_HEADS // 2
    pcols = [slice(hp * 2 * GDN_DK, (hp + 1) * 2 * GDN_DK) for hp in range(n_hp)]

    def pack_cols(x, rw, hp):
        return jnp.where(first, jnp.broadcast_to(x[rw, 2 * hp:2 * hp + 1], (c, c2)),
                         jnp.broadcast_to(x[rw, 2 * hp + 1:2 * hp + 2], (c, c2)))

    units = [(sub, hp) for sub in range(n_sub) for hp in range(n_hp)]
    rws = [slice(sub * c, (sub + 1) * c) for sub, _ in units]
    q2 = [q_ref[0, rw, pcols[hp]] for rw, (_, hp) in zip(rws, units)]
    k2 = [k_ref[0, rw, pcols[hp]] for rw, (_, hp) in zip(rws, units)]
    kbd = []
    for k in k2:
        kk = jnp.concatenate([k, k], axis=0)
        kbd.append(jnp.where(kmask, kk, jnp.zeros_like(kk)))
    qkk = [_dot_nt(jnp.concatenate([q, k], axis=0), kb) for q, k, kb in zip(q2, k2, kbd)]
    decay = []
    for rw, (_, hp) in zip(rws, units):
        gcr = jnp.concatenate([gc_rows[2 * hp:2 * hp + 1, rw], gc_rows[2 * hp + 1:2 * hp + 2, rw]], axis=1)
        diff = jnp.where(causal, pack_cols(gc_cols, rw, hp) - gcr, 0.0)
        decay.append(jnp.where(causal, jnp.exp(diff), 0.0))
    a_list = [jnp.where(strict, x[c:] * dc * pack_cols(beta, rw, hp), 0.0)
              for x, dc, rw, (_, hp) in zip(qkk, decay, rws, units)]
    attn = [(x[:c] * scale * dc).astype(BF16) for x, dc in zip(qkk, decay)]
    inv = _unit_lower_inverses(a_list, eye, blk16, off32, off64, bdmask)
    zeros = jnp.zeros((c, GDN_DV + GDN_DK), BF16)
    rhs, kfs = [], []
    for rw, (_, hp) in zip(rws, units):
        both = []
        for h in (2 * hp, 2 * hp + 1):
            bcol = beta[rw, h:h + 1]
            kf = k_ref[0, rw, hcols[h]].astype(F32)
            kfs.append(kf)
            both.append(jnp.concatenate([v_ref[0, rw, hcols[h]].astype(F32) * bcol,
                                         kf * (bcol * egc_cols[rw, h:h + 1])], axis=1).astype(BF16))
        rhs.append(jnp.concatenate([jnp.concatenate([both[0], zeros], axis=1),
                                    jnp.concatenate([zeros, both[1]], axis=1)], axis=0))
    sol = [_dot(t.astype(BF16), x) for t, x in zip(inv, rhs)]
    for i, (rw, (sub, hp)) in enumerate(zip(rws, units)):
        kds = []
        for j, h in enumerate((2 * hp, 2 * hp + 1)):
            base = j * (GDN_DV + GDN_DK)
            g_last = gc_cols[(sub + 1) * c - 1:(sub + 1) * c, h:h + 1]
            u_ref[sub, h] = sol[i][:, base:base + GDN_DV]
            qd = (q_ref[0, rw, hcols[h]].astype(F32) * (egc_cols[rw, h:h + 1] * scale)).astype(BF16)
            wq_ref[sub, h] = jnp.concatenate(
                [sol[i][:, base + GDN_DV:base + GDN_DV + GDN_DK].astype(BF16), qd], axis=0)
            kds.append(kfs[2 * i + j] * jnp.exp(g_last - gc_cols[rw, h:h + 1]))
        kdt = jnp.concatenate(kds, axis=0).T
        ak_ref[sub, hp] = jnp.concatenate([attn[i], kdt.astype(BF16)], axis=0)

    nw = nw_ref[...]
    zv = jnp.zeros((c, GDN_DV), BF16)
    for sub in range(n_sub):
        rows = slice(sub * c, (sub + 1) * c)
        g_last = gc_cols[(sub + 1) * c - 1:(sub + 1) * c, :]
        st = [s_ref[h] for h in heads]
        r1 = [_dot(wq_ref[sub, h], st[h].astype(BF16)) for h in heads]
        vb = [(u_ref[sub, h] - r1[h][:c]).astype(BF16) for h in heads]
        r2p = []
        for hp in range(n_hp):
            w2 = jnp.concatenate([jnp.concatenate([vb[2 * hp], zv], axis=1),
                                  jnp.concatenate([zv, vb[2 * hp + 1]], axis=1)], axis=0)
            r2p.append(_dot(ak_ref[sub, hp], w2))
        r2 = [r2p[h // 2][:, (h % 2) * GDN_DV:(h % 2 + 1) * GDN_DV] for h in heads]
        for h in heads:
            s_ref[h] = st[h] * jnp.exp(g_last[:, h:h + 1]) + r2[h][c:]
            o = r1[h][c:] + r2[h][:c]
            ms = jnp.mean(o * o, axis=-1, keepdims=True)
            on = o * lax.rsqrt(ms + RMS_EPS) * nw
            gt = gate_ref[0, rows, hcols[h]].astype(F32)
            o_ref[0, rows, hcols[h]] = (on * _silu(gt)).astype(BF16)


def _gdn_chunk(qkv, gate, g, beta, norm_w, b, t):
    hq = GDN_HEADS * GDN_DK
    hv = GDN_HEADS * GDN_DV
    step = GDN_STEP if t % GDN_STEP == 0 else t
    qkv3 = qkv.reshape(b, t, 2 * hq + hv)
    gate3 = gate.reshape(b, t, hv)
    g3 = g.reshape(b, t, GDN_HEADS)
    gt3 = jnp.swapaxes(g3, 1, 2)
    beta3 = beta.reshape(b, t, GDN_HEADS)
    n_sub = step // GDN_CHUNK
    kern = functools.partial(_gdn_chunk_kernel, n_sub=n_sub)
    col = lambda j: pl.BlockSpec((1, step, hq), lambda bi, ci: (bi, ci, j))
    small = pl.BlockSpec((1, step, GDN_HEADS), lambda bi, ci: (bi, ci, 0))
    out = pl.pallas_call(
        kern,
        grid=(b, t // step),
        in_specs=[col(0), col(1), col(2), col(0), small,
                  pl.BlockSpec((1, GDN_HEADS, step), lambda bi, ci: (bi, 0, ci)), small,
                  _const_spec((1, GDN_DV))],
        out_specs=col(0),
        out_shape=jax.ShapeDtypeStruct((b, t, hv), BF16),
        scratch_shapes=[pltpu.VMEM((GDN_HEADS, GDN_DK, GDN_DV), F32),
                        pltpu.VMEM((n_sub, GDN_HEADS, 2 * GDN_CHUNK, GDN_DK), BF16),
                        pltpu.VMEM((n_sub, GDN_HEADS, GDN_CHUNK, GDN_DV), F32),
                        pltpu.VMEM((n_sub, GDN_HEADS // 2, GDN_CHUNK + GDN_DK, 2 * GDN_CHUNK), BF16)],
        compiler_params=_params(2),
        name="gdn_chunk",
    )(qkv3, qkv3, qkv3, gate3, g3, gt3, beta3, norm_w.astype(F32).reshape(1, GDN_DV))
    return out.reshape(b * t, hv)


def _proj_ln_kernel(y_ref, w_ref, x_ref, g_ref, b_ref, o_ref):
    m = _dot(y_ref[...], w_ref[...])
    z = DEEPNORM_ALPHA * x_ref[...] + m
    o_ref[...] = _layer_norm_rows(z, g_ref[...], b_ref[...])


def _proj_ln(y, w, x2, g, b, tm):
    n, d = x2.shape
    k = y.shape[1]
    row = lambda w_: pl.BlockSpec((tm, w_), lambda i: (i, 0))
    return pl.pallas_call(
        _proj_ln_kernel,
        grid=(n // tm,),
        in_specs=[row(k), _const_spec((k, d)), row(d), _const_spec((1, d)), _const_spec((1, d))],
        out_specs=row(d),
        out_shape=jax.ShapeDtypeStruct((n, d), F32),
        compiler_params=_params(1),
        name="proj_ln",
    )(y, w.astype(BF16), x2, g.astype(F32).reshape(1, d), b.astype(F32).reshape(1, d))


def _ffn_kernel(x_ref, xh_ref, wup_ref, cw_ref, cb_ref, wdn_ref, g_ref, b_ref, o_ref,
                xb_ref, act_ref, *, seq_tiles, hidden):
    _stage_rows(x_ref, xh_ref, xb_ref, seq_tiles)
    xb = xb_ref[...]
    n_chunks = hidden // MXU_N

    def project(c):
        return [_dot(xb, wup_ref[:, base + c * MXU_N:base + (c + 1) * MXU_N]) for base in (0, hidden)]

    def finish(c, hs):
        halves = []
        for base, h in zip((0, hidden), hs):
            cols = slice(base + c * MXU_N, base + (c + 1) * MXU_N)
            halves.append(_causal_conv(h, cw_ref[:, cols], FFN_CONV) + cb_ref[:, cols])
        act_ref[:, c * MXU_N:(c + 1) * MXU_N] = (_silu(halves[0]) * halves[1]).astype(BF16)

    for c in range(n_chunks):
        finish(c, project(c))
    f = _dot(act_ref[...], wdn_ref[...])
    z = DEEPNORM_ALPHA * x_ref[...] + f
    o_ref[...] = _layer_norm_rows(z, g_ref[...], b_ref[...])


def _ffn(x2, w_up, conv_w, conv_b, w_down, g, b, tm, seq_tiles):
    n, d = x2.shape
    hidden = w_down.shape[0]
    kern = functools.partial(_ffn_kernel, seq_tiles=seq_tiles, hidden=hidden)
    row = pl.BlockSpec((tm, d), lambda i: (i, 0))
    single = lambda shape: pl.BlockSpec(shape, lambda *_: (0,) * len(shape),
                                        pipeline_mode=pl.Buffered(1))
    return pl.pallas_call(
        kern,
        grid=(n // tm,),
        in_specs=[row, _halo_spec(tm, d), single((d, 2 * hidden)),
                  _const_spec((FFN_CONV, 2 * hidden)), _const_spec((1, 2 * hidden)),
                  single((hidden, d)), _const_spec((1, d)), _const_spec((1, d))],
        out_specs=row,
        out_shape=jax.ShapeDtypeStruct((n, d), F32),
        scratch_shapes=[pltpu.VMEM((SUBLANES + tm, d), BF16), pltpu.VMEM((tm, hidden), BF16)],
        compiler_params=_params(1),
        name="conv_ffn",
    )(x2, x2, w_up.astype(BF16), conv_w.astype(F32), conv_b.astype(F32).reshape(1, 2 * hidden),
      w_down.astype(BF16), g.astype(F32).reshape(1, d), b.astype(F32).reshape(1, d))


def _diff_in_kernel(x_ref, w_ref, c_ref, s1_ref, s2_ref, qk_ref, v_ref, *, n_qk, n_v, q_cols):
    xb = x_ref[...].astype(BF16)
    cos = c_ref[...]
    s1 = s1_ref[...]
    s2 = s2_ref[...]
    half = DIFF_HD // 2
    for c in range(n_qk):
        h = _dot(xb, w_ref[:, c * MXU_N:(c + 1) * MXU_N])
        for j in range(MXU_N // LANES):
            hs = h[:, j * LANES:(j + 1) * LANES]
            rot = hs * cos + pltpu.roll(hs, LANES - half, 1) * s1 + pltpu.roll(hs, half, 1) * s2
            lo = c * MXU_N + j * LANES
            if lo < q_cols:
                rot = rot * (DIFF_HD ** -0.5)
            qk_ref[:, lo:lo + LANES] = rot.astype(BF16)
    base = n_qk * MXU_N
    for c in range(n_v):
        v_ref[:, c * MXU_N:(c + 1) * MXU_N] = _dot(
            xb, w_ref[:, base + c * MXU_N:base + (c + 1) * MXU_N]).astype(BF16)


def _diff_in(x2, w_in, tables, tm):
    n, d = x2.shape
    hw = DIFF_HEADS * 2 * DIFF_HD
    kern = functools.partial(_diff_in_kernel, n_qk=2 * hw // MXU_N, n_v=hw // MXU_N, q_cols=hw)
    row = lambda w: pl.BlockSpec((tm, w), lambda i: (i, 0))
    return pl.pallas_call(
        kern,
        grid=(n // tm,),
        in_specs=[row(d), _const_spec((d, 3 * hw)), row(LANES), row(LANES), row(LANES)],
        out_specs=[row(2 * hw), row(hw)],
        out_shape=[jax.ShapeDtypeStruct((n, 2 * hw), BF16), jax.ShapeDtypeStruct((n, hw), BF16)],
        compiler_params=_params(1),
        name="diff_in",
    )(x2, w_in.astype(BF16), *tables)


def _diff_attn_kernel(q_ref, k_ref, v_ref, lq1_ref, lk1_ref, lq2_ref, lk2_ref, sw_ref, o_ref,
                      *, t, blk, n_heads, lambda_init):
    lam = (jnp.exp(jnp.sum(lq1_ref[...] * lk1_ref[...], axis=-1, keepdims=True))
           - jnp.exp(jnp.sum(lq2_ref[...] * lk2_ref[...], axis=-1, keepdims=True))
           + lambda_init)
    lane = lax.broadcasted_iota(jnp.int32, (blk, 2 * DIFF_HD), 1)
    first_map = lane < DIFF_HD

    def causal_mask(width):
        r = lax.broadcasted_iota(jnp.int32, (2 * blk, width), 0)
        s = lax.broadcasted_iota(jnp.int32, (2 * blk, width), 1)
        return s - (width - blk) <= (r % blk)

    sw = sw_ref[...]

    def scores(hh, qi):
        cols = slice(hh * 2 * DIFF_HD, (hh + 1) * 2 * DIFF_HD)
        q = q_ref[0, qi * blk:(qi + 1) * blk, cols]
        zero = jnp.zeros_like(q)
        qs = jnp.concatenate([jnp.where(first_map, q, zero), jnp.where(first_map, zero, q)], axis=0)
        d0 = qi * blk
        sc_d = jnp.where(causal_mask(blk), _dot_nt(qs, k_ref[0, d0:d0 + blk, cols]), -jnp.inf)
        sc_f = _dot_nt(qs, k_ref[0, 0:d0, cols]) if qi > 0 else None
        return sc_d, sc_f

    def weighted_values(hh, qi, p_d, p_f, l):
        cols = slice(hh * 2 * DIFF_HD, (hh + 1) * 2 * DIFF_HD)
        d0 = qi * blk
        acc = _dot(p_d, v_ref[0, d0:d0 + blk, cols])
        if qi > 0:
            acc = acc + _dot(p_f, v_ref[0, 0:d0, cols])
        on = acc / l
        o = on[:blk] - lam * on[blk:]
        ms = jnp.mean(o * o, axis=-1, keepdims=True)
        o = o * lax.rsqrt(ms + RMS_EPS) * sw * (1.0 - lambda_init)
        o_ref[0, qi * blk:(qi + 1) * blk, cols] = o.astype(BF16)

    def softmax(qi, sc_d, sc_f):
        m = jnp.max(sc_d, axis=-1, keepdims=True)
        if qi > 0:
            m = jnp.maximum(m, jnp.max(sc_f, axis=-1, keepdims=True))
        p_d = jnp.exp(sc_d - m)
        l = jnp.sum(p_d, axis=-1, keepdims=True)
        p_f = None
        if qi > 0:
            p_f = jnp.exp(sc_f - m)
            l = l + jnp.sum(p_f, axis=-1, keepdims=True)
            p_f = p_f.astype(BF16)
        return p_d.astype(BF16), p_f, l

    items = [(hh, qi) for hh in range(n_heads) for qi in range(t // blk)]
    sc = {}
    for step in range(len(items) + ATTN_LEAD):
        if step < len(items):
            sc[step] = scores(*items[step])
        done = step - ATTN_LEAD
        if done >= 0:
            hh, qi = items[done]
            weighted_values(hh, qi, *softmax(qi, *sc.pop(done)))


def _diff_attn(qk, v, lam_q1, lam_k1, lam_q2, lam_k2, subln_w, lambda_init, b, t):
    hw = DIFF_HEADS * 2 * DIFF_HD
    hd2 = 2 * DIFF_HD
    blk = ATTN_BLOCK if t % ATTN_BLOCK == 0 else t
    qk3 = qk.reshape(b, t, 2 * hw)
    v3 = v.reshape(b, t, hw)
    g = ATTN_HEADS_PER_STEP
    kern = functools.partial(_diff_attn_kernel, t=t, blk=blk, n_heads=g, lambda_init=lambda_init)
    head = lambda off: pl.BlockSpec((1, t, g * hd2), lambda bi, hi: (bi, 0, hi + off))
    vec = lambda a: a.astype(F32).reshape(1, -1)
    out = pl.pallas_call(
        kern,
        grid=(b, DIFF_HEADS // g),
        in_specs=[head(0), head(DIFF_HEADS // g), head(0)] + [_const_spec((1, DIFF_HD))] * 4
                 + [_const_spec((1, hd2))],
        out_specs=head(0),
        out_shape=jax.ShapeDtypeStruct((b, t, hw), BF16),
        compiler_params=_params(2),
        name="diff_attn",
    )(qk3, qk3, v3, vec(lam_q1), vec(lam_k1), vec(lam_q2), vec(lam_k2), vec(subln_w))
    return out.reshape(b * t, hw)


def kernel(x, positions, gdn_w_in, gdn_conv_w, gdn_a_log, gdn_dt_bias, gdn_norm_w, gdn_w_out,
           diff_w_in, diff_lam_q1, diff_lam_k1, diff_lam_q2, diff_lam_k2, diff_subln_w, diff_w_out,
           ffn_w_up, ffn_conv_w, ffn_conv_b, ffn_w_down, ln_mix_g, ln_mix_b, ln_ffn_g, ln_ffn_b):
    b, t, d = x.shape
    tm = _row_tile(t)
    seq_tiles = t // tm
    x2 = x.reshape(b * t, d).astype(F32)
    tables = _rope_tables(positions, tm)
    for i in range(DEPTH):
        j = i // 2
        if i % 2 == 0:
            qkv, gate, gb = _gdn_in(x2, gdn_w_in[j], gdn_conv_w[j], gdn_a_log[j], gdn_dt_bias[j],
                                    tm, seq_tiles)
            g = gb[:, :GDN_HEADS]
            beta = gb[:, GDN_HEADS:2 * GDN_HEADS]
            mixed = _gdn_chunk(qkv, gate, g, beta, gdn_norm_w[j], b, t)
            w_out = gdn_w_out[j]
        else:
            lambda_init = 0.8 - 0.6 * math.exp(-0.3 * i)
            qk, v = _diff_in(x2, diff_w_in[j], tables, tm)
            mixed = _diff_attn(qk, v, diff_lam_q1[j], diff_lam_k1[j], diff_lam_q2[j], diff_lam_k2[j],
                               diff_subln_w[j], lambda_init, b, t)
            w_out = diff_w_out[j]
        x2 = _proj_ln(mixed, w_out, x2, ln_mix_g[i], ln_mix_b[i], PROJ_TILE if t % PROJ_TILE == 0 else tm)
        x2 = _ffn(x2, ffn_w_up[i], ffn_conv_w[i], ffn_conv_b[i], ffn_w_down[i],
                  ln_ffn_g[i], ln_ffn_b[i], tm, seq_tiles)
    return x2.reshape(b, t, d)
```

```python
import functools
import math

import jax
import jax.numpy as jnp
from jax import lax
from jax.experimental import pallas as pl
from jax.experimental.pallas import tpu as pltpu

F32 = jnp.float32
BF16 = jnp.bfloat16

DEPTH = 4
GDN_HEADS = 8
GDN_DK = 128
GDN_DV = 128
GDN_CONV = 4
GDN_CHUNK = 64
DIFF_HEADS = 8
DIFF_HD = 64
ROPE_THETA = 10000.0
FFN_CONV = 3
DEEPNORM_ALPHA = (2.0 * DEPTH) ** 0.25
LN_EPS = 1e-5
RMS_EPS = 1e-6

LANES = 128
SUBLANES = 8
MXU_N = 256
VMEM_LIMIT = 56 * 1024 * 1024

ROW_TILE = 512
PROJ_TILE = 1024
ATTN_LEAD = 1
ATTN_BLOCK = 256
GDN_STEP = 256


def _row_tile(t):
    return ROW_TILE if t % ROW_TILE == 0 else t


def _sigmoid(x):
    return 1.0 / (1.0 + jnp.exp(-x))


def _silu(x):
    return x * _sigmoid(x)


def _dot(a, b):
    return jnp.dot(a, b, preferred_element_type=F32)


def _dot_nt(a, b):
    return lax.dot_general(a, b, (((1,), (1,)), ((), ())), preferred_element_type=F32)


def _shift_rows(h, s):
    return pltpu.roll(h, s, 0)


def _causal_conv(h, cw, width):
    y = h * cw[width - 1:width, :]
    for s in range(1, width):
        y = y + _shift_rows(h, s) * cw[width - 1 - s:width - s, :]
    return y[SUBLANES:, :]


def _layer_norm_rows(z, g, b):
    mu = jnp.mean(z, axis=-1, keepdims=True)
    zc = z - mu
    var = jnp.mean(zc * zc, axis=-1, keepdims=True)
    return zc * lax.rsqrt(var + LN_EPS) * g + b


def _stage_rows(x_ref, xh_ref, xb_ref, tile, seq_tiles):
    first = (tile % seq_tiles) == 0
    halo = jnp.where(first, 0.0, xh_ref[...])
    xb_ref[0:SUBLANES, :] = halo.astype(BF16)
    xb_ref[SUBLANES:, :] = x_ref[...].astype(BF16)


def _halo_spec(tm, d):
    return pl.BlockSpec((SUBLANES, d), lambda i: (jnp.maximum(i * (tm // SUBLANES) - 1, 0), 0))


def _const_spec(shape):
    return pl.BlockSpec(shape, lambda *_: (0,) * len(shape))


def _params(n_axes):
    return pltpu.CompilerParams(
        dimension_semantics=("arbitrary",) * n_axes, vmem_limit_bytes=VMEM_LIMIT)


def _rope_kernel(pos_ref, inv_ref, c_ref, s1_ref, s2_ref):
    tm = pos_ref.shape[0]
    nf = DIFF_HD // 2
    groups = LANES // nf
    blk = tm // groups
    lane = lax.broadcasted_iota(jnp.int32, (blk, LANES), 1)
    pos = pos_ref[...].astype(F32)
    packed = jnp.zeros((blk, LANES), F32)
    for j in range(groups):
        packed = jnp.where(lane // nf == j, jnp.broadcast_to(pos[j * blk:(j + 1) * blk, :], (blk, LANES)),
                           packed)
    ang = packed * inv_ref[...]
    cos = jnp.cos(ang)
    sin = jnp.sin(ang)
    lower = (lane % DIFF_HD) < nf
    for j in range(groups):
        cj, sj = cos, sin
        for g in range(groups):
            shift = ((g - j) % groups) * nf
            if shift:
                cj = jnp.where(lane // nf == g, pltpu.roll(cos, shift, 1), cj)
                sj = jnp.where(lane // nf == g, pltpu.roll(sin, shift, 1), sj)
        rows = slice(j * blk, (j + 1) * blk)
        c_ref[rows, :] = cj
        s1_ref[rows, :] = jnp.where(lower, -sj, 0.0)
        s2_ref[rows, :] = jnp.where(lower, 0.0, sj)


def _rope_tables(positions, tm):
    n = positions.size
    inv_freq = ROPE_THETA ** (-jnp.arange(0, DIFF_HD, 2, dtype=F32) / DIFF_HD)
    inv = jnp.tile(inv_freq, LANES // (DIFF_HD // 2)).reshape(1, LANES)
    pos = positions.reshape(n, 1)
    out = jax.ShapeDtypeStruct((n, LANES), F32)
    return pl.pallas_call(
        _rope_kernel,
        grid=(n // tm,),
        in_specs=[pl.BlockSpec((tm, 1), lambda i: (i, 0)), _const_spec((1, LANES))],
        out_specs=[pl.BlockSpec((tm, LANES), lambda i: (i, 0))] * 3,
        out_shape=[out] * 3,
        compiler_params=_params(1),
        name="rope_tables",
    )(pos, inv)


def _gdn_in_kernel(x_ref, xh_ref, w_ref, wab_ref, cw_ref, alog_ref, dtb_ref,
                   qkv_ref, gate_ref, gb_ref, xb_ref, *, seq_tiles, n_qk, n_v, n_gate):
    _stage_rows(x_ref, xh_ref, xb_ref, pl.program_id(0), seq_tiles)
    xb = xb_ref[...]
    xt = xb[SUBLANES:, :]
    base = (n_qk + n_v) * MXU_N
    conv_per_gate = (n_qk + n_v) // n_gate
    jobs = []
    for c in range(n_qk + n_v):
        jobs.append(("conv", c))
        if c % conv_per_gate == conv_per_gate - 1:
            jobs.append(("gate", c // conv_per_gate))

    def project(job):
        kind, c = job
        if kind == "conv":
            return _dot(xb, w_ref[:, c * MXU_N:(c + 1) * MXU_N])
        return _dot(xt, w_ref[:, base + c * MXU_N:base + (c + 1) * MXU_N])

    def finish(job, h):
        kind, c = job
        cols = slice(c * MXU_N, (c + 1) * MXU_N)
        if kind == "gate":
            gate_ref[:, cols] = h.astype(BF16)
            return
        y = _silu(_causal_conv(h, cw_ref[:, cols], GDN_CONV))
        if c < n_qk:
            parts = []
            for j in range(MXU_N // GDN_DK):
                yh = y[:, j * GDN_DK:(j + 1) * GDN_DK]
                ss = jnp.sum(yh * yh, axis=-1, keepdims=True)
                parts.append(yh * lax.rsqrt(ss + RMS_EPS))
            y = jnp.concatenate(parts, axis=1)
        qkv_ref[:, cols] = y.astype(BF16)

    for job in jobs:
        finish(job, project(job))
    hab = _dot(xt, wab_ref[...])
    z = hab + dtb_ref[...]
    softplus = jnp.maximum(z, 0.0) + jnp.log(1.0 + jnp.exp(-jnp.abs(z)))
    g = -jnp.exp(alog_ref[...]) * softplus
    lane = lax.broadcasted_iota(jnp.int32, hab.shape, 1)
    gb_ref[...] = jnp.where(lane < GDN_HEADS, g, _sigmoid(hab))


def _gdn_in(x2, w_in, conv_w, a_log, dt_bias, tm, seq_tiles):
    n, d = x2.shape
    qk_w = 2 * GDN_HEADS * GDN_DK
    v_w = GDN_HEADS * GDN_DV
    main = qk_w + 2 * v_w
    w_main = w_in[:, :main].astype(BF16)
    w_ab = jnp.pad(w_in[:, main:], ((0, 0), (0, LANES - 2 * GDN_HEADS))).astype(BF16)
    pad = (0, LANES - GDN_HEADS)
    alog = jnp.pad(a_log.astype(F32), pad).reshape(1, LANES)
    dtb = jnp.pad(dt_bias.astype(F32), pad).reshape(1, LANES)
    kern = functools.partial(_gdn_in_kernel, seq_tiles=seq_tiles, n_qk=qk_w // MXU_N,
                             n_v=v_w // MXU_N, n_gate=v_w // MXU_N)
    row = lambda w: pl.BlockSpec((tm, w), lambda i: (i, 0))
    return pl.pallas_call(
        kern,
        grid=(n // tm,),
        in_specs=[row(d), _halo_spec(tm, d), _const_spec((d, main)), _const_spec((d, LANES)),
                  _const_spec((GDN_CONV, qk_w + v_w)), _const_spec((1, LANES)),
                  _const_spec((1, LANES))],
        out_specs=[row(qk_w + v_w), row(v_w), row(LANES)],
        out_shape=[jax.ShapeDtypeStruct((n, qk_w + v_w), BF16),
                   jax.ShapeDtypeStruct((n, v_w), BF16),
                   jax.ShapeDtypeStruct((n, LANES), F32)],
        scratch_shapes=[pltpu.VMEM((SUBLANES + tm, d), BF16)],
        compiler_params=_params(1),
        name="gdn_in",
    )(x2, x2, w_main, w_ab, conv_w.astype(F32), alog, dtb)


def _unit_lower_inverses(a_list, eye, blk16, off32, off64, bdmask):
    c = GDN_CHUNK

    def bd(y):
        yy = jnp.concatenate([y, y], axis=0)
        return jnp.where(bdmask, yy, jnp.zeros_like(yy))

    n = [jnp.where(blk16, -a, 0.0) for a in a_list]
    p = [eye + x for x in n]
    nb = [x.astype(BF16) for x in n]
    n = [_dot(x, bd(x)) for x in nb]
    for _ in range(2):
        nb = [x.astype(BF16) for x in n]
        r = [_dot(jnp.concatenate([pi.astype(BF16), ni], axis=0), bd(ni)) for pi, ni in zip(p, nb)]
        p = [pi + ri[:c] for pi, ri in zip(p, r)]
        n = [ri[c:] for ri in r]
    p = [pi + _dot(pi.astype(BF16), bd(ni.astype(BF16))) for pi, ni in zip(p, n)]
    for off in (off32, off64):
        pb = [pi.astype(BF16) for pi in p]
        t = [_dot(jnp.where(off, a, 0.0).astype(BF16), bd(pbi)) for a, pbi in zip(a_list, pb)]
        p = [pi - _dot(pbi, bd(ti.astype(BF16))) for pi, pbi, ti in zip(p, pb, t)]
    return p


def _gdn_chunk_kernel(q_ref, k_ref, v_ref, gate_ref, gb_ref, nw_ref,
                      o_ref, s_ref, wq_ref, u_ref, ak_ref, *, n_sub):
    c = GDN_CHUNK

    @pl.when(pl.program_id(1) == 0)
    def _():
        s_ref[...] = jnp.zeros_like(s_ref)

    step = n_sub * c
    gb = gb_ref[0]
    in_chunk = lax.broadcasted_iota(jnp.int32, (step, LANES), 0) % c
    gc_cols = gb
    shift = 1
    while shift < c:
        gc_cols = gc_cols + jnp.where(in_chunk >= shift, pltpu.roll(gc_cols, shift, 0), 0.0)
        shift *= 2
    gc_rows = gc_cols.T[:GDN_HEADS, :]
    egc_cols = jnp.exp(gc_cols)
    bl = GDN_HEADS

    c2 = 2 * c
    r = lax.broadcasted_iota(jnp.int32, (c, c2), 0)
    lane = lax.broadcasted_iota(jnp.int32, (c, c2), 1)
    s = lane % c
    first = lane < c
    causal = s <= r
    strict = s < r
    eye = jnp.where(r == s, 1.0, 0.0)
    blk16 = (r // 16) == (s // 16)
    off32 = ((r // 32) == (s // 32)) & ((r // 16) != (s // 16))
    off64 = (r // 32) != (s // 32)
    bdmask = (lax.broadcasted_iota(jnp.int32, (c2, c2), 0) // c
              == lax.broadcasted_iota(jnp.int32, (c2, c2), 1) // c)
    kmask = (lax.broadcasted_iota(jnp.int32, (c2, 2 * GDN_DK), 0) // c
             == lax.broadcasted_iota(jnp.int32, (c2, 2 * GDN_DK), 1) // GDN_DK)
    scale = GDN_DK ** -0.5
    heads = range(GDN_HEADS)
    hcols = [slice(h * GDN_DK, (h + 1) * GDN_DK) for h in heads]
    n_hp = GDN_HEADS // 2
    pcols = [slice(hp * 2 * GDN_DK, (hp + 1) * 2 * GDN_DK) for hp in range(n_hp)]

    def pack_cols(x, rw, lane0):
        return jnp.where(first, jnp.broadcast_to(x[rw, lane0:lane0 + 1], (c, c2)),
                         jnp.broadcast_to(x[rw, lane0 + 1:lane0 + 2], (c, c2)))

    units = [(sub, hp) for sub in range(n_sub) for hp in range(n_hp)]
    rws = [slice(sub * c, (sub + 1) * c) for sub, _ in units]
    q2 = [q_ref[0, rw, pcols[hp]] for rw, (_, hp) in zip(rws, units)]
    k2 = [k_ref[0, rw, pcols[hp]] for rw, (_, hp) in zip(rws, units)]
    kbd = []
    for k in k2:
        kk = jnp.concatenate([k, k], axis=0)
        kbd.append(jnp.where(kmask, kk, jnp.zeros_like(kk)))
    qkk = [_dot_nt(jnp.concatenate([q, k], axis=0), kb) for q, k, kb in zip(q2, k2, kbd)]
    decay = []
    for rw, (_, hp) in zip(rws, units):
        gcr = jnp.concatenate([gc_rows[2 * hp:2 * hp + 1, rw], gc_rows[2 * hp + 1:2 * hp + 2, rw]], axis=1)
        diff = jnp.where(causal, pack_cols(gc_cols, rw, 2 * hp) - gcr, 0.0)
        decay.append(jnp.where(causal, jnp.exp(diff), 0.0))
    a_list = [jnp.where(strict, x[c:] * dc * pack_cols(gb, rw, bl + 2 * hp), 0.0)
              for x, dc, rw, (_, hp) in zip(qkk, decay, rws, units)]
    attn = [(x[:c] * scale * dc).astype(BF16) for x, dc in zip(qkk, decay)]
    inv = _unit_lower_inverses(a_list, eye, blk16, off32, off64, bdmask)
    zeros = jnp.zeros((c, GDN_DV + GDN_DK), BF16)
    rhs, kfs = [], []
    for rw, (_, hp) in zip(rws, units):
        both = []
        for h in (2 * hp, 2 * hp + 1):
            bcol = gb[rw, bl + h:bl + h + 1]
            kf = k_ref[0, rw, hcols[h]].astype(F32)
            kfs.append(kf)
            both.append(jnp.concatenate([v_ref[0, rw, hcols[h]].astype(F32) * bcol,
                                         kf * (bcol * egc_cols[rw, h:h + 1])], axis=1).astype(BF16))
        rhs.append(jnp.concatenate([jnp.concatenate([both[0], zeros], axis=1),
                                    jnp.concatenate([zeros, both[1]], axis=1)], axis=0))
    sol = [_dot(t.astype(BF16), x) for t, x in zip(inv, rhs)]
    for i, (rw, (sub, hp)) in enumerate(zip(rws, units)):
        kds = []
        for j, h in enumerate((2 * hp, 2 * hp + 1)):
            base = j * (GDN_DV + GDN_DK)
            g_last = gc_cols[(sub + 1) * c - 1:(sub + 1) * c, h:h + 1]
            u_ref[sub, h] = sol[i][:, base:base + GDN_DV]
            qd = (q_ref[0, rw, hcols[h]].astype(F32) * (egc_cols[rw, h:h + 1] * scale)).astype(BF16)
            wq_ref[sub, h] = jnp.concatenate(
                [sol[i][:, base + GDN_DV:base + GDN_DV + GDN_DK].astype(BF16), qd], axis=0)
            kds.append(kfs[2 * i + j] * jnp.exp(g_last - gc_cols[rw, h:h + 1]))
        kdt = jnp.concatenate(kds, axis=0).T
        ak_ref[sub, hp] = jnp.concatenate([attn[i], kdt.astype(BF16)], axis=0)

    nw = nw_ref[...]
    zv = jnp.zeros((c, GDN_DV), BF16)
    for sub in range(n_sub):
        rows = slice(sub * c, (sub + 1) * c)
        g_last = gc_cols[(sub + 1) * c - 1:(sub + 1) * c, :]
        st = [s_ref[h] for h in heads]
        r1 = [_dot(wq_ref[sub, h], st[h].astype(BF16)) for h in heads]
        vb = [(u_ref[sub, h] - r1[h][:c]).astype(BF16) for h in heads]
        r2p = []
        for hp in range(n_hp):
            w2 = jnp.concatenate([jnp.concatenate([vb[2 * hp], zv], axis=1),
                                  jnp.concatenate([zv, vb[2 * hp + 1]], axis=1)], axis=0)
            r2p.append(_dot(ak_ref[sub, hp], w2))
        r2 = [r2p[h // 2][:, (h % 2) * GDN_DV:(h % 2 + 1) * GDN_DV] for h in heads]
        for h in heads:
            s_ref[h] = st[h] * jnp.exp(g_last[:, h:h + 1]) + r2[h][c:]
            o = r1[h][c:] + r2[h][:c]
            ms = jnp.mean(o * o, axis=-1, keepdims=True)
            on = o * lax.rsqrt(ms + RMS_EPS) * nw
            gt = gate_ref[0, rows, hcols[h]].astype(F32)
            o_ref[0, rows, hcols[h]] = (on * _silu(gt)).astype(BF16)


def _gdn_chunk(qkv, gate, gb, norm_w, b, t):
    hq = GDN_HEADS * GDN_DK
    hv = GDN_HEADS * GDN_DV
    step = GDN_STEP if t % GDN_STEP == 0 else t
    qkv3 = qkv.reshape(b, t, 2 * hq + hv)
    gate3 = gate.reshape(b, t, hv)
    gb3 = gb.reshape(b, t, LANES)
    n_sub = step // GDN_CHUNK
    kern = functools.partial(_gdn_chunk_kernel, n_sub=n_sub)
    col = lambda j: pl.BlockSpec((1, step, hq), lambda bi, ci: (bi, ci, j))
    out = pl.pallas_call(
        kern,
        grid=(b, t // step),
        in_specs=[col(0), col(1), col(2), col(0),
                  pl.BlockSpec((1, step, LANES), lambda bi, ci: (bi, ci, 0)),
                  _const_spec((1, GDN_DV))],
        out_specs=col(0),
        out_shape=jax.ShapeDtypeStruct((b, t, hv), BF16),
        scratch_shapes=[pltpu.VMEM((GDN_HEADS, GDN_DK, GDN_DV), F32),
                        pltpu.VMEM((n_sub, GDN_HEADS, 2 * GDN_CHUNK, GDN_DK), BF16),
                        pltpu.VMEM((n_sub, GDN_HEADS, GDN_CHUNK, GDN_DV), F32),
                        pltpu.VMEM((n_sub, GDN_HEADS // 2, GDN_CHUNK + GDN_DK, 2 * GDN_CHUNK), BF16)],
        compiler_params=_params(2),
        name="gdn_chunk",
    )(qkv3, qkv3, qkv3, gate3, gb3, norm_w.astype(F32).reshape(1, GDN_DV))
    return out.reshape(b * t, hv)


def _proj_ln_kernel(y_ref, w_ref, x_ref, g_ref, b_ref, o_ref):
    m = _dot(y_ref[...], w_ref[...])
    z = DEEPNORM_ALPHA * x_ref[...] + m
    o_ref[...] = _layer_norm_rows(z, g_ref[...], b_ref[...])


def _proj_ln(y, w, x2, g, b, tm):
    n, d = x2.shape
    k = y.shape[1]
    row = lambda w_: pl.BlockSpec((tm, w_), lambda i: (i, 0))
    return pl.pallas_call(
        _proj_ln_kernel,
        grid=(n // tm,),
        in_specs=[row(k), _const_spec((k, d)), row(d), _const_spec((1, d)), _const_spec((1, d))],
        out_specs=row(d),
        out_shape=jax.ShapeDtypeStruct((n, d), F32),
        compiler_params=_params(1),
        name="proj_ln",
    )(y, w.astype(BF16), x2, g.astype(F32).reshape(1, d), b.astype(F32).reshape(1, d))


def _ffn_kernel(x_ref, xh_ref, wup_ref, cw_ref, cb_ref, wdn_ref, g_ref, b_ref, o_ref,
                xb_ref, act_ref, *, seq_tiles, hidden):
    _stage_rows(x_ref, xh_ref, xb_ref, pl.program_id(0), seq_tiles)
    xb = xb_ref[...]
    n_chunks = hidden // MXU_N

    def project(c):
        return [_dot(xb, wup_ref[:, base + c * MXU_N:base + (c + 1) * MXU_N]) for base in (0, hidden)]

    def finish(c, hs):
        halves = []
        for base, h in zip((0, hidden), hs):
            cols = slice(base + c * MXU_N, base + (c + 1) * MXU_N)
            halves.append(_causal_conv(h, cw_ref[:, cols], FFN_CONV) + cb_ref[:, cols])
        act_ref[:, c * MXU_N:(c + 1) * MXU_N] = (_silu(halves[0]) * halves[1]).astype(BF16)

    for c in range(n_chunks):
        finish(c, project(c))
    f = _dot(act_ref[...], wdn_ref[...])
    z = DEEPNORM_ALPHA * x_ref[...] + f
    o_ref[...] = _layer_norm_rows(z, g_ref[...], b_ref[...])


def _ffn(x2, w_up, conv_w, conv_b, w_down, g, b, tm, seq_tiles):
    n, d = x2.shape
    hidden = w_down.shape[0]
    kern = functools.partial(_ffn_kernel, seq_tiles=seq_tiles, hidden=hidden)
    row = pl.BlockSpec((tm, d), lambda i: (i, 0))
    single = lambda shape: pl.BlockSpec(shape, lambda *_: (0,) * len(shape),
                                        pipeline_mode=pl.Buffered(1))
    return pl.pallas_call(
        kern,
        grid=(n // tm,),
        in_specs=[row, _halo_spec(tm, d), single((d, 2 * hidden)),
                  _const_spec((FFN_CONV, 2 * hidden)), _const_spec((1, 2 * hidden)),
                  single((hidden, d)), _const_spec((1, d)), _const_spec((1, d))],
        out_specs=row,
        out_shape=jax.ShapeDtypeStruct((n, d), F32),
        scratch_shapes=[pltpu.VMEM((SUBLANES + tm, d), BF16), pltpu.VMEM((tm, hidden), BF16)],
        compiler_params=_params(1),
        name="conv_ffn",
    )(x2, x2, w_up.astype(BF16), conv_w.astype(F32), conv_b.astype(F32).reshape(1, 2 * hidden),
      w_down.astype(BF16), g.astype(F32).reshape(1, d), b.astype(F32).reshape(1, d))


def _diff_in_kernel(x_ref, w_ref, c_ref, s1_ref, s2_ref, qk_ref, v_ref, *, n_qk, n_v, q_cols):
    xb = x_ref[...].astype(BF16)
    cos = c_ref[...]
    s1 = s1_ref[...]
    s2 = s2_ref[...]
    half = DIFF_HD // 2
    for c in range(n_qk):
        h = _dot(xb, w_ref[:, c * MXU_N:(c + 1) * MXU_N])
        for j in range(MXU_N // LANES):
            hs = h[:, j * LANES:(j + 1) * LANES]
            rot = hs * cos + pltpu.roll(hs, LANES - half, 1) * s1 + pltpu.roll(hs, half, 1) * s2
            lo = c * MXU_N + j * LANES
            if lo < q_cols:
                rot = rot * (DIFF_HD ** -0.5)
            qk_ref[:, lo:lo + LANES] = rot.astype(BF16)
    base = n_qk * MXU_N
    for c in range(n_v):
        v_ref[:, c * MXU_N:(c + 1) * MXU_N] = _dot(
            xb, w_ref[:, base + c * MXU_N:base + (c + 1) * MXU_N]).astype(BF16)


def _diff_in(x2, w_in, tables, tm):
    n, d = x2.shape
    hw = DIFF_HEADS * 2 * DIFF_HD
    kern = functools.partial(_diff_in_kernel, n_qk=2 * hw // MXU_N, n_v=hw // MXU_N, q_cols=hw)
    row = lambda w: pl.BlockSpec((tm, w), lambda i: (i, 0))
    return pl.pallas_call(
        kern,
        grid=(n // tm,),
        in_specs=[row(d), _const_spec((d, 3 * hw)), row(LANES), row(LANES), row(LANES)],
        out_specs=[row(2 * hw), row(hw)],
        out_shape=[jax.ShapeDtypeStruct((n, 2 * hw), BF16), jax.ShapeDtypeStruct((n, hw), BF16)],
        compiler_params=_params(1),
        name="diff_in",
    )(x2, w_in.astype(BF16), *tables)


def _diff_attn_kernel(q_ref, k_ref, v_ref, lq1_ref, lk1_ref, lq2_ref, lk2_ref, sw_ref, o_ref,
                      *, t, blk, lambda_init):
    lam = (jnp.exp(jnp.sum(lq1_ref[...] * lk1_ref[...], axis=-1, keepdims=True))
           - jnp.exp(jnp.sum(lq2_ref[...] * lk2_ref[...], axis=-1, keepdims=True))
           + lambda_init)
    lane = lax.broadcasted_iota(jnp.int32, (blk, 2 * DIFF_HD), 1)
    first_map = lane < DIFF_HD
    r = lax.broadcasted_iota(jnp.int32, (2 * blk, blk), 0)
    s = lax.broadcasted_iota(jnp.int32, (2 * blk, blk), 1)
    diag_ok = s <= (r % blk)
    sw = sw_ref[...]

    def scores(qi):
        q = q_ref[0, qi * blk:(qi + 1) * blk, :]
        zero = jnp.zeros_like(q)
        qs = jnp.concatenate([jnp.where(first_map, q, zero), jnp.where(first_map, zero, q)], axis=0)
        d0 = qi * blk
        sc_d = jnp.where(diag_ok, _dot_nt(qs, k_ref[0, d0:d0 + blk, :]), -jnp.inf)
        sc_f = _dot_nt(qs, k_ref[0, 0:d0, :]) if qi > 0 else None
        return sc_d, sc_f

    def softmax(qi, sc_d, sc_f):
        m = jnp.max(sc_d, axis=-1, keepdims=True)
        if qi > 0:
            m = jnp.maximum(m, jnp.max(sc_f, axis=-1, keepdims=True))
        p_d = jnp.exp(sc_d - m)
        l = jnp.sum(p_d, axis=-1, keepdims=True)
        p_f = None
        if qi > 0:
            p_f = jnp.exp(sc_f - m)
            l = l + jnp.sum(p_f, axis=-1, keepdims=True)
            p_f = p_f.astype(BF16)
        return p_d.astype(BF16), p_f, l

    def weighted_values(qi, p_d, p_f, l):
        d0 = qi * blk
        acc = _dot(p_d, v_ref[0, d0:d0 + blk, :])
        if qi > 0:
            acc = acc + _dot(p_f, v_ref[0, 0:d0, :])
        on = acc / l
        o = on[:blk] - lam * on[blk:]
        ms = jnp.mean(o * o, axis=-1, keepdims=True)
        o = o * lax.rsqrt(ms + RMS_EPS) * sw * (1.0 - lambda_init)
        o_ref[0, qi * blk:(qi + 1) * blk, :] = o.astype(BF16)

    n_q = t // blk
    sc = {}
    for step in range(n_q + ATTN_LEAD):
        if step < n_q:
            sc[step] = scores(step)
        qi = step - ATTN_LEAD
        if qi >= 0:
            weighted_values(qi, *softmax(qi, *sc.pop(qi)))


def _diff_attn(qk, v, lam_q1, lam_k1, lam_q2, lam_k2, subln_w, lambda_init, b, t):
    hw = DIFF_HEADS * 2 * DIFF_HD
    hd2 = 2 * DIFF_HD
    blk = ATTN_BLOCK if t % ATTN_BLOCK == 0 else t
    qk3 = qk.reshape(b, t, 2 * hw)
    v3 = v.reshape(b, t, hw)
    kern = functools.partial(_diff_attn_kernel, t=t, blk=blk, lambda_init=lambda_init)
    head = lambda off: pl.BlockSpec((1, t, hd2), lambda bi, hi: (bi, 0, hi + off))
    vec = lambda a: a.astype(F32).reshape(1, -1)
    out = pl.pallas_call(
        kern,
        grid=(b, DIFF_HEADS),
        in_specs=[head(0), head(DIFF_HEADS), head(0)] + [_const_spec((1, DIFF_HD))] * 4
                 + [_const_spec((1, hd2))],
        out_specs=head(0),
        out_shape=jax.ShapeDtypeStruct((b, t, hw), BF16),
        compiler_params=_params(2),
        name="diff_attn",
    )(qk3, qk3, v3, vec(lam_q1), vec(lam_k1), vec(lam_q2), vec(lam_k2), vec(subln_w))
    return out.reshape(b * t, hw)


def kernel(x, positions, gdn_w_in, gdn_conv_w, gdn_a_log, gdn_dt_bias, gdn_norm_w, gdn_w_out,
           diff_w_in, diff_lam_q1, diff_lam_k1, diff_lam_q2, diff_lam_k2, diff_subln_w, diff_w_out,
           ffn_w_up, ffn_conv_w, ffn_conv_b, ffn_w_down, ln_mix_g, ln_mix_b, ln_ffn_g, ln_ffn_b):
    b, t, d = x.shape
    tm = _row_tile(t)
    seq_tiles = t // tm
    x2 = x.reshape(b * t, d).astype(F32)
    tables = _rope_tables(positions, tm)
    for i in range(DEPTH):
        j = i // 2
        if i % 2 == 0:
            qkv, gate, gb = _gdn_in(x2, gdn_w_in[j], gdn_conv_w[j], gdn_a_log[j], gdn_dt_bias[j],
                                    tm, seq_tiles)
            mixed = _gdn_chunk(qkv, gate, gb, gdn_norm_w[j], b, t)
            w_out = gdn_w_out[j]
        else:
            lambda_init = 0.8 - 0.6 * math.exp(-0.3 * i)
            qk, v = _diff_in(x2, diff_w_in[j], tables, tm)
            mixed = _diff_attn(qk, v, diff_lam_q1[j], diff_lam_k1[j], diff_lam_q2[j], diff_lam_k2[j],
                               diff_subln_w[j], lambda_init, b, t)
            w_out = diff_w_out[j]
        proj_tile = PROJ_TILE if t % PROJ_TILE == 0 else tm
        x2 = _proj_ln(mixed, w_out, x2, ln_mix_g[i], ln_mix_b[i], proj_tile)
        x2 = _ffn(x2, ffn_w_up[i], ffn_conv_w[i], ffn_conv_b[i], ffn_w_down[i],
                  ln_ffn_g[i], ln_ffn_b[i], tm, seq_tiles)
    return x2.reshape(b, t, d)
```

```python
import functools
import math

import jax
import jax.numpy as jnp
from jax import lax
from jax.experimental import pallas as pl
from jax.experimental.pallas import tpu as pltpu

F32 = jnp.float32
BF16 = jnp.bfloat16

DEPTH = 4
GDN_HEADS = 8
GDN_DK = 128
GDN_DV = 128
GDN_CONV = 4
GDN_CHUNK = 64
DIFF_HEADS = 8
DIFF_HD = 64
ROPE_THETA = 10000.0
FFN_CONV = 3
DEEPNORM_ALPHA = (2.0 * DEPTH) ** 0.25
LN_EPS = 1e-5
RMS_EPS = 1e-6

LANES = 128
SUBLANES = 8
MXU_N = 256
VMEM_LIMIT = 56 * 1024 * 1024

ROW_TILE = 512
PROJ_TILE = 1024
ATTN_LEAD = 1
ATTN_BLOCK = 256
GDN_STEP = 512


def _row_tile(t):
    return ROW_TILE if t % ROW_TILE == 0 else t


def _sigmoid(x):
    return 1.0 / (1.0 + jnp.exp(-x))


def _silu(x):
    return x * _sigmoid(x)


def _dot(a, b):
    return jnp.dot(a, b, preferred_element_type=F32)


def _dot_nt(a, b):
    return lax.dot_general(a, b, (((1,), (1,)), ((), ())), preferred_element_type=F32)


def _shift_rows(h, s):
    return pltpu.roll(h, s, 0)


def _causal_conv(h, cw, width):
    y = h * cw[width - 1:width, :]
    for s in range(1, width):
        y = y + _shift_rows(h, s) * cw[width - 1 - s:width - s, :]
    return y[SUBLANES:, :]


def _layer_norm_rows(z, g, b):
    mu = jnp.mean(z, axis=-1, keepdims=True)
    zc = z - mu
    var = jnp.mean(zc * zc, axis=-1, keepdims=True)
    return zc * lax.rsqrt(var + LN_EPS) * g + b


def _stage_rows(x_ref, xh_ref, xb_ref, tile, seq_tiles):
    first = (tile % seq_tiles) == 0
    halo = jnp.where(first, 0.0, xh_ref[...])
    xb_ref[0:SUBLANES, :] = halo.astype(BF16)
    xb_ref[SUBLANES:, :] = x_ref[...].astype(BF16)


def _halo_spec(tm, d):
    return pl.BlockSpec((SUBLANES, d), lambda i: (jnp.maximum(i * (tm // SUBLANES) - 1, 0), 0))


def _const_spec(shape):
    return pl.BlockSpec(shape, lambda *_: (0,) * len(shape))


def _params(n_axes):
    return pltpu.CompilerParams(
        dimension_semantics=("arbitrary",) * n_axes, vmem_limit_bytes=VMEM_LIMIT)


def _rope_kernel(pos_ref, inv_ref, c_ref, s1_ref, s2_ref):
    tm = pos_ref.shape[0]
    nf = DIFF_HD // 2
    groups = LANES // nf
    blk = tm // groups
    lane = lax.broadcasted_iota(jnp.int32, (blk, LANES), 1)
    pos = pos_ref[...].astype(F32)
    packed = jnp.zeros((blk, LANES), F32)
    for j in range(groups):
        packed = jnp.where(lane // nf == j, jnp.broadcast_to(pos[j * blk:(j + 1) * blk, :], (blk, LANES)),
                           packed)
    ang = packed * inv_ref[...]
    cos = jnp.cos(ang)
    sin = jnp.sin(ang)
    lower = (lane % DIFF_HD) < nf
    for j in range(groups):
        cj, sj = cos, sin
        for g in range(groups):
            shift = ((g - j) % groups) * nf
            if shift:
                cj = jnp.where(lane // nf == g, pltpu.roll(cos, shift, 1), cj)
                sj = jnp.where(lane // nf == g, pltpu.roll(sin, shift, 1), sj)
        rows = slice(j * blk, (j + 1) * blk)
        c_ref[rows, :] = cj
        s1_ref[rows, :] = jnp.where(lower, -sj, 0.0)
        s2_ref[rows, :] = jnp.where(lower, 0.0, sj)


def _rope_tables(positions, tm):
    n = positions.size
    inv_freq = ROPE_THETA ** (-jnp.arange(0, DIFF_HD, 2, dtype=F32) / DIFF_HD)
    inv = jnp.tile(inv_freq, LANES // (DIFF_HD // 2)).reshape(1, LANES)
    pos = positions.reshape(n, 1)
    out = jax.ShapeDtypeStruct((n, LANES), F32)
    return pl.pallas_call(
        _rope_kernel,
        grid=(n // tm,),
        in_specs=[pl.BlockSpec((tm, 1), lambda i: (i, 0)), _const_spec((1, LANES))],
        out_specs=[pl.BlockSpec((tm, LANES), lambda i: (i, 0))] * 3,
        out_shape=[out] * 3,
        compiler_params=_params(1),
        name="rope_tables",
    )(pos, inv)


def _gdn_in_kernel(x_ref, xh_ref, w_ref, wab_ref, cw_ref, alog_ref, dtb_ref,
                   qkv_ref, gate_ref, gb_ref, xb_ref, *, seq_tiles, n_qk, n_v, n_gate):
    _stage_rows(x_ref, xh_ref, xb_ref, pl.program_id(0), seq_tiles)
    xb = xb_ref[...]
    xt = xb[SUBLANES:, :]
    base = (n_qk + n_v) * MXU_N
    conv_per_gate = (n_qk + n_v) // n_gate
    jobs = []
    for c in range(n_qk + n_v):
        jobs.append(("conv", c))
        if c % conv_per_gate == conv_per_gate - 1:
            jobs.append(("gate", c // conv_per_gate))

    def project(job):
        kind, c = job
        if kind == "conv":
            return _dot(xb, w_ref[:, c * MXU_N:(c + 1) * MXU_N])
        return _dot(xt, w_ref[:, base + c * MXU_N:base + (c + 1) * MXU_N])

    def finish(job, h):
        kind, c = job
        cols = slice(c * MXU_N, (c + 1) * MXU_N)
        if kind == "gate":
            gate_ref[:, cols] = h.astype(BF16)
            return
        y = _silu(_causal_conv(h, cw_ref[:, cols], GDN_CONV))
        if c < n_qk:
            parts = []
            for j in range(MXU_N // GDN_DK):
                yh = y[:, j * GDN_DK:(j + 1) * GDN_DK]
                ss = jnp.sum(yh * yh, axis=-1, keepdims=True)
                parts.append(yh * lax.rsqrt(ss + RMS_EPS))
            y = jnp.concatenate(parts, axis=1)
        qkv_ref[:, cols] = y.astype(BF16)

    for job in jobs:
        finish(job, project(job))
    hab = _dot(xt, wab_ref[...])
    z = hab + dtb_ref[...]
    softplus = jnp.maximum(z, 0.0) + jnp.log(1.0 + jnp.exp(-jnp.abs(z)))
    g = -jnp.exp(alog_ref[...]) * softplus
    lane = lax.broadcasted_iota(jnp.int32, hab.shape, 1)
    gb_ref[...] = jnp.where(lane < GDN_HEADS, g, _sigmoid(hab))


def _gdn_in(x2, w_in, conv_w, a_log, dt_bias, tm, seq_tiles):
    n, d = x2.shape
    qk_w = 2 * GDN_HEADS * GDN_DK
    v_w = GDN_HEADS * GDN_DV
    main = qk_w + 2 * v_w
    w_main = w_in[:, :main].astype(BF16)
    w_ab = jnp.pad(w_in[:, main:], ((0, 0), (0, LANES - 2 * GDN_HEADS))).astype(BF16)
    pad = (0, LANES - GDN_HEADS)
    alog = jnp.pad(a_log.astype(F32), pad).reshape(1, LANES)
    dtb = jnp.pad(dt_bias.astype(F32), pad).reshape(1, LANES)
    kern = functools.partial(_gdn_in_kernel, seq_tiles=seq_tiles, n_qk=qk_w // MXU_N,
                             n_v=v_w // MXU_N, n_gate=v_w // MXU_N)
    row = lambda w: pl.BlockSpec((tm, w), lambda i: (i, 0))
    return pl.pallas_call(
        kern,
        grid=(n // tm,),
        in_specs=[row(d), _halo_spec(tm, d),
                  pl.BlockSpec((d, main), lambda i: (0, 0), pipeline_mode=pl.Buffered(1)),
                  _const_spec((d, LANES)),
                  _const_spec((GDN_CONV, qk_w + v_w)), _const_spec((1, LANES)),
                  _const_spec((1, LANES))],
        out_specs=[row(qk_w + v_w), row(v_w), row(LANES)],
        out_shape=[jax.ShapeDtypeStruct((n, qk_w + v_w), BF16),
                   jax.ShapeDtypeStruct((n, v_w), BF16),
                   jax.ShapeDtypeStruct((n, LANES), F32)],
        scratch_shapes=[pltpu.VMEM((SUBLANES + tm, d), BF16)],
        compiler_params=_params(1),
        name="gdn_in",
    )(x2, x2, w_main, w_ab, conv_w.astype(F32), alog, dtb)


def _unit_lower_inverses(a_list, eye, blk16, off32, off64, bdmask):
    c = GDN_CHUNK

    def bd(y):
        yy = jnp.concatenate([y, y], axis=0)
        return jnp.where(bdmask, yy, jnp.zeros_like(yy))

    n = [jnp.where(blk16, -a, 0.0) for a in a_list]
    p = [eye + x for x in n]
    nb = [x.astype(BF16) for x in n]
    n = [_dot(x, bd(x)) for x in nb]
    for _ in range(2):
        nb = [x.astype(BF16) for x in n]
        r = [_dot(jnp.concatenate([pi.astype(BF16), ni], axis=0), bd(ni)) for pi, ni in zip(p, nb)]
        p = [pi + ri[:c] for pi, ri in zip(p, r)]
        n = [ri[c:] for ri in r]
    p = [pi + _dot(pi.astype(BF16), bd(ni.astype(BF16))) for pi, ni in zip(p, n)]
    for off in (off32, off64):
        pb = [pi.astype(BF16) for pi in p]
        t = [_dot(jnp.where(off, a, 0.0).astype(BF16), bd(pbi)) for a, pbi in zip(a_list, pb)]
        p = [pi - _dot(pbi, bd(ti.astype(BF16))) for pi, pbi, ti in zip(p, pb, t)]
    return p


def _gdn_chunk_kernel(q_ref, k_ref, v_ref, gate_ref, gb_ref, nw_ref,
                      o_ref, s_ref, wq_ref, u_ref, ak_ref, *, n_sub):
    c = GDN_CHUNK

    @pl.when(pl.program_id(1) == 0)
    def _():
        s_ref[...] = jnp.zeros_like(s_ref)

    step = n_sub * c
    gb = gb_ref[0]
    in_chunk = lax.broadcasted_iota(jnp.int32, (step, LANES), 0) % c
    gc_cols = gb
    shift = 1
    while shift < c:
        gc_cols = gc_cols + jnp.where(in_chunk >= shift, pltpu.roll(gc_cols, shift, 0), 0.0)
        shift *= 2
    gc_rows = gc_cols.T[:GDN_HEADS, :]
    egc_cols = jnp.exp(gc_cols)
    bl = GDN_HEADS

    c2 = 2 * c
    r = lax.broadcasted_iota(jnp.int32, (c, c2), 0)
    lane = lax.broadcasted_iota(jnp.int32, (c, c2), 1)
    s = lane % c
    first = lane < c
    causal = s <= r
    strict = s < r
    eye = jnp.where(r == s, 1.0, 0.0)
    blk16 = (r // 16) == (s // 16)
    off32 = ((r // 32) == (s // 32)) & ((r // 16) != (s // 16))
    off64 = (r // 32) != (s // 32)
    bdmask = (lax.broadcasted_iota(jnp.int32, (c2, c2), 0) // c
              == lax.broadcasted_iota(jnp.int32, (c2, c2), 1) // c)
    kmask = (lax.broadcasted_iota(jnp.int32, (c2, 2 * GDN_DK), 0) // c
             == lax.broadcasted_iota(jnp.int32, (c2, 2 * GDN_DK), 1) // GDN_DK)
    scale = GDN_DK ** -0.5
    heads = range(GDN_HEADS)
    hcols = [slice(h * GDN_DK, (h + 1) * GDN_DK) for h in heads]
    n_hp = GDN_HEADS // 2
    pcols = [slice(hp * 2 * GDN_DK, (hp + 1) * 2 * GDN_DK) for hp in range(n_hp)]

    def pack_cols(x, rw, lane0):
        return jnp.where(first, jnp.broadcast_to(x[rw, lane0:lane0 + 1], (c, c2)),
                         jnp.broadcast_to(x[rw, lane0 + 1:lane0 + 2], (c, c2)))

    units = [(sub, hp) for sub in range(n_sub) for hp in range(n_hp)]
    rws = [slice(sub * c, (sub + 1) * c) for sub, _ in units]
    q2 = [q_ref[0, rw, pcols[hp]] for rw, (_, hp) in zip(rws, units)]
    k2 = [k_ref[0, rw, pcols[hp]] for rw, (_, hp) in zip(rws, units)]
    kbd = []
    for k in k2:
        kk = jnp.concatenate([k, k], axis=0)
        kbd.append(jnp.where(kmask, kk, jnp.zeros_like(kk)))
    qkk = [_dot_nt(jnp.concatenate([q, k], axis=0), kb) for q, k, kb in zip(q2, k2, kbd)]
    decay = []
    for rw, (_, hp) in zip(rws, units):
        gcr = jnp.concatenate([gc_rows[2 * hp:2 * hp + 1, rw], gc_rows[2 * hp + 1:2 * hp + 2, rw]], axis=1)
        diff = jnp.where(causal, pack_cols(gc_cols, rw, 2 * hp) - gcr, 0.0)
        decay.append(jnp.where(causal, jnp.exp(diff), 0.0))
    a_list = [jnp.where(strict, x[c:] * dc * pack_cols(gb, rw, bl + 2 * hp), 0.0)
              for x, dc, rw, (_, hp) in zip(qkk, decay, rws, units)]
    attn = [(x[:c] * scale * dc).astype(BF16) for x, dc in zip(qkk, decay)]
    inv = _unit_lower_inverses(a_list, eye, blk16, off32, off64, bdmask)
    zeros = jnp.zeros((c, GDN_DV + GDN_DK), BF16)
    rhs, kfs = [], []
    for rw, (_, hp) in zip(rws, units):
        both = []
        for h in (2 * hp, 2 * hp + 1):
            bcol = gb[rw, bl + h:bl + h + 1]
            kf = k_ref[0, rw, hcols[h]].astype(F32)
            kfs.append(kf)
            both.append(jnp.concatenate([v_ref[0, rw, hcols[h]].astype(F32) * bcol,
                                         kf * (bcol * egc_cols[rw, h:h + 1])], axis=1).astype(BF16))
        rhs.append(jnp.concatenate([jnp.concatenate([both[0], zeros], axis=1),
                                    jnp.concatenate([zeros, both[1]], axis=1)], axis=0))
    sol = [_dot(t.astype(BF16), x) for t, x in zip(inv, rhs)]
    for i, (rw, (sub, hp)) in enumerate(zip(rws, units)):
        kds = []
        for j, h in enumerate((2 * hp, 2 * hp + 1)):
            base = j * (GDN_DV + GDN_DK)
            g_last = gc_cols[(sub + 1) * c - 1:(sub + 1) * c, h:h + 1]
            u_ref[sub, h] = sol[i][:, base:base + GDN_DV]
            qd = (q_ref[0, rw, hcols[h]].astype(F32) * (egc_cols[rw, h:h + 1] * scale)).astype(BF16)
            wq_ref[sub, h] = jnp.concatenate(
                [sol[i][:, base + GDN_DV:base + GDN_DV + GDN_DK].astype(BF16), qd], axis=0)
            kds.append(kfs[2 * i + j] * jnp.exp(g_last - gc_cols[rw, h:h + 1]))
        kdt = jnp.concatenate(kds, axis=0).T
        ak_ref[sub, hp] = jnp.concatenate([attn[i], kdt.astype(BF16)], axis=0)

    nw = nw_ref[...]
    zv = jnp.zeros((c, GDN_DV), BF16)
    for sub in range(n_sub):
        rows = slice(sub * c, (sub + 1) * c)
        g_last = gc_cols[(sub + 1) * c - 1:(sub + 1) * c, :]
        st = [s_ref[h] for h in heads]
        r1 = [_dot(wq_ref[sub, h], st[h].astype(BF16)) for h in heads]
        vb = [(u_ref[sub, h] - r1[h][:c]).astype(BF16) for h in heads]
        r2p = []
        for hp in range(n_hp):
            w2 = jnp.concatenate([jnp.concatenate([vb[2 * hp], zv], axis=1),
                                  jnp.concatenate([zv, vb[2 * hp + 1]], axis=1)], axis=0)
            r2p.append(_dot(ak_ref[sub, hp], w2))
        r2 = [r2p[h // 2][:, (h % 2) * GDN_DV:(h % 2 + 1) * GDN_DV] for h in heads]
        for h in heads:
            s_ref[h] = st[h] * jnp.exp(g_last[:, h:h + 1]) + r2[h][c:]
            o = r1[h][c:] + r2[h][:c]
            ms = jnp.mean(o * o, axis=-1, keepdims=True)
            on = o * lax.rsqrt(ms + RMS_EPS) * nw
            gt = gate_ref[0, rows, hcols[h]].astype(F32)
            o_ref[0, rows, hcols[h]] = (on * _silu(gt)).astype(BF16)


def _gdn_chunk(qkv, gate, gb, norm_w, b, t):
    hq = GDN_HEADS * GDN_DK
    hv = GDN_HEADS * GDN_DV
    step = GDN_STEP if t % GDN_STEP == 0 else t
    qkv3 = qkv.reshape(b, t, 2 * hq + hv)
    gate3 = gate.reshape(b, t, hv)
    gb3 = gb.reshape(b, t, LANES)
    n_sub = step // GDN_CHUNK
    kern = functools.partial(_gdn_chunk_kernel, n_sub=n_sub)
    col = lambda j: pl.BlockSpec((1, step, hq), lambda bi, ci: (bi, ci, j))
    out = pl.pallas_call(
        kern,
        grid=(b, t // step),
        in_specs=[col(0), col(1), col(2), col(0),
                  pl.BlockSpec((1, step, LANES), lambda bi, ci: (bi, ci, 0)),
                  _const_spec((1, GDN_DV))],
        out_specs=col(0),
        out_shape=jax.ShapeDtypeStruct((b, t, hv), BF16),
        scratch_shapes=[pltpu.VMEM((GDN_HEADS, GDN_DK, GDN_DV), F32),
                        pltpu.VMEM((n_sub, GDN_HEADS, 2 * GDN_CHUNK, GDN_DK), BF16),
                        pltpu.VMEM((n_sub, GDN_HEADS, GDN_CHUNK, GDN_DV), F32),
                        pltpu.VMEM((n_sub, GDN_HEADS // 2, GDN_CHUNK + GDN_DK, 2 * GDN_CHUNK), BF16)],
        compiler_params=_params(2),
        name="gdn_chunk",
    )(qkv3, qkv3, qkv3, gate3, gb3, norm_w.astype(F32).reshape(1, GDN_DV))
    return out.reshape(b * t, hv)


def _proj_ln_kernel(y_ref, w_ref, x_ref, g_ref, b_ref, o_ref):
    m = _dot(y_ref[...], w_ref[...])
    z = DEEPNORM_ALPHA * x_ref[...] + m
    o_ref[...] = _layer_norm_rows(z, g_ref[...], b_ref[...])


def _proj_ln(y, w, x2, g, b, tm):
    n, d = x2.shape
    k = y.shape[1]
    row = lambda w_: pl.BlockSpec((tm, w_), lambda i: (i, 0))
    return pl.pallas_call(
        _proj_ln_kernel,
        grid=(n // tm,),
        in_specs=[row(k), _const_spec((k, d)), row(d), _const_spec((1, d)), _const_spec((1, d))],
        out_specs=row(d),
        out_shape=jax.ShapeDtypeStruct((n, d), F32),
        compiler_params=_params(1),
        name="proj_ln",
    )(y, w.astype(BF16), x2, g.astype(F32).reshape(1, d), b.astype(F32).reshape(1, d))


def _ffn_kernel(x_ref, xh_ref, wup_ref, cw_ref, cb_ref, wdn_ref, g_ref, b_ref, o_ref,
                xb_ref, act_ref, *, seq_tiles, hidden):
    _stage_rows(x_ref, xh_ref, xb_ref, pl.program_id(0), seq_tiles)
    xb = xb_ref[...]
    n_chunks = hidden // MXU_N

    def project(c):
        return [_dot(xb, wup_ref[:, base + c * MXU_N:base + (c + 1) * MXU_N]) for base in (0, hidden)]

    def finish(c, hs):
        halves = []
        for base, h in zip((0, hidden), hs):
            cols = slice(base + c * MXU_N, base + (c + 1) * MXU_N)
            halves.append(_causal_conv(h, cw_ref[:, cols], FFN_CONV) + cb_ref[:, cols])
        act_ref[:, c * MXU_N:(c + 1) * MXU_N] = (_silu(halves[0]) * halves[1]).astype(BF16)

    for c in range(n_chunks):
        finish(c, project(c))
    f = _dot(act_ref[...], wdn_ref[...])
    z = DEEPNORM_ALPHA * x_ref[...] + f
    o_ref[...] = _layer_norm_rows(z, g_ref[...], b_ref[...])


def _ffn(x2, w_up, conv_w, conv_b, w_down, g, b, tm, seq_tiles):
    n, d = x2.shape
    hidden = w_down.shape[0]
    kern = functools.partial(_ffn_kernel, seq_tiles=seq_tiles, hidden=hidden)
    row = pl.BlockSpec((tm, d), lambda i: (i, 0))
    single = lambda shape: pl.BlockSpec(shape, lambda *_: (0,) * len(shape),
                                        pipeline_mode=pl.Buffered(1))
    return pl.pallas_call(
        kern,
        grid=(n // tm,),
        in_specs=[row, _halo_spec(tm, d), single((d, 2 * hidden)),
                  _const_spec((FFN_CONV, 2 * hidden)), _const_spec((1, 2 * hidden)),
                  single((hidden, d)), _const_spec((1, d)), _const_spec((1, d))],
        out_specs=row,
        out_shape=jax.ShapeDtypeStruct((n, d), F32),
        scratch_shapes=[pltpu.VMEM((SUBLANES + tm, d), BF16), pltpu.VMEM((tm, hidden), BF16)],
        compiler_params=_params(1),
        name="conv_ffn",
    )(x2, x2, w_up.astype(BF16), conv_w.astype(F32), conv_b.astype(F32).reshape(1, 2 * hidden),
      w_down.astype(BF16), g.astype(F32).reshape(1, d), b.astype(F32).reshape(1, d))


def _diff_in_kernel(x_ref, w_ref, c_ref, s1_ref, s2_ref, qk_ref, v_ref, *, n_qk, n_v, q_cols):
    xb = x_ref[...].astype(BF16)
    cos = c_ref[...]
    s1 = s1_ref[...]
    s2 = s2_ref[...]
    half = DIFF_HD // 2
    for c in range(n_qk):
        h = _dot(xb, w_ref[:, c * MXU_N:(c + 1) * MXU_N])
        for j in range(MXU_N // LANES):
            hs = h[:, j * LANES:(j + 1) * LANES]
            rot = hs * cos + pltpu.roll(hs, LANES - half, 1) * s1 + pltpu.roll(hs, half, 1) * s2
            lo = c * MXU_N + j * LANES
            if lo < q_cols:
                rot = rot * (DIFF_HD ** -0.5 * math.log2(math.e))
            qk_ref[:, lo:lo + LANES] = rot.astype(BF16)
    base = n_qk * MXU_N
    for c in range(n_v):
        v_ref[:, c * MXU_N:(c + 1) * MXU_N] = _dot(
            xb, w_ref[:, base + c * MXU_N:base + (c + 1) * MXU_N]).astype(BF16)


def _diff_in(x2, w_in, tables, tm):
    n, d = x2.shape
    hw = DIFF_HEADS * 2 * DIFF_HD
    kern = functools.partial(_diff_in_kernel, n_qk=2 * hw // MXU_N, n_v=hw // MXU_N, q_cols=hw)
    row = lambda w: pl.BlockSpec((tm, w), lambda i: (i, 0))
    return pl.pallas_call(
        kern,
        grid=(n // tm,),
        in_specs=[row(d), _const_spec((d, 3 * hw)), row(LANES), row(LANES), row(LANES)],
        out_specs=[row(2 * hw), row(hw)],
        out_shape=[jax.ShapeDtypeStruct((n, 2 * hw), BF16), jax.ShapeDtypeStruct((n, hw), BF16)],
        compiler_params=_params(1),
        name="diff_in",
    )(x2, w_in.astype(BF16), *tables)


def _diff_attn_kernel(q_ref, k_ref, v_ref, lq1_ref, lk1_ref, lq2_ref, lk2_ref, sw_ref, o_ref,
                      *, t, blk, lambda_init):
    lam = (jnp.exp(jnp.sum(lq1_ref[...] * lk1_ref[...], axis=-1, keepdims=True))
           - jnp.exp(jnp.sum(lq2_ref[...] * lk2_ref[...], axis=-1, keepdims=True))
           + lambda_init)
    lane = lax.broadcasted_iota(jnp.int32, (blk, 2 * DIFF_HD), 1)
    first_map = lane < DIFF_HD
    r = lax.broadcasted_iota(jnp.int32, (2 * blk, blk), 0)
    s = lax.broadcasted_iota(jnp.int32, (2 * blk, blk), 1)
    diag_ok = s <= (r % blk)
    sw = sw_ref[...]

    def scores(qi):
        q = q_ref[0, qi * blk:(qi + 1) * blk, :]
        zero = jnp.zeros_like(q)
        qs = jnp.concatenate([jnp.where(first_map, q, zero), jnp.where(first_map, zero, q)], axis=0)
        d0 = qi * blk
        sc_d = jnp.where(diag_ok, _dot_nt(qs, k_ref[0, d0:d0 + blk, :]), -jnp.inf)
        sc_f = _dot_nt(qs, k_ref[0, 0:d0, :]) if qi > 0 else None
        return sc_d, sc_f

    def softmax(qi, sc_d, sc_f):
        m = jnp.max(sc_d, axis=-1, keepdims=True)
        if qi > 0:
            m = jnp.maximum(m, jnp.max(sc_f, axis=-1, keepdims=True))
        p_d = jnp.exp2(sc_d - m)
        l = jnp.sum(p_d, axis=-1, keepdims=True)
        p_f = None
        if qi > 0:
            p_f = jnp.exp2(sc_f - m)
            l = l + jnp.sum(p_f, axis=-1, keepdims=True)
            p_f = p_f.astype(BF16)
        return p_d.astype(BF16), p_f, l

    def weighted_values(qi, p_d, p_f, l):
        d0 = qi * blk
        acc = _dot(p_d, v_ref[0, d0:d0 + blk, :])
        if qi > 0:
            acc = acc + _dot(p_f, v_ref[0, 0:d0, :])
        on = acc / l
        o = on[:blk] - lam * on[blk:]
        ms = jnp.mean(o * o, axis=-1, keepdims=True)
        o = o * lax.rsqrt(ms + RMS_EPS) * sw * (1.0 - lambda_init)
        o_ref[0, qi * blk:(qi + 1) * blk, :] = o.astype(BF16)

    n_q = t // blk
    sc = {}
    for step in range(n_q + ATTN_LEAD):
        if step < n_q:
            sc[step] = scores(step)
        qi = step - ATTN_LEAD
        if qi >= 0:
            weighted_values(qi, *softmax(qi, *sc.pop(qi)))


def _diff_attn(qk, v, lam_q1, lam_k1, lam_q2, lam_k2, subln_w, lambda_init, b, t):
    hw = DIFF_HEADS * 2 * DIFF_HD
    hd2 = 2 * DIFF_HD
    blk = ATTN_BLOCK if t % ATTN_BLOCK == 0 else t
    qk3 = qk.reshape(b, t, 2 * hw)
    v3 = v.reshape(b, t, hw)
    kern = functools.partial(_diff_attn_kernel, t=t, blk=blk, lambda_init=lambda_init)
    head = lambda off: pl.BlockSpec((1, t, hd2), lambda bi, hi: (bi, 0, hi + off))
    vec = lambda a: a.astype(F32).reshape(1, -1)
    out = pl.pallas_call(
        kern,
        grid=(b, DIFF_HEADS),
        in_specs=[head(0), head(DIFF_HEADS), head(0)] + [_const_spec((1, DIFF_HD))] * 4
                 + [_const_spec((1, hd2))],
        out_specs=head(0),
        out_shape=jax.ShapeDtypeStruct((b, t, hw), BF16),
        compiler_params=_params(2),
        name="diff_attn",
    )(qk3, qk3, v3, vec(lam_q1), vec(lam_k1), vec(lam_q2), vec(lam_k2), vec(subln_w))
    return out.reshape(b * t, hw)


def kernel(x, positions, gdn_w_in, gdn_conv_w, gdn_a_log, gdn_dt_bias, gdn_norm_w, gdn_w_out,
           diff_w_in, diff_lam_q1, diff_lam_k1, diff_lam_q2, diff_lam_k2, diff_subln_w, diff_w_out,
           ffn_w_up, ffn_conv_w, ffn_conv_b, ffn_w_down, ln_mix_g, ln_mix_b, ln_ffn_g, ln_ffn_b):
    b, t, d = x.shape
    tm = _row_tile(t)
    seq_tiles = t // tm
    x2 = x.reshape(b * t, d).astype(F32)
    tables = _rope_tables(positions, tm)
    proj_tile = PROJ_TILE if t % PROJ_TILE == 0 else tm
    for i in range(DEPTH):
        j = i // 2
        if i % 2 == 0:
            qkv, gate, gb = _gdn_in(x2, gdn_w_in[j], gdn_conv_w[j], gdn_a_log[j], gdn_dt_bias[j],
                                    proj_tile, t // proj_tile)
            mixed = _gdn_chunk(qkv, gate, gb, gdn_norm_w[j], b, t)
            w_out = gdn_w_out[j]
        else:
            lambda_init = 0.8 - 0.6 * math.exp(-0.3 * i)
            qk, v = _diff_in(x2, diff_w_in[j], tables, tm)
            mixed = _diff_attn(qk, v, diff_lam_q1[j], diff_lam_k1[j], diff_lam_q2[j], diff_lam_k2[j],
                               diff_subln_w[j], lambda_init, b, t)
            w_out = diff_w_out[j]
        x2 = _proj_ln(mixed, w_out, x2, ln_mix_g[i], ln_mix_b[i], proj_tile)
        x2 = _ffn(x2, ffn_w_up[i], ffn_conv_w[i], ffn_conv_b[i], ffn_w_down[i],
                  ln_ffn_g[i], ln_ffn_b[i], tm, seq_tiles)
    return x2.reshape(b, t, d)
```

```python
import functools
import math

import jax
import jax.numpy as jnp
from jax import lax
from jax.experimental import pallas as pl
from jax.experimental.pallas import tpu as pltpu

F32 = jnp.float32
BF16 = jnp.bfloat16

DEPTH = 4
GDN_HEADS = 8
GDN_DK = 128
GDN_DV = 128
GDN_CONV = 4
GDN_CHUNK = 64
DIFF_HEADS = 8
DIFF_HD = 64
ROPE_THETA = 10000.0
FFN_CONV = 3
DEEPNORM_ALPHA = (2.0 * DEPTH) ** 0.25
LN_EPS = 1e-5
RMS_EPS = 1e-6

LANES = 128
SUBLANES = 8
MXU_N = 256
VMEM_LIMIT = 56 * 1024 * 1024

ROW_TILE = 1024
ATTN_LEAD = 1
ATTN_BLOCK = 256
GDN_STEP = 512


def _row_tile(t):
    return ROW_TILE if t % ROW_TILE == 0 else t


def _sigmoid(x):
    return 1.0 / (1.0 + jnp.exp(-x))


def _silu(x):
    return x * _sigmoid(x)


def _dot(a, b):
    return jnp.dot(a, b, preferred_element_type=F32)


def _dot_nt(a, b):
    return lax.dot_general(a, b, (((1,), (1,)), ((), ())), preferred_element_type=F32)


def _shift_rows(h, s):
    return pltpu.roll(h, s, 0)


def _causal_conv(h, cw, width):
    y = h * cw[width - 1:width, :]
    for s in range(1, width):
        y = y + _shift_rows(h, s) * cw[width - 1 - s:width - s, :]
    return y[SUBLANES:, :]


def _layer_norm_rows(z, g, b):
    mu = jnp.mean(z, axis=-1, keepdims=True)
    zc = z - mu
    var = jnp.mean(zc * zc, axis=-1, keepdims=True)
    return zc * lax.rsqrt(var + LN_EPS) * g + b


def _stage_rows(x_ref, xh_ref, xb_ref, tile, seq_tiles):
    first = (tile % seq_tiles) == 0
    halo = jnp.where(first, 0.0, xh_ref[...])
    xb_ref[0:SUBLANES, :] = halo.astype(BF16)
    xb_ref[SUBLANES:, :] = x_ref[...].astype(BF16)


def _halo_spec(tm, d):
    return pl.BlockSpec((SUBLANES, d), lambda i: (jnp.maximum(i * (tm // SUBLANES) - 1, 0), 0))


def _const_spec(shape):
    return pl.BlockSpec(shape, lambda *_: (0,) * len(shape))


def _resident_spec(shape):
    return pl.BlockSpec(shape, lambda *_: (0,) * len(shape), pipeline_mode=pl.Buffered(1))


def _params(n_axes):
    return pltpu.CompilerParams(
        dimension_semantics=("arbitrary",) * n_axes, vmem_limit_bytes=VMEM_LIMIT)


def _rope_kernel(pos_ref, inv_ref, c_ref, s1_ref, s2_ref):
    tm = pos_ref.shape[0]
    nf = DIFF_HD // 2
    groups = LANES // nf
    blk = tm // groups
    lane = lax.broadcasted_iota(jnp.int32, (blk, LANES), 1)
    pos = pos_ref[...].astype(F32)
    packed = jnp.zeros((blk, LANES), F32)
    for j in range(groups):
        packed = jnp.where(lane // nf == j, jnp.broadcast_to(pos[j * blk:(j + 1) * blk, :], (blk, LANES)),
                           packed)
    ang = packed * inv_ref[...]
    cos = jnp.cos(ang)
    sin = jnp.sin(ang)
    lower = (lane % DIFF_HD) < nf
    for j in range(groups):
        cj, sj = cos, sin
        for g in range(groups):
            shift = ((g - j) % groups) * nf
            if shift:
                cj = jnp.where(lane // nf == g, pltpu.roll(cos, shift, 1), cj)
                sj = jnp.where(lane // nf == g, pltpu.roll(sin, shift, 1), sj)
        rows = slice(j * blk, (j + 1) * blk)
        c_ref[rows, :] = cj
        s1_ref[rows, :] = jnp.where(lower, -sj, 0.0)
        s2_ref[rows, :] = jnp.where(lower, 0.0, sj)


def _rope_tables(positions, tm):
    n = positions.size
    inv_freq = ROPE_THETA ** (-jnp.arange(0, DIFF_HD, 2, dtype=F32) / DIFF_HD)
    inv = jnp.tile(inv_freq, LANES // (DIFF_HD // 2)).reshape(1, LANES)
    pos = positions.reshape(n, 1)
    out = jax.ShapeDtypeStruct((n, LANES), F32)
    return pl.pallas_call(
        _rope_kernel,
        grid=(n // tm,),
        in_specs=[pl.BlockSpec((tm, 1), lambda i: (i, 0)), _const_spec((1, LANES))],
        out_specs=[pl.BlockSpec((tm, LANES), lambda i: (i, 0))] * 3,
        out_shape=[out] * 3,
        compiler_params=_params(1),
        name="rope_tables",
    )(pos, inv)


def _gdn_in_kernel(x_ref, xh_ref, w_ref, wab_ref, cw_ref, alog_ref, dtb_ref,
                   qkv_ref, gate_ref, gb_ref, xb_ref, *, seq_tiles, n_qk, n_v, n_gate):
    _stage_rows(x_ref, xh_ref, xb_ref, pl.program_id(0), seq_tiles)
    xb = xb_ref[...]
    xt = xb[SUBLANES:, :]
    base = (n_qk + n_v) * MXU_N
    conv_per_gate = (n_qk + n_v) // n_gate
    jobs = []
    for c in range(n_qk + n_v):
        jobs.append(("conv", c))
        if c % conv_per_gate == conv_per_gate - 1:
            jobs.append(("gate", c // conv_per_gate))

    def project(job):
        kind, c = job
        if kind == "conv":
            return _dot(xb, w_ref[:, c * MXU_N:(c + 1) * MXU_N])
        return _dot(xt, w_ref[:, base + c * MXU_N:base + (c + 1) * MXU_N])

    def finish(job, h):
        kind, c = job
        cols = slice(c * MXU_N, (c + 1) * MXU_N)
        if kind == "gate":
            gate_ref[:, cols] = h.astype(BF16)
            return
        y = _silu(_causal_conv(h, cw_ref[:, cols], GDN_CONV))
        if c < n_qk:
            parts = []
            for j in range(MXU_N // GDN_DK):
                yh = y[:, j * GDN_DK:(j + 1) * GDN_DK]
                ss = jnp.sum(yh * yh, axis=-1, keepdims=True)
                parts.append(yh * lax.rsqrt(ss + RMS_EPS))
            y = jnp.concatenate(parts, axis=1)
        qkv_ref[:, cols] = y.astype(BF16)

    for job in jobs:
        finish(job, project(job))
    hab = _dot(xt, wab_ref[...])
    z = hab + dtb_ref[...]
    softplus = jnp.maximum(z, 0.0) + jnp.log(1.0 + jnp.exp(-jnp.abs(z)))
    g = -jnp.exp(alog_ref[...]) * softplus
    lane = lax.broadcasted_iota(jnp.int32, hab.shape, 1)
    gb_ref[...] = jnp.where(lane < GDN_HEADS, g, _sigmoid(hab))


def _gdn_in(x2, w_in, conv_w, a_log, dt_bias, tm, seq_tiles):
    n, d = x2.shape
    qk_w = 2 * GDN_HEADS * GDN_DK
    v_w = GDN_HEADS * GDN_DV
    main = qk_w + 2 * v_w
    w_main = w_in[:, :main].astype(BF16)
    w_ab = jnp.pad(w_in[:, main:], ((0, 0), (0, LANES - 2 * GDN_HEADS))).astype(BF16)
    pad = (0, LANES - GDN_HEADS)
    alog = jnp.pad(a_log.astype(F32), pad).reshape(1, LANES)
    dtb = jnp.pad(dt_bias.astype(F32), pad).reshape(1, LANES)
    kern = functools.partial(_gdn_in_kernel, seq_tiles=seq_tiles, n_qk=qk_w // MXU_N,
                             n_v=v_w // MXU_N, n_gate=v_w // MXU_N)
    row = lambda w: pl.BlockSpec((tm, w), lambda i: (i, 0))
    return pl.pallas_call(
        kern,
        grid=(n // tm,),
        in_specs=[row(d), _halo_spec(tm, d),
                  _resident_spec((d, main)),
                  _const_spec((d, LANES)),
                  _const_spec((GDN_CONV, qk_w + v_w)), _const_spec((1, LANES)),
                  _const_spec((1, LANES))],
        out_specs=[row(qk_w + v_w), row(v_w), row(LANES)],
        out_shape=[jax.ShapeDtypeStruct((n, qk_w + v_w), BF16),
                   jax.ShapeDtypeStruct((n, v_w), BF16),
                   jax.ShapeDtypeStruct((n, LANES), F32)],
        scratch_shapes=[pltpu.VMEM((SUBLANES + tm, d), BF16)],
        compiler_params=_params(1),
        name="gdn_in",
    )(x2, x2, w_main, w_ab, conv_w.astype(F32), alog, dtb)


def _unit_lower_inverses(a_list, eye, blk16, off32, off64, bdmask):
    c = GDN_CHUNK

    def bd(y):
        yy = jnp.concatenate([y, y], axis=0)
        return jnp.where(bdmask, yy, jnp.zeros_like(yy))

    n = [jnp.where(blk16, -a, 0.0) for a in a_list]
    p = [eye + x for x in n]
    nb = [x.astype(BF16) for x in n]
    n = [_dot(x, bd(x)) for x in nb]
    for _ in range(2):
        nb = [x.astype(BF16) for x in n]
        r = [_dot(jnp.concatenate([pi.astype(BF16), ni], axis=0), bd(ni)) for pi, ni in zip(p, nb)]
        p = [pi + ri[:c] for pi, ri in zip(p, r)]
        n = [ri[c:] for ri in r]
    p = [pi + _dot(pi.astype(BF16), bd(ni.astype(BF16))) for pi, ni in zip(p, n)]
    for off in (off32, off64):
        pb = [pi.astype(BF16) for pi in p]
        t = [_dot(jnp.where(off, a, 0.0).astype(BF16), bd(pbi)) for a, pbi in zip(a_list, pb)]
        p = [pi - _dot(pbi, bd(ti.astype(BF16))) for pi, pbi, ti in zip(p, pb, t)]
    return p


def _gdn_chunk_kernel(q_ref, k_ref, v_ref, gate_ref, gb_ref, nw_ref,
                      o_ref, s_ref, wq_ref, u_ref, ak_ref, *, n_sub):
    c = GDN_CHUNK

    @pl.when(pl.program_id(1) == 0)
    def _():
        s_ref[...] = jnp.zeros_like(s_ref)

    step = n_sub * c
    gb = gb_ref[0]
    in_chunk = lax.broadcasted_iota(jnp.int32, (step, LANES), 0) % c
    gc_cols = gb
    shift = 1
    while shift < c:
        gc_cols = gc_cols + jnp.where(in_chunk >= shift, pltpu.roll(gc_cols, shift, 0), 0.0)
        shift *= 2
    gc_rows = gc_cols.T[:GDN_HEADS, :]
    egc_cols = jnp.exp(gc_cols)
    bl = GDN_HEADS

    c2 = 2 * c
    r = lax.broadcasted_iota(jnp.int32, (c, c2), 0)
    lane = lax.broadcasted_iota(jnp.int32, (c, c2), 1)
    s = lane % c
    first = lane < c
    causal = s <= r
    strict = s < r
    eye = jnp.where(r == s, 1.0, 0.0)
    blk16 = (r // 16) == (s // 16)
    off32 = ((r // 32) == (s // 32)) & ((r // 16) != (s // 16))
    off64 = (r // 32) != (s // 32)
    bdmask = (lax.broadcasted_iota(jnp.int32, (c2, c2), 0) // c
              == lax.broadcasted_iota(jnp.int32, (c2, c2), 1) // c)
    kmask = (lax.broadcasted_iota(jnp.int32, (c2, 2 * GDN_DK), 0) // c
             == lax.broadcasted_iota(jnp.int32, (c2, 2 * GDN_DK), 1) // GDN_DK)
    scale = GDN_DK ** -0.5
    heads = range(GDN_HEADS)
    hcols = [slice(h * GDN_DK, (h + 1) * GDN_DK) for h in heads]
    n_hp = GDN_HEADS // 2
    pcols = [slice(hp * 2 * GDN_DK, (hp + 1) * 2 * GDN_DK) for hp in range(n_hp)]

    def pack_cols(x, rw, lane0):
        return jnp.where(first, jnp.broadcast_to(x[rw, lane0:lane0 + 1], (c, c2)),
                         jnp.broadcast_to(x[rw, lane0 + 1:lane0 + 2], (c, c2)))

    units = [(sub, hp) for sub in range(n_sub) for hp in range(n_hp)]
    rws = [slice(sub * c, (sub + 1) * c) for sub, _ in units]
    q2 = [q_ref[0, rw, pcols[hp]] for rw, (_, hp) in zip(rws, units)]
    k2 = [k_ref[0, rw, pcols[hp]] for rw, (_, hp) in zip(rws, units)]
    kbd = []
    for k in k2:
        kk = jnp.concatenate([k, k], axis=0)
        kbd.append(jnp.where(kmask, kk, jnp.zeros_like(kk)))
    qkk = [_dot_nt(jnp.concatenate([q, k], axis=0), kb) for q, k, kb in zip(q2, k2, kbd)]
    decay = []
    for rw, (_, hp) in zip(rws, units):
        gcr = jnp.concatenate([gc_rows[2 * hp:2 * hp + 1, rw], gc_rows[2 * hp + 1:2 * hp + 2, rw]], axis=1)
        diff = jnp.where(causal, pack_cols(gc_cols, rw, 2 * hp) - gcr, 0.0)
        decay.append(jnp.where(causal, jnp.exp(diff), 0.0))
    a_list = [jnp.where(strict, x[c:] * dc * pack_cols(gb, rw, bl + 2 * hp), 0.0)
              for x, dc, rw, (_, hp) in zip(qkk, decay, rws, units)]
    attn = [(x[:c] * scale * dc).astype(BF16) for x, dc in zip(qkk, decay)]
    inv = _unit_lower_inverses(a_list, eye, blk16, off32, off64, bdmask)
    zeros = jnp.zeros((c, GDN_DV + GDN_DK), BF16)
    rhs, kfs = [], []
    for rw, (_, hp) in zip(rws, units):
        both = []
        for h in (2 * hp, 2 * hp + 1):
            bcol = gb[rw, bl + h:bl + h + 1]
            kf = k_ref[0, rw, hcols[h]].astype(F32)
            kfs.append(kf)
            both.append(jnp.concatenate([v_ref[0, rw, hcols[h]].astype(F32) * bcol,
                                         kf * (bcol * egc_cols[rw, h:h + 1])], axis=1).astype(BF16))
        rhs.append(jnp.concatenate([jnp.concatenate([both[0], zeros], axis=1),
                                    jnp.concatenate([zeros, both[1]], axis=1)], axis=0))
    sol = [_dot(t.astype(BF16), x) for t, x in zip(inv, rhs)]
    for i, (rw, (sub, hp)) in enumerate(zip(rws, units)):
        kds = []
        for j, h in enumerate((2 * hp, 2 * hp + 1)):
            base = j * (GDN_DV + GDN_DK)
            g_last = gc_cols[(sub + 1) * c - 1:(sub + 1) * c, h:h + 1]
            u_ref[sub, h] = sol[i][:, base:base + GDN_DV]
            qd = (q_ref[0, rw, hcols[h]].astype(F32) * (egc_cols[rw, h:h + 1] * scale)).astype(BF16)
            wq_ref[sub, h] = jnp.concatenate(
                [sol[i][:, base + GDN_DV:base + GDN_DV + GDN_DK].astype(BF16), qd], axis=0)
            kds.append(kfs[2 * i + j] * jnp.exp(g_last - gc_cols[rw, h:h + 1]))
        kdt = jnp.concatenate(kds, axis=0).T
        ak_ref[sub, hp] = jnp.concatenate([attn[i], kdt.astype(BF16)], axis=0)

    nw = nw_ref[...]
    zv = jnp.zeros((c, GDN_DV), BF16)
    for sub in range(n_sub):
        rows = slice(sub * c, (sub + 1) * c)
        g_last = gc_cols[(sub + 1) * c - 1:(sub + 1) * c, :]
        st = [s_ref[h] for h in heads]
        r1 = [_dot(wq_ref[sub, h], st[h].astype(BF16)) for h in heads]
        vb = [(u_ref[sub, h] - r1[h][:c]).astype(BF16) for h in heads]
        r2p = []
        for hp in range(n_hp):
            w2 = jnp.concatenate([jnp.concatenate([vb[2 * hp], zv], axis=1),
                                  jnp.concatenate([zv, vb[2 * hp + 1]], axis=1)], axis=0)
            r2p.append(_dot(ak_ref[sub, hp], w2))
        r2 = [r2p[h // 2][:, (h % 2) * GDN_DV:(h % 2 + 1) * GDN_DV] for h in heads]
        for h in heads:
            s_ref[h] = st[h] * jnp.exp(g_last[:, h:h + 1]) + r2[h][c:]
            o = r1[h][c:] + r2[h][:c]
            ms = jnp.mean(o * o, axis=-1, keepdims=True)
            on = o * lax.rsqrt(ms + RMS_EPS) * nw
            gt = gate_ref[0, rows, hcols[h]].astype(F32)
            o_ref[0, rows, hcols[h]] = (on * _silu(gt)).astype(BF16)


def _gdn_chunk(qkv, gate, gb, norm_w, b, t):
    hq = GDN_HEADS * GDN_DK
    hv = GDN_HEADS * GDN_DV
    step = GDN_STEP if t % GDN_STEP == 0 else t
    qkv3 = qkv.reshape(b, t, 2 * hq + hv)
    gate3 = gate.reshape(b, t, hv)
    gb3 = gb.reshape(b, t, LANES)
    n_sub = step // GDN_CHUNK
    kern = functools.partial(_gdn_chunk_kernel, n_sub=n_sub)
    col = lambda j: pl.BlockSpec((1, step, hq), lambda bi, ci: (bi, ci, j))
    out = pl.pallas_call(
        kern,
        grid=(b, t // step),
        in_specs=[col(0), col(1), col(2), col(0),
                  pl.BlockSpec((1, step, LANES), lambda bi, ci: (bi, ci, 0)),
                  _const_spec((1, GDN_DV))],
        out_specs=col(0),
        out_shape=jax.ShapeDtypeStruct((b, t, hv), BF16),
        scratch_shapes=[pltpu.VMEM((GDN_HEADS, GDN_DK, GDN_DV), F32),
                        pltpu.VMEM((n_sub, GDN_HEADS, 2 * GDN_CHUNK, GDN_DK), BF16),
                        pltpu.VMEM((n_sub, GDN_HEADS, GDN_CHUNK, GDN_DV), F32),
                        pltpu.VMEM((n_sub, GDN_HEADS // 2, GDN_CHUNK + GDN_DK, 2 * GDN_CHUNK), BF16)],
        compiler_params=_params(2),
        name="gdn_chunk",
    )(qkv3, qkv3, qkv3, gate3, gb3, norm_w.astype(F32).reshape(1, GDN_DV))
    return out.reshape(b * t, hv)


def _proj_ln_kernel(y_ref, w_ref, x_ref, g_ref, b_ref, o_ref):
    m = _dot(y_ref[...], w_ref[...])
    z = DEEPNORM_ALPHA * x_ref[...] + m
    o_ref[...] = _layer_norm_rows(z, g_ref[...], b_ref[...])


def _proj_ln(y, w, x2, g, b, tm):
    n, d = x2.shape
    k = y.shape[1]
    row = lambda w_: pl.BlockSpec((tm, w_), lambda i: (i, 0))
    return pl.pallas_call(
        _proj_ln_kernel,
        grid=(n // tm,),
        in_specs=[row(k), _resident_spec((k, d)), row(d), _const_spec((1, d)), _const_spec((1, d))],
        out_specs=row(d),
        out_shape=jax.ShapeDtypeStruct((n, d), F32),
        compiler_params=_params(1),
        name="proj_ln",
    )(y, w.astype(BF16), x2, g.astype(F32).reshape(1, d), b.astype(F32).reshape(1, d))


def _ffn_kernel(x_ref, xh_ref, wup_ref, cw_ref, cb_ref, wdn_ref, g_ref, b_ref, o_ref,
                xb_ref, act_ref, *, seq_tiles, hidden):
    _stage_rows(x_ref, xh_ref, xb_ref, pl.program_id(0), seq_tiles)
    xb = xb_ref[...]
    n_chunks = hidden // MXU_N

    def project(c):
        return [_dot(xb, wup_ref[:, base + c * MXU_N:base + (c + 1) * MXU_N]) for base in (0, hidden)]

    def finish(c, hs):
        halves = []
        for base, h in zip((0, hidden), hs):
            cols = slice(base + c * MXU_N, base + (c + 1) * MXU_N)
            halves.append(_causal_conv(h, cw_ref[:, cols], FFN_CONV) + cb_ref[:, cols])
        act_ref[:, c * MXU_N:(c + 1) * MXU_N] = (_silu(halves[0]) * halves[1]).astype(BF16)

    for c in range(n_chunks):
        finish(c, project(c))
    f = _dot(act_ref[...], wdn_ref[...])
    z = DEEPNORM_ALPHA * x_ref[...] + f
    o_ref[...] = _layer_norm_rows(z, g_ref[...], b_ref[...])


def _ffn(x2, w_up, conv_w, conv_b, w_down, g, b, tm, seq_tiles):
    n, d = x2.shape
    hidden = w_down.shape[0]
    kern = functools.partial(_ffn_kernel, seq_tiles=seq_tiles, hidden=hidden)
    row = pl.BlockSpec((tm, d), lambda i: (i, 0))
    return pl.pallas_call(
        kern,
        grid=(n // tm,),
        in_specs=[row, _halo_spec(tm, d), _resident_spec((d, 2 * hidden)),
                  _const_spec((FFN_CONV, 2 * hidden)), _const_spec((1, 2 * hidden)),
                  _resident_spec((hidden, d)), _const_spec((1, d)), _const_spec((1, d))],
        out_specs=row,
        out_shape=jax.ShapeDtypeStruct((n, d), F32),
        scratch_shapes=[pltpu.VMEM((SUBLANES + tm, d), BF16), pltpu.VMEM((tm, hidden), BF16)],
        compiler_params=_params(1),
        name="conv_ffn",
    )(x2, x2, w_up.astype(BF16), conv_w.astype(F32), conv_b.astype(F32).reshape(1, 2 * hidden),
      w_down.astype(BF16), g.astype(F32).reshape(1, d), b.astype(F32).reshape(1, d))


def _diff_in_kernel(x_ref, w_ref, c_ref, s1_ref, s2_ref, qk_ref, v_ref, *, n_qk, n_v, q_cols):
    xb = x_ref[...].astype(BF16)
    cos = c_ref[...]
    s1 = s1_ref[...]
    s2 = s2_ref[...]
    half = DIFF_HD // 2
    for c in range(n_qk):
        h = _dot(xb, w_ref[:, c * MXU_N:(c + 1) * MXU_N])
        for j in range(MXU_N // LANES):
            hs = h[:, j * LANES:(j + 1) * LANES]
            rot = hs * cos + pltpu.roll(hs, LANES - half, 1) * s1 + pltpu.roll(hs, half, 1) * s2
            lo = c * MXU_N + j * LANES
            if lo < q_cols:
                rot = rot * (DIFF_HD ** -0.5 * math.log2(math.e))
            qk_ref[:, lo:lo + LANES] = rot.astype(BF16)
    base = n_qk * MXU_N
    for c in range(n_v):
        v_ref[:, c * MXU_N:(c + 1) * MXU_N] = _dot(
            xb, w_ref[:, base + c * MXU_N:base + (c + 1) * MXU_N]).astype(BF16)


def _diff_in(x2, w_in, tables, tm):
    n, d = x2.shape
    hw = DIFF_HEADS * 2 * DIFF_HD
    kern = functools.partial(_diff_in_kernel, n_qk=2 * hw // MXU_N, n_v=hw // MXU_N, q_cols=hw)
    row = lambda w: pl.BlockSpec((tm, w), lambda i: (i, 0))
    return pl.pallas_call(
        kern,
        grid=(n // tm,),
        in_specs=[row(d), _resident_spec((d, 3 * hw)), row(LANES), row(LANES), row(LANES)],
        out_specs=[row(2 * hw), row(hw)],
        out_shape=[jax.ShapeDtypeStruct((n, 2 * hw), BF16), jax.ShapeDtypeStruct((n, hw), BF16)],
        compiler_params=_params(1),
        name="diff_in",
    )(x2, w_in.astype(BF16), *tables)


def _diff_attn_kernel(q_ref, k_ref, v_ref, lq1_ref, lk1_ref, lq2_ref, lk2_ref, sw_ref, o_ref,
                      *, t, blk, lambda_init):
    lam = (jnp.exp(jnp.sum(lq1_ref[...] * lk1_ref[...], axis=-1, keepdims=True))
           - jnp.exp(jnp.sum(lq2_ref[...] * lk2_ref[...], axis=-1, keepdims=True))
           + lambda_init)
    lane = lax.broadcasted_iota(jnp.int32, (blk, 2 * DIFF_HD), 1)
    first_map = lane < DIFF_HD
    r = lax.broadcasted_iota(jnp.int32, (2 * blk, blk), 0)
    s = lax.broadcasted_iota(jnp.int32, (2 * blk, blk), 1)
    diag_ok = s <= (r % blk)
    sw = sw_ref[...]

    def scores(qi):
        q = q_ref[0, qi * blk:(qi + 1) * blk, :]
        zero = jnp.zeros_like(q)
        qs = jnp.concatenate([jnp.where(first_map, q, zero), jnp.where(first_map, zero, q)], axis=0)
        d0 = qi * blk
        sc_d = jnp.where(diag_ok, _dot_nt(qs, k_ref[0, d0:d0 + blk, :]), -jnp.inf)
        sc_f = _dot_nt(qs, k_ref[0, 0:d0, :]) if qi > 0 else None
        return sc_d, sc_f

    def softmax(qi, sc_d, sc_f):
        m = jnp.max(sc_d, axis=-1, keepdims=True)
        if qi > 0:
            m = jnp.maximum(m, jnp.max(sc_f, axis=-1, keepdims=True))
        p_d = jnp.exp2(sc_d - m)
        l = jnp.sum(p_d, axis=-1, keepdims=True)
        p_f = None
        if qi > 0:
            p_f = jnp.exp2(sc_f - m)
            l = l + jnp.sum(p_f, axis=-1, keepdims=True)
            p_f = p_f.astype(BF16)
        return p_d.astype(BF16), p_f, l

    def weighted_values(qi, p_d, p_f, l):
        d0 = qi * blk
        acc = _dot(p_d, v_ref[0, d0:d0 + blk, :])
        if qi > 0:
            acc = acc + _dot(p_f, v_ref[0, 0:d0, :])
        on = acc / l
        o = on[:blk] - lam * on[blk:]
        ms = jnp.mean(o * o, axis=-1, keepdims=True)
        o = o * lax.rsqrt(ms + RMS_EPS) * sw * (1.0 - lambda_init)
        o_ref[0, qi * blk:(qi + 1) * blk, :] = o.astype(BF16)

    n_q = t // blk
    sc = {}
    for step in range(n_q + ATTN_LEAD):
        if step < n_q:
            sc[step] = scores(step)
        qi = step - ATTN_LEAD
        if qi >= 0:
            weighted_values(qi, *softmax(qi, *sc.pop(qi)))


def _diff_attn(qk, v, lam_q1, lam_k1, lam_q2, lam_k2, subln_w, lambda_init, b, t):
    hw = DIFF_HEADS * 2 * DIFF_HD
    hd2 = 2 * DIFF_HD
    blk = ATTN_BLOCK if t % ATTN_BLOCK == 0 else t
    qk3 = qk.reshape(b, t, 2 * hw)
    v3 = v.reshape(b, t, hw)
    kern = functools.partial(_diff_attn_kernel, t=t, blk=blk, lambda_init=lambda_init)
    head = lambda off: pl.BlockSpec((1, t, hd2), lambda bi, hi: (bi, 0, hi + off))
    vec = lambda a: a.astype(F32).reshape(1, -1)
    out = pl.pallas_call(
        kern,
        grid=(b, DIFF_HEADS),
        in_specs=[head(0), head(DIFF_HEADS), head(0)] + [_const_spec((1, DIFF_HD))] * 4
                 + [_const_spec((1, hd2))],
        out_specs=head(0),
        out_shape=jax.ShapeDtypeStruct((b, t, hw), BF16),
        compiler_params=_params(2),
        name="diff_attn",
    )(qk3, qk3, v3, vec(lam_q1), vec(lam_k1), vec(lam_q2), vec(lam_k2), vec(subln_w))
    return out.reshape(b * t, hw)


def kernel(x, positions, gdn_w_in, gdn_conv_w, gdn_a_log, gdn_dt_bias, gdn_norm_w, gdn_w_out,
           diff_w_in, diff_lam_q1, diff_lam_k1, diff_lam_q2, diff_lam_k2, diff_subln_w, diff_w_out,
           ffn_w_up, ffn_conv_w, ffn_conv_b, ffn_w_down, ln_mix_g, ln_mix_b, ln_ffn_g, ln_ffn_b):
    b, t, d = x.shape
    tm = _row_tile(t)
    seq_tiles = t // tm
    x2 = x.reshape(b * t, d).astype(F32)
    tables = _rope_tables(positions, tm)
    for i in range(DEPTH):
        j = i // 2
        if i % 2 == 0:
            qkv, gate, gb = _gdn_in(x2, gdn_w_in[j], gdn_conv_w[j], gdn_a_log[j], gdn_dt_bias[j],
                                    tm, seq_tiles)
            mixed = _gdn_chunk(qkv, gate, gb, gdn_norm_w[j], b, t)
            w_out = gdn_w_out[j]
        else:
            lambda_init = 0.8 - 0.6 * math.exp(-0.3 * i)
            qk, v = _diff_in(x2, diff_w_in[j], tables, tm)
            mixed = _diff_attn(qk, v, diff_lam_q1[j], diff_lam_k1[j], diff_lam_q2[j], diff_lam_k2[j],
                               diff_subln_w[j], lambda_init, b, t)
            w_out = diff_w_out[j]
        x2 = _proj_ln(mixed, w_out, x2, ln_mix_g[i], ln_mix_b[i], tm)
        x2 = _ffn(x2, ffn_w_up[i], ffn_conv_w[i], ffn_conv_b[i], ffn_w_down[i],
                  ln_ffn_g[i], ln_ffn_b[i], tm, seq_tiles)
    return x2.reshape(b, t, d)
```

```python
import functools
import math

import jax
import jax.numpy as jnp
from jax import lax
from jax.experimental import pallas as pl
from jax.experimental.pallas import tpu as pltpu

F32 = jnp.float32
BF16 = jnp.bfloat16

DEPTH = 4
GDN_HEADS = 8
GDN_DK = 128
GDN_DV = 128
GDN_CONV = 4
GDN_CHUNK = 64
DIFF_HEADS = 8
DIFF_HD = 64
ROPE_THETA = 10000.0
FFN_CONV = 3
DEEPNORM_ALPHA = (2.0 * DEPTH) ** 0.25
LN_EPS = 1e-5
RMS_EPS = 1e-6

LANES = 128
SUBLANES = 8
MXU_N = 256
VMEM_LIMIT = 56 * 1024 * 1024

ROW_TILE = 1024
ATTN_LEAD = 1
ATTN_BLOCK = 256
GDN_STEP = 512
GDN_INV_BLOCK = 16


def _row_tile(t):
    return ROW_TILE if t % ROW_TILE == 0 else t


def _sigmoid(x):
    return 1.0 / (1.0 + jnp.exp2(x * -math.log2(math.e)))


def _silu(x):
    return x * _sigmoid(x)


def _dot(a, b):
    return jnp.dot(a, b, preferred_element_type=F32)


def _dot_nt(a, b):
    return lax.dot_general(a, b, (((1,), (1,)), ((), ())), preferred_element_type=F32)


def _shift_rows(h, s):
    return pltpu.roll(h, s, 0)


def _causal_conv(h, cw, width):
    y = h * cw[width - 1:width, :]
    for s in range(1, width):
        y = y + _shift_rows(h, s) * cw[width - 1 - s:width - s, :]
    return y[SUBLANES:, :]


def _layer_norm_rows(z, g, b):
    mu = jnp.mean(z, axis=-1, keepdims=True)
    zc = z - mu
    var = jnp.mean(zc * zc, axis=-1, keepdims=True)
    return zc * lax.rsqrt(var + LN_EPS) * g + b


def _stage_rows(x_ref, xh_ref, xb_ref, tile, seq_tiles):
    first = (tile % seq_tiles) == 0
    halo = jnp.where(first, 0.0, xh_ref[...])
    xb_ref[0:SUBLANES, :] = halo.astype(BF16)
    xb_ref[SUBLANES:, :] = x_ref[...].astype(BF16)


def _halo_spec(tm, d):
    return pl.BlockSpec((SUBLANES, d), lambda i: (jnp.maximum(i * (tm // SUBLANES) - 1, 0), 0))


def _const_spec(shape):
    return pl.BlockSpec(shape, lambda *_: (0,) * len(shape))


def _resident_spec(shape):
    return pl.BlockSpec(shape, lambda *_: (0,) * len(shape), pipeline_mode=pl.Buffered(1))


def _params(n_axes):
    return pltpu.CompilerParams(
        dimension_semantics=("arbitrary",) * n_axes, vmem_limit_bytes=VMEM_LIMIT)


def _rope_kernel(pos_ref, inv_ref, c_ref, s1_ref, s2_ref):
    tm = pos_ref.shape[0]
    nf = DIFF_HD // 2
    groups = LANES // nf
    blk = tm // groups
    lane = lax.broadcasted_iota(jnp.int32, (blk, LANES), 1)
    pos = pos_ref[...].astype(F32)
    packed = jnp.zeros((blk, LANES), F32)
    for j in range(groups):
        packed = jnp.where(lane // nf == j, jnp.broadcast_to(pos[j * blk:(j + 1) * blk, :], (blk, LANES)),
                           packed)
    ang = packed * inv_ref[...]
    cos = jnp.cos(ang)
    sin = jnp.sin(ang)
    lower = (lane % DIFF_HD) < nf
    for j in range(groups):
        cj, sj = cos, sin
        for g in range(groups):
            shift = ((g - j) % groups) * nf
            if shift:
                cj = jnp.where(lane // nf == g, pltpu.roll(cos, shift, 1), cj)
                sj = jnp.where(lane // nf == g, pltpu.roll(sin, shift, 1), sj)
        rows = slice(j * blk, (j + 1) * blk)
        c_ref[rows, :] = cj
        s1_ref[rows, :] = jnp.where(lower, -sj, 0.0)
        s2_ref[rows, :] = jnp.where(lower, 0.0, sj)


def _rope_tables(positions, tm):
    n = positions.size
    inv_freq = ROPE_THETA ** (-jnp.arange(0, DIFF_HD, 2, dtype=F32) / DIFF_HD)
    inv = jnp.tile(inv_freq, LANES // (DIFF_HD // 2)).reshape(1, LANES)
    pos = positions.reshape(n, 1)
    out = jax.ShapeDtypeStruct((n, LANES), F32)
    return pl.pallas_call(
        _rope_kernel,
        grid=(n // tm,),
        in_specs=[pl.BlockSpec((tm, 1), lambda i: (i, 0)), _const_spec((1, LANES))],
        out_specs=[pl.BlockSpec((tm, LANES), lambda i: (i, 0))] * 3,
        out_shape=[out] * 3,
        compiler_params=_params(1),
        name="rope_tables",
    )(pos, inv)


def _gdn_in_kernel(x_ref, xh_ref, w_ref, wab_ref, cw_ref, alog_ref, dtb_ref,
                   qkv_ref, gate_ref, gb_ref, xb_ref, *, seq_tiles, n_qk, n_v, n_gate):
    _stage_rows(x_ref, xh_ref, xb_ref, pl.program_id(0), seq_tiles)
    xb = xb_ref[...]
    xt = xb[SUBLANES:, :]
    base = (n_qk + n_v) * MXU_N
    conv_per_gate = (n_qk + n_v) // n_gate
    jobs = []
    for c in range(n_qk + n_v):
        jobs.append(("conv", c))
        if c % conv_per_gate == conv_per_gate - 1:
            jobs.append(("gate", c // conv_per_gate))

    def project(job):
        kind, c = job
        if kind == "conv":
            return _dot(xb, w_ref[:, c * MXU_N:(c + 1) * MXU_N])
        return _dot(xt, w_ref[:, base + c * MXU_N:base + (c + 1) * MXU_N])

    def finish(job, h):
        kind, c = job
        cols = slice(c * MXU_N, (c + 1) * MXU_N)
        if kind == "gate":
            gate_ref[:, cols] = h.astype(BF16)
            return
        y = _silu(_causal_conv(h, cw_ref[:, cols], GDN_CONV))
        if c < n_qk:
            parts = []
            for j in range(MXU_N // GDN_DK):
                yh = y[:, j * GDN_DK:(j + 1) * GDN_DK]
                ss = jnp.sum(yh * yh, axis=-1, keepdims=True)
                parts.append(yh * lax.rsqrt(ss + RMS_EPS))
            y = jnp.concatenate(parts, axis=1)
        qkv_ref[:, cols] = y.astype(BF16)

    for job in jobs:
        finish(job, project(job))
    hab = _dot(xt, wab_ref[...])
    z = hab + dtb_ref[...]
    softplus = jnp.maximum(z, 0.0) + jnp.log(1.0 + jnp.exp(-jnp.abs(z)))
    g = -jnp.exp(alog_ref[...]) * softplus
    lane = lax.broadcasted_iota(jnp.int32, hab.shape, 1)
    gb_ref[...] = jnp.where(lane < GDN_HEADS, g, _sigmoid(hab))


def _gdn_in(x2, w_in, conv_w, a_log, dt_bias, tm, seq_tiles):
    n, d = x2.shape
    qk_w = 2 * GDN_HEADS * GDN_DK
    v_w = GDN_HEADS * GDN_DV
    main = qk_w + 2 * v_w
    w_main = w_in[:, :main].astype(BF16)
    w_ab = jnp.pad(w_in[:, main:], ((0, 0), (0, LANES - 2 * GDN_HEADS))).astype(BF16)
    pad = (0, LANES - GDN_HEADS)
    alog = jnp.pad(a_log.astype(F32), pad).reshape(1, LANES)
    dtb = jnp.pad(dt_bias.astype(F32), pad).reshape(1, LANES)
    kern = functools.partial(_gdn_in_kernel, seq_tiles=seq_tiles, n_qk=qk_w // MXU_N,
                             n_v=v_w // MXU_N, n_gate=v_w // MXU_N)
    row = lambda w: pl.BlockSpec((tm, w), lambda i: (i, 0))
    return pl.pallas_call(
        kern,
        grid=(n // tm,),
        in_specs=[row(d), _halo_spec(tm, d),
                  _resident_spec((d, main)),
                  _const_spec((d, LANES)),
                  _const_spec((GDN_CONV, qk_w + v_w)), _const_spec((1, LANES)),
                  _const_spec((1, LANES))],
        out_specs=[row(qk_w + v_w), row(v_w), row(LANES)],
        out_shape=[jax.ShapeDtypeStruct((n, qk_w + v_w), BF16),
                   jax.ShapeDtypeStruct((n, v_w), BF16),
                   jax.ShapeDtypeStruct((n, LANES), F32)],
        scratch_shapes=[pltpu.VMEM((SUBLANES + tm, d), BF16)],
        compiler_params=_params(1),
        name="gdn_in",
    )(x2, x2, w_main, w_ab, conv_w.astype(F32), alog, dtb)


def _unit_lower_inverses(a_list, eye, diag_blk, merge_masks, bdmask):
    c = GDN_CHUNK

    def bd(y):
        yy = jnp.concatenate([y, y], axis=0)
        return jnp.where(bdmask, yy, jnp.zeros_like(yy))

    n = [jnp.where(diag_blk, -a, 0.0) for a in a_list]
    p = [eye + x for x in n]
    nb = [x.astype(BF16) for x in n]
    n = [_dot(x, bd(x)) for x in nb]
    squarings = GDN_INV_BLOCK.bit_length() - 1
    for _ in range(squarings - 2):
        nb = [x.astype(BF16) for x in n]
        r = [_dot(jnp.concatenate([pi.astype(BF16), ni], axis=0), bd(ni)) for pi, ni in zip(p, nb)]
        p = [pi + ri[:c] for pi, ri in zip(p, r)]
        n = [ri[c:] for ri in r]
    p = [pi + _dot(pi.astype(BF16), bd(ni.astype(BF16))) for pi, ni in zip(p, n)]
    for off in merge_masks:
        pb = [pi.astype(BF16) for pi in p]
        t = [_dot(jnp.where(off, a, 0.0).astype(BF16), bd(pbi)) for a, pbi in zip(a_list, pb)]
        p = [pi - _dot(pbi, bd(ti.astype(BF16))) for pi, pbi, ti in zip(p, pb, t)]
    return p


def _gdn_chunk_kernel(q_ref, k_ref, v_ref, gate_ref, gb_ref, nw_ref,
                      o_ref, s_ref, wq_ref, u_ref, ak_ref, *, n_sub):
    c = GDN_CHUNK

    @pl.when(pl.program_id(1) == 0)
    def _():
        s_ref[...] = jnp.zeros_like(s_ref)

    step = n_sub * c
    gb = gb_ref[0]
    in_chunk = lax.broadcasted_iota(jnp.int32, (step, LANES), 0) % c
    gc_cols = gb
    shift = 1
    while shift < c:
        gc_cols = gc_cols + jnp.where(in_chunk >= shift, pltpu.roll(gc_cols, shift, 0), 0.0)
        shift *= 2
    gc_rows = gc_cols.T[:GDN_HEADS, :]
    egc_cols = jnp.exp(gc_cols)
    bl = GDN_HEADS

    c2 = 2 * c
    r = lax.broadcasted_iota(jnp.int32, (c, c2), 0)
    lane = lax.broadcasted_iota(jnp.int32, (c, c2), 1)
    s = lane % c
    first = lane < c
    causal = s <= r
    strict = s < r
    eye = jnp.where(r == s, 1.0, 0.0)
    diag_blk = (r // GDN_INV_BLOCK) == (s // GDN_INV_BLOCK)
    merge_masks = []
    size = GDN_INV_BLOCK
    while size < c:
        merge_masks.append(((r // (2 * size)) == (s // (2 * size))) & ((r // size) != (s // size)))
        size *= 2
    bdmask = (lax.broadcasted_iota(jnp.int32, (c2, c2), 0) // c
              == lax.broadcasted_iota(jnp.int32, (c2, c2), 1) // c)
    kmask = (lax.broadcasted_iota(jnp.int32, (c2, 2 * GDN_DK), 0) // c
             == lax.broadcasted_iota(jnp.int32, (c2, 2 * GDN_DK), 1) // GDN_DK)
    scale = GDN_DK ** -0.5
    heads = range(GDN_HEADS)
    hcols = [slice(h * GDN_DK, (h + 1) * GDN_DK) for h in heads]
    n_hp = GDN_HEADS // 2
    pcols = [slice(hp * 2 * GDN_DK, (hp + 1) * 2 * GDN_DK) for hp in range(n_hp)]

    def pack_cols(x, rw, lane0):
        return jnp.where(first, jnp.broadcast_to(x[rw, lane0:lane0 + 1], (c, c2)),
                         jnp.broadcast_to(x[rw, lane0 + 1:lane0 + 2], (c, c2)))

    units = [(sub, hp) for sub in range(n_sub) for hp in range(n_hp)]
    rws = [slice(sub * c, (sub + 1) * c) for sub, _ in units]
    q2 = [q_ref[0, rw, pcols[hp]] for rw, (_, hp) in zip(rws, units)]
    k2 = [k_ref[0, rw, pcols[hp]] for rw, (_, hp) in zip(rws, units)]
    kbd = []
    for k in k2:
        kk = jnp.concatenate([k, k], axis=0)
        kbd.append(jnp.where(kmask, kk, jnp.zeros_like(kk)))
    qkk = [_dot_nt(jnp.concatenate([q, k], axis=0), kb) for q, k, kb in zip(q2, k2, kbd)]
    decay = []
    for rw, (_, hp) in zip(rws, units):
        gcr = jnp.concatenate([gc_rows[2 * hp:2 * hp + 1, rw], gc_rows[2 * hp + 1:2 * hp + 2, rw]], axis=1)
        diff = jnp.where(causal, pack_cols(gc_cols, rw, 2 * hp) - gcr, 0.0)
        decay.append(jnp.where(causal, jnp.exp(diff), 0.0))
    a_list = [jnp.where(strict, x[c:] * dc * pack_cols(gb, rw, bl + 2 * hp), 0.0)
              for x, dc, rw, (_, hp) in zip(qkk, decay, rws, units)]
    attn = [(x[:c] * scale * dc).astype(BF16) for x, dc in zip(qkk, decay)]
    inv = _unit_lower_inverses(a_list, eye, diag_blk, merge_masks, bdmask)
    zeros = jnp.zeros((c, GDN_DV + GDN_DK), BF16)
    rhs, kfs = [], []
    for rw, (_, hp) in zip(rws, units):
        both = []
        for h in (2 * hp, 2 * hp + 1):
            bcol = gb[rw, bl + h:bl + h + 1]
            kf = k_ref[0, rw, hcols[h]].astype(F32)
            kfs.append(kf)
            both.append(jnp.concatenate([v_ref[0, rw, hcols[h]].astype(F32) * bcol,
                                         kf * (bcol * egc_cols[rw, h:h + 1])], axis=1).astype(BF16))
        rhs.append(jnp.concatenate([jnp.concatenate([both[0], zeros], axis=1),
                                    jnp.concatenate([zeros, both[1]], axis=1)], axis=0))
    sol = [_dot(t.astype(BF16), x) for t, x in zip(inv, rhs)]
    for i, (rw, (sub, hp)) in enumerate(zip(rws, units)):
        kds = []
        for j, h in enumerate((2 * hp, 2 * hp + 1)):
            base = j * (GDN_DV + GDN_DK)
            g_last = gc_cols[(sub + 1) * c - 1:(sub + 1) * c, h:h + 1]
            u_ref[sub, h] = sol[i][:, base:base + GDN_DV]
            qd = (q_ref[0, rw, hcols[h]].astype(F32) * (egc_cols[rw, h:h + 1] * scale)).astype(BF16)
            wq_ref[sub, h] = jnp.concatenate(
                [sol[i][:, base + GDN_DV:base + GDN_DV + GDN_DK].astype(BF16), qd], axis=0)
            kds.append(kfs[2 * i + j] * jnp.exp(g_last - gc_cols[rw, h:h + 1]))
        kdt = jnp.concatenate(kds, axis=0).T
        ak_ref[sub, hp] = jnp.concatenate([attn[i], kdt.astype(BF16)], axis=0)

    nw = nw_ref[...]
    zv = jnp.zeros((c, GDN_DV), BF16)
    for sub in range(n_sub):
        rows = slice(sub * c, (sub + 1) * c)
        g_last = gc_cols[(sub + 1) * c - 1:(sub + 1) * c, :]
        st = [s_ref[h] for h in heads]
        r1 = [_dot(wq_ref[sub, h], st[h].astype(BF16)) for h in heads]
        vb = [(u_ref[sub, h] - r1[h][:c]).astype(BF16) for h in heads]
        r2p = []
        for hp in range(n_hp):
            w2 = jnp.concatenate([jnp.concatenate([vb[2 * hp], zv], axis=1),
                                  jnp.concatenate([zv, vb[2 * hp + 1]], axis=1)], axis=0)
            r2p.append(_dot(ak_ref[sub, hp], w2))
        r2 = [r2p[h // 2][:, (h % 2) * GDN_DV:(h % 2 + 1) * GDN_DV] for h in heads]
        for h in heads:
            s_ref[h] = st[h] * jnp.exp(g_last[:, h:h + 1]) + r2[h][c:]
            o = r1[h][c:] + r2[h][:c]
            ms = jnp.mean(o * o, axis=-1, keepdims=True)
            on = o * lax.rsqrt(ms + RMS_EPS) * nw
            gt = gate_ref[0, rows, hcols[h]].astype(F32)
            o_ref[0, rows, hcols[h]] = (on * _silu(gt)).astype(BF16)


def _gdn_chunk(qkv, gate, gb, norm_w, b, t):
    hq = GDN_HEADS * GDN_DK
    hv = GDN_HEADS * GDN_DV
    step = GDN_STEP if t % GDN_STEP == 0 else t
    qkv3 = qkv.reshape(b, t, 2 * hq + hv)
    gate3 = gate.reshape(b, t, hv)
    gb3 = gb.reshape(b, t, LANES)
    n_sub = step // GDN_CHUNK
    kern = functools.partial(_gdn_chunk_kernel, n_sub=n_sub)
    col = lambda j: pl.BlockSpec((1, step, hq), lambda bi, ci: (bi, ci, j))
    out = pl.pallas_call(
        kern,
        grid=(b, t // step),
        in_specs=[col(0), col(1), col(2), col(0),
                  pl.BlockSpec((1, step, LANES), lambda bi, ci: (bi, ci, 0)),
                  _const_spec((1, GDN_DV))],
        out_specs=col(0),
        out_shape=jax.ShapeDtypeStruct((b, t, hv), BF16),
        scratch_shapes=[pltpu.VMEM((GDN_HEADS, GDN_DK, GDN_DV), F32),
                        pltpu.VMEM((n_sub, GDN_HEADS, 2 * GDN_CHUNK, GDN_DK), BF16),
                        pltpu.VMEM((n_sub, GDN_HEADS, GDN_CHUNK, GDN_DV), F32),
                        pltpu.VMEM((n_sub, GDN_HEADS // 2, GDN_CHUNK + GDN_DK, 2 * GDN_CHUNK), BF16)],
        compiler_params=_params(2),
        name="gdn_chunk",
    )(qkv3, qkv3, qkv3, gate3, gb3, norm_w.astype(F32).reshape(1, GDN_DV))
    return out.reshape(b * t, hv)


def _proj_ln_kernel(y_ref, w_ref, x_ref, g_ref, b_ref, o_ref):
    m = _dot(y_ref[...], w_ref[...])
    z = DEEPNORM_ALPHA * x_ref[...] + m
    o_ref[...] = _layer_norm_rows(z, g_ref[...], b_ref[...])


def _proj_ln(y, w, x2, g, b, tm):
    n, d = x2.shape
    k = y.shape[1]
    row = lambda w_: pl.BlockSpec((tm, w_), lambda i: (i, 0))
    return pl.pallas_call(
        _proj_ln_kernel,
        grid=(n // tm,),
        in_specs=[row(k), _resident_spec((k, d)), row(d), _const_spec((1, d)), _const_spec((1, d))],
        out_specs=row(d),
        out_shape=jax.ShapeDtypeStruct((n, d), F32),
        compiler_params=_params(1),
        name="proj_ln",
    )(y, w.astype(BF16), x2, g.astype(F32).reshape(1, d), b.astype(F32).reshape(1, d))


def _ffn_kernel(x_ref, xh_ref, wup_ref, cw_ref, cb_ref, wdn_ref, g_ref, b_ref, o_ref,
                xb_ref, act_ref, *, seq_tiles, hidden):
    _stage_rows(x_ref, xh_ref, xb_ref, pl.program_id(0), seq_tiles)
    xb = xb_ref[...]
    n_chunks = hidden // MXU_N

    def project(c):
        return [_dot(xb, wup_ref[:, base + c * MXU_N:base + (c + 1) * MXU_N]) for base in (0, hidden)]

    def finish(c, hs):
        halves = []
        for base, h in zip((0, hidden), hs):
            cols = slice(base + c * MXU_N, base + (c + 1) * MXU_N)
            halves.append(_causal_conv(h, cw_ref[:, cols], FFN_CONV) + cb_ref[:, cols])
        act_ref[:, c * MXU_N:(c + 1) * MXU_N] = (_silu(halves[0]) * halves[1]).astype(BF16)

    for c in range(n_chunks):
        finish(c, project(c))
    f = _dot(act_ref[...], wdn_ref[...])
    z = DEEPNORM_ALPHA * x_ref[...] + f
    o_ref[...] = _layer_norm_rows(z, g_ref[...], b_ref[...])


def _ffn(x2, w_up, conv_w, conv_b, w_down, g, b, tm, seq_tiles):
    n, d = x2.shape
    hidden = w_down.shape[0]
    kern = functools.partial(_ffn_kernel, seq_tiles=seq_tiles, hidden=hidden)
    row = pl.BlockSpec((tm, d), lambda i: (i, 0))
    return pl.pallas_call(
        kern,
        grid=(n // tm,),
        in_specs=[row, _halo_spec(tm, d), _resident_spec((d, 2 * hidden)),
                  _const_spec((FFN_CONV, 2 * hidden)), _const_spec((1, 2 * hidden)),
                  _resident_spec((hidden, d)), _const_spec((1, d)), _const_spec((1, d))],
        out_specs=row,
        out_shape=jax.ShapeDtypeStruct((n, d), F32),
        scratch_shapes=[pltpu.VMEM((SUBLANES + tm, d), BF16), pltpu.VMEM((tm, hidden), BF16)],
        compiler_params=_params(1),
        name="conv_ffn",
    )(x2, x2, w_up.astype(BF16), conv_w.astype(F32), conv_b.astype(F32).reshape(1, 2 * hidden),
      w_down.astype(BF16), g.astype(F32).reshape(1, d), b.astype(F32).reshape(1, d))


def _diff_in_kernel(x_ref, w_ref, c_ref, s1_ref, s2_ref, qk_ref, v_ref, *, n_qk, n_v, q_cols):
    xb = x_ref[...].astype(BF16)
    cos = c_ref[...]
    s1 = s1_ref[...]
    s2 = s2_ref[...]
    half = DIFF_HD // 2
    for c in range(n_qk):
        h = _dot(xb, w_ref[:, c * MXU_N:(c + 1) * MXU_N])
        for j in range(MXU_N // LANES):
            hs = h[:, j * LANES:(j + 1) * LANES]
            rot = hs * cos + pltpu.roll(hs, LANES - half, 1) * s1 + pltpu.roll(hs, half, 1) * s2
            lo = c * MXU_N + j * LANES
            if lo < q_cols:
                rot = rot * (DIFF_HD ** -0.5 * math.log2(math.e))
            qk_ref[:, lo:lo + LANES] = rot.astype(BF16)
    base = n_qk * MXU_N
    for c in range(n_v):
        v_ref[:, c * MXU_N:(c + 1) * MXU_N] = _dot(
            xb, w_ref[:, base + c * MXU_N:base + (c + 1) * MXU_N]).astype(BF16)


def _diff_in(x2, w_in, tables, tm):
    n, d = x2.shape
    hw = DIFF_HEADS * 2 * DIFF_HD
    kern = functools.partial(_diff_in_kernel, n_qk=2 * hw // MXU_N, n_v=hw // MXU_N, q_cols=hw)
    row = lambda w: pl.BlockSpec((tm, w), lambda i: (i, 0))
    return pl.pallas_call(
        kern,
        grid=(n // tm,),
        in_specs=[row(d), _resident_spec((d, 3 * hw)), row(LANES), row(LANES), row(LANES)],
        out_specs=[row(2 * hw), row(hw)],
        out_shape=[jax.ShapeDtypeStruct((n, 2 * hw), BF16), jax.ShapeDtypeStruct((n, hw), BF16)],
        compiler_params=_params(1),
        name="diff_in",
    )(x2, w_in.astype(BF16), *tables)


def _diff_attn_kernel(q_ref, k_ref, v_ref, lq1_ref, lk1_ref, lq2_ref, lk2_ref, sw_ref, o_ref,
                      *, t, blk, lambda_init):
    lam = (jnp.exp(jnp.sum(lq1_ref[...] * lk1_ref[...], axis=-1, keepdims=True))
           - jnp.exp(jnp.sum(lq2_ref[...] * lk2_ref[...], axis=-1, keepdims=True))
           + lambda_init)
    lane = lax.broadcasted_iota(jnp.int32, (blk, 2 * DIFF_HD), 1)
    first_map = lane < DIFF_HD
    r = lax.broadcasted_iota(jnp.int32, (2 * blk, blk), 0)
    s = lax.broadcasted_iota(jnp.int32, (2 * blk, blk), 1)
    diag_ok = s <= (r % blk)
    sw = sw_ref[...]

    def scores(qi):
        q = q_ref[0, qi * blk:(qi + 1) * blk, :]
        zero = jnp.zeros_like(q)
        qs = jnp.concatenate([jnp.where(first_map, q, zero), jnp.where(first_map, zero, q)], axis=0)
        d0 = qi * blk
        sc_d = jnp.where(diag_ok, _dot_nt(qs, k_ref[0, d0:d0 + blk, :]), -jnp.inf)
        sc_f = _dot_nt(qs, k_ref[0, 0:d0, :]) if qi > 0 else None
        return sc_d, sc_f

    def softmax(qi, sc_d, sc_f):
        m = jnp.max(sc_d, axis=-1, keepdims=True)
        if qi > 0:
            m = jnp.maximum(m, jnp.max(sc_f, axis=-1, keepdims=True))
        p_d = jnp.exp2(sc_d - m)
        l = jnp.sum(p_d, axis=-1, keepdims=True)
        p_f = None
        if qi > 0:
            p_f = jnp.exp2(sc_f - m)
            l = l + jnp.sum(p_f, axis=-1, keepdims=True)
            p_f = p_f.astype(BF16)
        return p_d.astype(BF16), p_f, l

    def weighted_values(qi, p_d, p_f, l):
        d0 = qi * blk
        acc = _dot(p_d, v_ref[0, d0:d0 + blk, :])
        if qi > 0:
            acc = acc + _dot(p_f, v_ref[0, 0:d0, :])
        on = acc / l
        o = on[:blk] - lam * on[blk:]
        ms = jnp.mean(o * o, axis=-1, keepdims=True)
        o = o * lax.rsqrt(ms + RMS_EPS) * sw * (1.0 - lambda_init)
        o_ref[0, qi * blk:(qi + 1) * blk, :] = o.astype(BF16)

    n_q = t // blk
    sc = {}
    for step in range(n_q + ATTN_LEAD):
        if step < n_q:
            sc[step] = scores(step)
        qi = step - ATTN_LEAD
        if qi >= 0:
            weighted_values(qi, *softmax(qi, *sc.pop(qi)))


def _diff_attn(qk, v, lam_q1, lam_k1, lam_q2, lam_k2, subln_w, lambda_init, b, t):
    hw = DIFF_HEADS * 2 * DIFF_HD
    hd2 = 2 * DIFF_HD
    blk = ATTN_BLOCK if t % ATTN_BLOCK == 0 else t
    qk3 = qk.reshape(b, t, 2 * hw)
    v3 = v.reshape(b, t, hw)
    kern = functools.partial(_diff_attn_kernel, t=t, blk=blk, lambda_init=lambda_init)
    head = lambda off: pl.BlockSpec((1, t, hd2), lambda bi, hi: (bi, 0, hi + off))
    vec = lambda a: a.astype(F32).reshape(1, -1)
    out = pl.pallas_call(
        kern,
        grid=(b, DIFF_HEADS),
        in_specs=[head(0), head(DIFF_HEADS), head(0)] + [_const_spec((1, DIFF_HD))] * 4
                 + [_const_spec((1, hd2))],
        out_specs=head(0),
        out_shape=jax.ShapeDtypeStruct((b, t, hw), BF16),
        compiler_params=_params(2),
        name="diff_attn",
    )(qk3, qk3, v3, vec(lam_q1), vec(lam_k1), vec(lam_q2), vec(lam_k2), vec(subln_w))
    return out.reshape(b * t, hw)


def kernel(x, positions, gdn_w_in, gdn_conv_w, gdn_a_log, gdn_dt_bias, gdn_norm_w, gdn_w_out,
           diff_w_in, diff_lam_q1, diff_lam_k1, diff_lam_q2, diff_lam_k2, diff_subln_w, diff_w_out,
           ffn_w_up, ffn_conv_w, ffn_conv_b, ffn_w_down, ln_mix_g, ln_mix_b, ln_ffn_g, ln_ffn_b):
    b, t, d = x.shape
    tm = _row_tile(t)
    seq_tiles = t // tm
    x2 = x.reshape(b * t, d).astype(F32)
    tables = _rope_tables(positions, tm)
    for i in range(DEPTH):
        j = i // 2
        if i % 2 == 0:
            qkv, gate, gb = _gdn_in(x2, gdn_w_in[j], gdn_conv_w[j], gdn_a_log[j], gdn_dt_bias[j],
                                    tm, seq_tiles)
            mixed = _gdn_chunk(qkv, gate, gb, gdn_norm_w[j], b, t)
            w_out = gdn_w_out[j]
        else:
            lambda_init = 0.8 - 0.6 * math.exp(-0.3 * i)
            qk, v = _diff_in(x2, diff_w_in[j], tables, tm)
            mixed = _diff_attn(qk, v, diff_lam_q1[j], diff_lam_k1[j], diff_lam_q2[j], diff_lam_k2[j],
                               diff_subln_w[j], lambda_init, b, t)
            w_out = diff_w_out[j]
        x2 = _proj_ln(mixed, w_out, x2, ln_mix_g[i], ln_mix_b[i], tm)
        x2 = _ffn(x2, ffn_w_up[i], ffn_conv_w[i], ffn_conv_b[i], ffn_w_down[i],
                  ln_ffn_g[i], ln_ffn_b[i], tm, seq_tiles)
    return x2.reshape(b, t, d)
```

```python
import functools
import math

import jax
import jax.numpy as jnp
from jax import lax
from jax.experimental import pallas as pl
from jax.experimental.pallas import tpu as pltpu

F32 = jnp.float32
BF16 = jnp.bfloat16

DEPTH = 4
GDN_HEADS = 8
GDN_DK = 128
GDN_DV = 128
GDN_CONV = 4
GDN_CHUNK = 64
DIFF_HEADS = 8
DIFF_HD = 64
ROPE_THETA = 10000.0
FFN_CONV = 3
DEEPNORM_ALPHA = (2.0 * DEPTH) ** 0.25
LN_EPS = 1e-5
RMS_EPS = 1e-6

LANES = 128
SUBLANES = 8
MXU_N = 256
VMEM_LIMIT = 56 * 1024 * 1024

ROW_TILE = 1024
ATTN_LEAD = 1
ATTN_BLOCK = 256
GDN_STEP = 512
GDN_INV_BLOCK = 16


def _row_tile(t):
    return ROW_TILE if t % ROW_TILE == 0 else t


def _sigmoid(x):
    return 1.0 / (1.0 + jnp.exp2(x * -math.log2(math.e)))


def _silu(x):
    return x * _sigmoid(x)


def _dot(a, b):
    return jnp.dot(a, b, preferred_element_type=F32)


def _dot_nt(a, b):
    return lax.dot_general(a, b, (((1,), (1,)), ((), ())), preferred_element_type=F32)


def _shift_rows(h, s):
    return pltpu.roll(h, s, 0)


def _causal_conv(h, cw, width):
    y = h * cw[width - 1:width, :]
    for s in range(1, width):
        y = y + _shift_rows(h, s) * cw[width - 1 - s:width - s, :]
    return y[SUBLANES:, :]


def _layer_norm_rows(z, g, b):
    mu = jnp.mean(z, axis=-1, keepdims=True)
    zc = z - mu
    var = jnp.mean(zc * zc, axis=-1, keepdims=True)
    return zc * lax.rsqrt(var + LN_EPS) * g + b


def _stage_rows(x_ref, xh_ref, xb_ref, tile, seq_tiles):
    first = (tile % seq_tiles) == 0
    halo = jnp.where(first, 0.0, xh_ref[...])
    xb_ref[0:SUBLANES, :] = halo.astype(BF16)
    xb_ref[SUBLANES:, :] = x_ref[...].astype(BF16)


def _halo_spec(tm, d):
    return pl.BlockSpec((SUBLANES, d), lambda i: (jnp.maximum(i * (tm // SUBLANES) - 1, 0), 0))


def _const_spec(shape):
    return pl.BlockSpec(shape, lambda *_: (0,) * len(shape))


def _resident_spec(shape):
    return pl.BlockSpec(shape, lambda *_: (0,) * len(shape), pipeline_mode=pl.Buffered(1))


def _params(n_axes):
    return pltpu.CompilerParams(
        dimension_semantics=("arbitrary",) * n_axes, vmem_limit_bytes=VMEM_LIMIT)


def _rope_kernel(pos_ref, inv_ref, c_ref, s1_ref, s2_ref):
    tm = pos_ref.shape[0]
    nf = DIFF_HD // 2
    groups = LANES // nf
    blk = tm // groups
    lane = lax.broadcasted_iota(jnp.int32, (blk, LANES), 1)
    pos = pos_ref[...].astype(F32)
    packed = jnp.zeros((blk, LANES), F32)
    for j in range(groups):
        packed = jnp.where(lane // nf == j, jnp.broadcast_to(pos[j * blk:(j + 1) * blk, :], (blk, LANES)),
                           packed)
    ang = packed * inv_ref[...]
    cos = jnp.cos(ang)
    sin = jnp.sin(ang)
    lower = (lane % DIFF_HD) < nf
    for j in range(groups):
        cj, sj = cos, sin
        for g in range(groups):
            shift = ((g - j) % groups) * nf
            if shift:
                cj = jnp.where(lane // nf == g, pltpu.roll(cos, shift, 1), cj)
                sj = jnp.where(lane // nf == g, pltpu.roll(sin, shift, 1), sj)
        rows = slice(j * blk, (j + 1) * blk)
        c_ref[rows, :] = cj
        s1_ref[rows, :] = jnp.where(lower, -sj, 0.0)
        s2_ref[rows, :] = jnp.where(lower, 0.0, sj)


def _rope_tables(positions, tm):
    n = positions.size
    inv_freq = ROPE_THETA ** (-jnp.arange(0, DIFF_HD, 2, dtype=F32) / DIFF_HD)
    inv = jnp.tile(inv_freq, LANES // (DIFF_HD // 2)).reshape(1, LANES)
    pos = positions.reshape(n, 1)
    out = jax.ShapeDtypeStruct((n, LANES), F32)
    return pl.pallas_call(
        _rope_kernel,
        grid=(n // tm,),
        in_specs=[pl.BlockSpec((tm, 1), lambda i: (i, 0)), _const_spec((1, LANES))],
        out_specs=[pl.BlockSpec((tm, LANES), lambda i: (i, 0))] * 3,
        out_shape=[out] * 3,
        compiler_params=_params(1),
        name="rope_tables",
    )(pos, inv)


def _gdn_in_kernel(x_ref, xh_ref, w_ref, wab_ref, cw_ref, alog_ref, dtb_ref,
                   qkv_ref, gate_ref, gb_ref, xb_ref, *, seq_tiles, n_qk, n_v, n_gate):
    _stage_rows(x_ref, xh_ref, xb_ref, pl.program_id(0), seq_tiles)
    xb = xb_ref[...]
    xt = xb[SUBLANES:, :]
    base = (n_qk + n_v) * MXU_N
    conv_per_gate = (n_qk + n_v) // n_gate
    jobs = []
    for c in range(n_qk + n_v):
        jobs.append(("conv", c))
        if c % conv_per_gate == conv_per_gate - 1:
            jobs.append(("gate", c // conv_per_gate))

    def project(job):
        kind, c = job
        if kind == "conv":
            return _dot(xb, w_ref[:, c * MXU_N:(c + 1) * MXU_N])
        return _dot(xt, w_ref[:, base + c * MXU_N:base + (c + 1) * MXU_N])

    def finish(job, h):
        kind, c = job
        cols = slice(c * MXU_N, (c + 1) * MXU_N)
        if kind == "gate":
            gate_ref[:, cols] = h.astype(BF16)
            return
        y = _silu(_causal_conv(h, cw_ref[:, cols], GDN_CONV))
        if c < n_qk:
            parts = []
            for j in range(MXU_N // GDN_DK):
                yh = y[:, j * GDN_DK:(j + 1) * GDN_DK]
                ss = jnp.sum(yh * yh, axis=-1, keepdims=True)
                parts.append(yh * lax.rsqrt(ss + RMS_EPS))
            y = jnp.concatenate(parts, axis=1)
        qkv_ref[:, cols] = y.astype(BF16)

    for job in jobs:
        finish(job, project(job))
    hab = _dot(xt, wab_ref[...])
    z = hab + dtb_ref[...]
    softplus = jnp.maximum(z, 0.0) + jnp.log(1.0 + jnp.exp(-jnp.abs(z)))
    g = -jnp.exp(alog_ref[...]) * softplus
    lane = lax.broadcasted_iota(jnp.int32, hab.shape, 1)
    gb_ref[...] = jnp.where(lane < GDN_HEADS, g, _sigmoid(hab))


def _gdn_in(x2, w_in, conv_w, a_log, dt_bias, tm, seq_tiles):
    n, d = x2.shape
    qk_w = 2 * GDN_HEADS * GDN_DK
    v_w = GDN_HEADS * GDN_DV
    main = qk_w + 2 * v_w
    w_main = w_in[:, :main].astype(BF16)
    w_ab = jnp.pad(w_in[:, main:], ((0, 0), (0, LANES - 2 * GDN_HEADS))).astype(BF16)
    pad = (0, LANES - GDN_HEADS)
    alog = jnp.pad(a_log.astype(F32), pad).reshape(1, LANES)
    dtb = jnp.pad(dt_bias.astype(F32), pad).reshape(1, LANES)
    kern = functools.partial(_gdn_in_kernel, seq_tiles=seq_tiles, n_qk=qk_w // MXU_N,
                             n_v=v_w // MXU_N, n_gate=v_w // MXU_N)
    row = lambda w: pl.BlockSpec((tm, w), lambda i: (i, 0))
    return pl.pallas_call(
        kern,
        grid=(n // tm,),
        in_specs=[row(d), _halo_spec(tm, d),
                  _resident_spec((d, main)),
                  _const_spec((d, LANES)),
                  _const_spec((GDN_CONV, qk_w + v_w)), _const_spec((1, LANES)),
                  _const_spec((1, LANES))],
        out_specs=[row(qk_w + v_w), row(v_w), row(LANES)],
        out_shape=[jax.ShapeDtypeStruct((n, qk_w + v_w), BF16),
                   jax.ShapeDtypeStruct((n, v_w), BF16),
                   jax.ShapeDtypeStruct((n, LANES), F32)],
        scratch_shapes=[pltpu.VMEM((SUBLANES + tm, d), BF16)],
        compiler_params=_params(1),
        name="gdn_in",
    )(x2, x2, w_main, w_ab, conv_w.astype(F32), alog, dtb)


def _unit_lower_inverses(a_list, eye, diag_blk, merge_masks, bdmask):
    c = GDN_CHUNK

    def bd(y):
        yy = jnp.concatenate([y, y], axis=0)
        return jnp.where(bdmask, yy, jnp.zeros_like(yy))

    n = [jnp.where(diag_blk, -a, 0.0) for a in a_list]
    p = [eye + x for x in n]
    nb = [x.astype(BF16) for x in n]
    n = [_dot(x, bd(x)) for x in nb]
    squarings = GDN_INV_BLOCK.bit_length() - 1
    for _ in range(squarings - 2):
        nb = [x.astype(BF16) for x in n]
        r = [_dot(jnp.concatenate([pi.astype(BF16), ni], axis=0), bd(ni)) for pi, ni in zip(p, nb)]
        p = [pi + ri[:c] for pi, ri in zip(p, r)]
        n = [ri[c:] for ri in r]
    p = [pi + _dot(pi.astype(BF16), bd(ni.astype(BF16))) for pi, ni in zip(p, n)]
    for off in merge_masks:
        pb = [pi.astype(BF16) for pi in p]
        t = [_dot(jnp.where(off, a, 0.0).astype(BF16), bd(pbi)) for a, pbi in zip(a_list, pb)]
        p = [pi - _dot(pbi, bd(ti.astype(BF16))) for pi, pbi, ti in zip(p, pb, t)]
    return p


def _gdn_chunk_kernel(q_ref, k_ref, v_ref, gate_ref, gb_ref, nw_ref,
                      o_ref, s_ref, wq_ref, u_ref, ak_ref, *, n_sub):
    c = GDN_CHUNK

    @pl.when(pl.program_id(1) == 0)
    def _():
        s_ref[...] = jnp.zeros_like(s_ref)

    step = n_sub * c
    gb = gb_ref[0]
    in_chunk = lax.broadcasted_iota(jnp.int32, (step, LANES), 0) % c
    gc_cols = gb
    shift = 1
    while shift < c:
        gc_cols = gc_cols + jnp.where(in_chunk >= shift, pltpu.roll(gc_cols, shift, 0), 0.0)
        shift *= 2
    gc_rows = gc_cols.T[:GDN_HEADS, :]
    egc_cols = jnp.exp(gc_cols)
    bl = GDN_HEADS

    c2 = 2 * c
    r = lax.broadcasted_iota(jnp.int32, (c, c2), 0)
    lane = lax.broadcasted_iota(jnp.int32, (c, c2), 1)
    s = lane % c
    first = lane < c
    causal = s <= r
    strict = s < r
    eye = jnp.where(r == s, 1.0, 0.0)
    diag_blk = (r // GDN_INV_BLOCK) == (s // GDN_INV_BLOCK)
    merge_masks = []
    size = GDN_INV_BLOCK
    while size < c:
        merge_masks.append(((r // (2 * size)) == (s // (2 * size))) & ((r // size) != (s // size)))
        size *= 2
    bdmask = (lax.broadcasted_iota(jnp.int32, (c2, c2), 0) // c
              == lax.broadcasted_iota(jnp.int32, (c2, c2), 1) // c)
    kmask = (lax.broadcasted_iota(jnp.int32, (c2, 2 * GDN_DK), 0) // c
             == lax.broadcasted_iota(jnp.int32, (c2, 2 * GDN_DK), 1) // GDN_DK)
    scale = GDN_DK ** -0.5
    heads = range(GDN_HEADS)
    hcols = [slice(h * GDN_DK, (h + 1) * GDN_DK) for h in heads]
    n_hp = GDN_HEADS // 2
    pcols = [slice(hp * 2 * GDN_DK, (hp + 1) * 2 * GDN_DK) for hp in range(n_hp)]

    def pack_cols(x, rw, lane0):
        return jnp.where(first, jnp.broadcast_to(x[rw, lane0:lane0 + 1], (c, c2)),
                         jnp.broadcast_to(x[rw, lane0 + 1:lane0 + 2], (c, c2)))

    units = [(sub, hp) for sub in range(n_sub) for hp in range(n_hp)]
    rws = [slice(sub * c, (sub + 1) * c) for sub, _ in units]
    q2 = [q_ref[0, rw, pcols[hp]] for rw, (_, hp) in zip(rws, units)]
    k2 = [k_ref[0, rw, pcols[hp]] for rw, (_, hp) in zip(rws, units)]
    kbd = []
    for k in k2:
        kk = jnp.concatenate([k, k], axis=0)
        kbd.append(jnp.where(kmask, kk, jnp.zeros_like(kk)))
    qkk = [_dot_nt(jnp.concatenate([q, k], axis=0), kb) for q, k, kb in zip(q2, k2, kbd)]
    decay = []
    for rw, (_, hp) in zip(rws, units):
        gcr = jnp.concatenate([gc_rows[2 * hp:2 * hp + 1, rw], gc_rows[2 * hp + 1:2 * hp + 2, rw]], axis=1)
        diff = jnp.where(causal, pack_cols(gc_cols, rw, 2 * hp) - gcr, 0.0)
        decay.append(jnp.where(causal, jnp.exp(diff), 0.0))
    a_list = [jnp.where(strict, x[c:] * dc * pack_cols(gb, rw, bl + 2 * hp), 0.0)
              for x, dc, rw, (_, hp) in zip(qkk, decay, rws, units)]
    attn = [(x[:c] * scale * dc).astype(BF16) for x, dc in zip(qkk, decay)]
    inv = _unit_lower_inverses(a_list, eye, diag_blk, merge_masks, bdmask)
    zeros = jnp.zeros((c, GDN_DV + GDN_DK), BF16)
    rhs, kfs = [], []
    for rw, (_, hp) in zip(rws, units):
        both = []
        for h in (2 * hp, 2 * hp + 1):
            bcol = gb[rw, bl + h:bl + h + 1]
            kf = k_ref[0, rw, hcols[h]].astype(F32)
            kfs.append(kf)
            both.append(jnp.concatenate([v_ref[0, rw, hcols[h]].astype(F32) * bcol,
                                         kf * (bcol * egc_cols[rw, h:h + 1])], axis=1).astype(BF16))
        rhs.append(jnp.concatenate([jnp.concatenate([both[0], zeros], axis=1),
                                    jnp.concatenate([zeros, both[1]], axis=1)], axis=0))
    sol = [_dot(t.astype(BF16), x) for t, x in zip(inv, rhs)]
    for i, (rw, (sub, hp)) in enumerate(zip(rws, units)):
        kds = []
        for j, h in enumerate((2 * hp, 2 * hp + 1)):
            base = j * (GDN_DV + GDN_DK)
            g_last = gc_cols[(sub + 1) * c - 1:(sub + 1) * c, h:h + 1]
            u_ref[sub, h] = sol[i][:, base:base + GDN_DV]
            qd = (q_ref[0, rw, hcols[h]].astype(F32) * (egc_cols[rw, h:h + 1] * scale)).astype(BF16)
            wq_ref[sub, h] = jnp.concatenate(
                [sol[i][:, base + GDN_DV:base + GDN_DV + GDN_DK].astype(BF16), qd], axis=0)
            kds.append(kfs[2 * i + j] * jnp.exp(g_last - gc_cols[rw, h:h + 1]))
        kdt = jnp.concatenate(kds, axis=0).T
        ak_ref[sub, hp] = jnp.concatenate([attn[i], kdt.astype(BF16)], axis=0)

    nw = nw_ref[...]
    zv = jnp.zeros((c, GDN_DV), BF16)
    for sub in range(n_sub):
        rows = slice(sub * c, (sub + 1) * c)
        g_last = gc_cols[(sub + 1) * c - 1:(sub + 1) * c, :]
        st = [s_ref[h] for h in heads]
        r1 = [_dot(wq_ref[sub, h], st[h].astype(BF16)) for h in heads]
        vb = [(u_ref[sub, h] - r1[h][:c]).astype(BF16) for h in heads]
        r2p = []
        for hp in range(n_hp):
            w2 = jnp.concatenate([jnp.concatenate([vb[2 * hp], zv], axis=1),
                                  jnp.concatenate([zv, vb[2 * hp + 1]], axis=1)], axis=0)
            r2p.append(_dot(ak_ref[sub, hp], w2))
        r2 = [r2p[h // 2][:, (h % 2) * GDN_DV:(h % 2 + 1) * GDN_DV] for h in heads]
        for h in heads:
            s_ref[h] = st[h] * jnp.exp(g_last[:, h:h + 1]) + r2[h][c:]
            o = r1[h][c:] + r2[h][:c]
            ms = jnp.mean(o * o, axis=-1, keepdims=True)
            on = o * lax.rsqrt(ms + RMS_EPS) * nw
            gt = gate_ref[0, rows, hcols[h]].astype(F32)
            o_ref[0, rows, hcols[h]] = (on * _silu(gt)).astype(BF16)


def _gdn_chunk(qkv, gate, gb, norm_w, b, t):
    hq = GDN_HEADS * GDN_DK
    hv = GDN_HEADS * GDN_DV
    step = GDN_STEP if t % GDN_STEP == 0 else t
    qkv3 = qkv.reshape(b, t, 2 * hq + hv)
    gate3 = gate.reshape(b, t, hv)
    gb3 = gb.reshape(b, t, LANES)
    n_sub = step // GDN_CHUNK
    kern = functools.partial(_gdn_chunk_kernel, n_sub=n_sub)
    col = lambda j: pl.BlockSpec((1, step, hq), lambda bi, ci: (bi, ci, j))
    out = pl.pallas_call(
        kern,
        grid=(b, t // step),
        in_specs=[col(0), col(1), col(2), col(0),
                  pl.BlockSpec((1, step, LANES), lambda bi, ci: (bi, ci, 0)),
                  _const_spec((1, GDN_DV))],
        out_specs=col(0),
        out_shape=jax.ShapeDtypeStruct((b, t, hv), BF16),
        scratch_shapes=[pltpu.VMEM((GDN_HEADS, GDN_DK, GDN_DV), F32),
                        pltpu.VMEM((n_sub, GDN_HEADS, 2 * GDN_CHUNK, GDN_DK), BF16),
                        pltpu.VMEM((n_sub, GDN_HEADS, GDN_CHUNK, GDN_DV), F32),
                        pltpu.VMEM((n_sub, GDN_HEADS // 2, GDN_CHUNK + GDN_DK, 2 * GDN_CHUNK), BF16)],
        compiler_params=_params(2),
        name="gdn_chunk",
    )(qkv3, qkv3, qkv3, gate3, gb3, norm_w.astype(F32).reshape(1, GDN_DV))
    return out.reshape(b * t, hv)


def _proj_ln_kernel(y_ref, w_ref, x_ref, g_ref, b_ref, o_ref):
    m = _dot(y_ref[...], w_ref[...])
    z = DEEPNORM_ALPHA * x_ref[...] + m
    o_ref[...] = _layer_norm_rows(z, g_ref[...], b_ref[...])


def _proj_ln(y, w, x2, g, b, tm):
    n, d = x2.shape
    k = y.shape[1]
    row = lambda w_: pl.BlockSpec((tm, w_), lambda i: (i, 0))
    return pl.pallas_call(
        _proj_ln_kernel,
        grid=(n // tm,),
        in_specs=[row(k), _resident_spec((k, d)), row(d), _const_spec((1, d)), _const_spec((1, d))],
        out_specs=row(d),
        out_shape=jax.ShapeDtypeStruct((n, d), F32),
        compiler_params=_params(1),
        name="proj_ln",
    )(y, w.astype(BF16), x2, g.astype(F32).reshape(1, d), b.astype(F32).reshape(1, d))


def _ffn_kernel(x_ref, xh_ref, wup_ref, cw_ref, cb_ref, wdn_ref, g_ref, b_ref, o_ref,
                xb_ref, act_ref, *, seq_tiles, hidden):
    _stage_rows(x_ref, xh_ref, xb_ref, pl.program_id(0), seq_tiles)
    xb = xb_ref[...]
    n_chunks = hidden // MXU_N

    def project(c):
        return [_dot(xb, wup_ref[:, base + c * MXU_N:base + (c + 1) * MXU_N]) for base in (0, hidden)]

    def finish(c, hs):
        halves = []
        for base, h in zip((0, hidden), hs):
            cols = slice(base + c * MXU_N, base + (c + 1) * MXU_N)
            halves.append(_causal_conv(h, cw_ref[:, cols], FFN_CONV) + cb_ref[:, cols])
        act_ref[:, c * MXU_N:(c + 1) * MXU_N] = (_silu(halves[0]) * halves[1]).astype(BF16)

    for c in range(n_chunks):
        finish(c, project(c))
    f = _dot(act_ref[...], wdn_ref[...])
    z = DEEPNORM_ALPHA * x_ref[...] + f
    o_ref[...] = _layer_norm_rows(z, g_ref[...], b_ref[...])


def _ffn(x2, w_up, conv_w, conv_b, w_down, g, b, tm, seq_tiles):
    n, d = x2.shape
    hidden = w_down.shape[0]
    kern = functools.partial(_ffn_kernel, seq_tiles=seq_tiles, hidden=hidden)
    row = pl.BlockSpec((tm, d), lambda i: (i, 0))
    return pl.pallas_call(
        kern,
        grid=(n // tm,),
        in_specs=[row, _halo_spec(tm, d), _resident_spec((d, 2 * hidden)),
                  _const_spec((FFN_CONV, 2 * hidden)), _const_spec((1, 2 * hidden)),
                  _resident_spec((hidden, d)), _const_spec((1, d)), _const_spec((1, d))],
        out_specs=row,
        out_shape=jax.ShapeDtypeStruct((n, d), F32),
        scratch_shapes=[pltpu.VMEM((SUBLANES + tm, d), BF16), pltpu.VMEM((tm, hidden), BF16)],
        compiler_params=_params(1),
        name="conv_ffn",
    )(x2, x2, w_up.astype(BF16), conv_w.astype(F32), conv_b.astype(F32).reshape(1, 2 * hidden),
      w_down.astype(BF16), g.astype(F32).reshape(1, d), b.astype(F32).reshape(1, d))


def _diff_in_kernel(x_ref, w_ref, c_ref, s1_ref, s2_ref, qk_ref, v_ref, *, n_qk, n_v, q_cols):
    xb = x_ref[...].astype(BF16)
    cos = c_ref[...]
    s1 = s1_ref[...]
    s2 = s2_ref[...]
    half = DIFF_HD // 2
    for c in range(n_qk):
        h = _dot(xb, w_ref[:, c * MXU_N:(c + 1) * MXU_N])
        for j in range(MXU_N // LANES):
            hs = h[:, j * LANES:(j + 1) * LANES]
            rot = hs * cos + pltpu.roll(hs, LANES - half, 1) * s1 + pltpu.roll(hs, half, 1) * s2
            lo = c * MXU_N + j * LANES
            if lo < q_cols:
                rot = rot * (DIFF_HD ** -0.5 * math.log2(math.e))
            qk_ref[:, lo:lo + LANES] = rot.astype(BF16)
    base = n_qk * MXU_N
    for c in range(n_v):
        v_ref[:, c * MXU_N:(c + 1) * MXU_N] = _dot(
            xb, w_ref[:, base + c * MXU_N:base + (c + 1) * MXU_N]).astype(BF16)


def _diff_in(x2, w_in, tables, tm):
    n, d = x2.shape
    hw = DIFF_HEADS * 2 * DIFF_HD
    kern = functools.partial(_diff_in_kernel, n_qk=2 * hw // MXU_N, n_v=hw // MXU_N, q_cols=hw)
    row = lambda w: pl.BlockSpec((tm, w), lambda i: (i, 0))
    return pl.pallas_call(
        kern,
        grid=(n // tm,),
        in_specs=[row(d), _resident_spec((d, 3 * hw)), row(LANES), row(LANES), row(LANES)],
        out_specs=[row(2 * hw), row(hw)],
        out_shape=[jax.ShapeDtypeStruct((n, 2 * hw), BF16), jax.ShapeDtypeStruct((n, hw), BF16)],
        compiler_params=_params(1),
        name="diff_in",
    )(x2, w_in.astype(BF16), *tables)


def _diff_attn_kernel(q_ref, k_ref, v_ref, lq1_ref, lk1_ref, lq2_ref, lk2_ref, sw_ref, o_ref,
                      *, t, blk, lambda_init):
    lam = (jnp.exp(jnp.sum(lq1_ref[...] * lk1_ref[...], axis=-1, keepdims=True))
           - jnp.exp(jnp.sum(lq2_ref[...] * lk2_ref[...], axis=-1, keepdims=True))
           + lambda_init)
    lane = lax.broadcasted_iota(jnp.int32, (blk, 2 * DIFF_HD), 1)
    first_map = lane < DIFF_HD
    r = lax.broadcasted_iota(jnp.int32, (2 * blk, blk), 0)
    s = lax.broadcasted_iota(jnp.int32, (2 * blk, blk), 1)
    diag_ok = s <= (r % blk)
    sw = sw_ref[...]

    def scores(qi):
        q = q_ref[0, qi * blk:(qi + 1) * blk, :]
        zero = jnp.zeros_like(q)
        qs = jnp.concatenate([jnp.where(first_map, q, zero), jnp.where(first_map, zero, q)], axis=0)
        d0 = qi * blk
        sc_d = jnp.where(diag_ok, _dot_nt(qs, k_ref[0, d0:d0 + blk, :]), -jnp.inf)
        sc_f = _dot_nt(qs, k_ref[0, 0:d0, :]) if qi > 0 else None
        return sc_d, sc_f

    def softmax(qi, sc_d, sc_f):
        m = jnp.max(sc_d, axis=-1, keepdims=True)
        if qi > 0:
            m = jnp.maximum(m, jnp.max(sc_f, axis=-1, keepdims=True))
        p_d = jnp.exp2(sc_d - m)
        l = jnp.sum(p_d, axis=-1, keepdims=True)
        p_f = None
        if qi > 0:
            p_f = jnp.exp2(sc_f - m)
            l = l + jnp.sum(p_f, axis=-1, keepdims=True)
            p_f = p_f.astype(BF16)
        return p_d.astype(BF16), p_f, l

    def weighted_values(qi, p_d, p_f, l):
        d0 = qi * blk
        acc = _dot(p_d, v_ref[0, d0:d0 + blk, :])
        if qi > 0:
            acc = acc + _dot(p_f, v_ref[0, 0:d0, :])
        on = acc / l
        o = on[:blk] - lam * on[blk:]
        ms = jnp.mean(o * o, axis=-1, keepdims=True)
        o = o * lax.rsqrt(ms + RMS_EPS) * sw * (1.0 - lambda_init)
        o_ref[0, qi * blk:(qi + 1) * blk, :] = o.astype(BF16)

    n_q = t // blk
    order = list(range(n_q - 1, -1, -1))
    sc = {}
    for step in range(n_q + ATTN_LEAD):
        if step < n_q:
            sc[order[step]] = scores(order[step])
        if step >= ATTN_LEAD:
            qi = order[step - ATTN_LEAD]
            weighted_values(qi, *softmax(qi, *sc.pop(qi)))


def _diff_attn(qk, v, lam_q1, lam_k1, lam_q2, lam_k2, subln_w, lambda_init, b, t):
    hw = DIFF_HEADS * 2 * DIFF_HD
    hd2 = 2 * DIFF_HD
    blk = ATTN_BLOCK if t % ATTN_BLOCK == 0 else t
    qk3 = qk.reshape(b, t, 2 * hw)
    v3 = v.reshape(b, t, hw)
    kern = functools.partial(_diff_attn_kernel, t=t, blk=blk, lambda_init=lambda_init)
    head = lambda off: pl.BlockSpec((1, t, hd2), lambda bi, hi: (bi, 0, hi + off))
    vec = lambda a: a.astype(F32).reshape(1, -1)
    out = pl.pallas_call(
        kern,
        grid=(b, DIFF_HEADS),
        in_specs=[head(0), head(DIFF_HEADS), head(0)] + [_const_spec((1, DIFF_HD))] * 4
                 + [_const_spec((1, hd2))],
        out_specs=head(0),
        out_shape=jax.ShapeDtypeStruct((b, t, hw), BF16),
        compiler_params=_params(2),
        name="diff_attn",
    )(qk3, qk3, v3, vec(lam_q1), vec(lam_k1), vec(lam_q2), vec(lam_k2), vec(subln_w))
    return out.reshape(b * t, hw)


def kernel(x, positions, gdn_w_in, gdn_conv_w, gdn_a_log, gdn_dt_bias, gdn_norm_w, gdn_w_out,
           diff_w_in, diff_lam_q1, diff_lam_k1, diff_lam_q2, diff_lam_k2, diff_subln_w, diff_w_out,
           ffn_w_up, ffn_conv_w, ffn_conv_b, ffn_w_down, ln_mix_g, ln_mix_b, ln_ffn_g, ln_ffn_b):
    b, t, d = x.shape
    tm = _row_tile(t)
    seq_tiles = t // tm
    x2 = x.reshape(b * t, d).astype(F32)
    tables = _rope_tables(positions, tm)
    for i in range(DEPTH):
        j = i // 2
        if i % 2 == 0:
            qkv, gate, gb = _gdn_in(x2, gdn_w_in[j], gdn_conv_w[j], gdn_a_log[j], gdn_dt_bias[j],
                                    tm, seq_tiles)
            mixed = _gdn_chunk(qkv, gate, gb, gdn_norm_w[j], b, t)
            w_out = gdn_w_out[j]
        else:
            lambda_init = 0.8 - 0.6 * math.exp(-0.3 * i)
            qk, v = _diff_in(x2, diff_w_in[j], tables, tm)
            mixed = _diff_attn(qk, v, diff_lam_q1[j], diff_lam_k1[j], diff_lam_q2[j], diff_lam_k2[j],
                               diff_subln_w[j], lambda_init, b, t)
            w_out = diff_w_out[j]
        x2 = _proj_ln(mixed, w_out, x2, ln_mix_g[i], ln_mix_b[i], tm)
        x2 = _ffn(x2, ffn_w_up[i], ffn_conv_w[i], ffn_conv_b[i], ffn_w_down[i],
                  ln_ffn_g[i], ln_ffn_b[i], tm, seq_tiles)
    return x2.reshape(b, t, d)
```

```python
import functools
import math

import jax
import jax.numpy as jnp
from jax import lax
from jax.experimental import pallas as pl
from jax.experimental.pallas import tpu as pltpu

F32 = jnp.float32
BF16 = jnp.bfloat16

DEPTH = 4
GDN_HEADS = 8
GDN_DK = 128
GDN_DV = 128
GDN_CONV = 4
GDN_CHUNK = 64
DIFF_HEADS = 8
DIFF_HD = 64
ROPE_THETA = 10000.0
FFN_CONV = 3
DEEPNORM_ALPHA = (2.0 * DEPTH) ** 0.25
LN_EPS = 1e-5
RMS_EPS = 1e-6

LANES = 128
SUBLANES = 8
MXU_N = 256
VMEM_LIMIT = 56 * 1024 * 1024

ROW_TILE = 1024
ATTN_LEAD = 1
ATTN_BLOCK = 256
GDN_STEP = 512
GDN_INV_BLOCK = 16


def _row_tile(t):
    return ROW_TILE if t % ROW_TILE == 0 else t


def _sigmoid(x):
    return 1.0 / (1.0 + jnp.exp2(x * -math.log2(math.e)))


def _silu(x):
    return x * _sigmoid(x)


def _dot(a, b):
    return jnp.dot(a, b, preferred_element_type=F32)


def _dot_nt(a, b):
    return lax.dot_general(a, b, (((1,), (1,)), ((), ())), preferred_element_type=F32)


def _shift_rows(h, s):
    return pltpu.roll(h, s, 0)


def _causal_conv(h, cw, width):
    y = h * cw[width - 1:width, :]
    for s in range(1, width):
        y = y + _shift_rows(h, s) * cw[width - 1 - s:width - s, :]
    return y[SUBLANES:, :]


def _layer_norm_rows(z, g, b):
    mu = jnp.mean(z, axis=-1, keepdims=True)
    zc = z - mu
    var = jnp.mean(zc * zc, axis=-1, keepdims=True)
    return zc * lax.rsqrt(var + LN_EPS) * g + b


def _stage_rows(x_ref, xh_ref, xb_ref, tile, seq_tiles):
    first = (tile % seq_tiles) == 0
    halo = jnp.where(first, 0.0, xh_ref[...])
    xb_ref[0:SUBLANES, :] = halo.astype(BF16)
    xb_ref[SUBLANES:, :] = x_ref[...].astype(BF16)


def _halo_spec(tm, d):
    return pl.BlockSpec((SUBLANES, d), lambda i: (jnp.maximum(i * (tm // SUBLANES) - 1, 0), 0))


def _const_spec(shape):
    return pl.BlockSpec(shape, lambda *_: (0,) * len(shape))


def _resident_spec(shape):
    return pl.BlockSpec(shape, lambda *_: (0,) * len(shape), pipeline_mode=pl.Buffered(1))


def _params(n_axes):
    return pltpu.CompilerParams(
        dimension_semantics=("arbitrary",) * n_axes, vmem_limit_bytes=VMEM_LIMIT)


def _rope_kernel(pos_ref, inv_ref, c_ref, s1_ref, s2_ref):
    tm = pos_ref.shape[0]
    nf = DIFF_HD // 2
    groups = LANES // nf
    blk = tm // groups
    lane = lax.broadcasted_iota(jnp.int32, (blk, LANES), 1)
    pos = pos_ref[...].astype(F32)
    packed = jnp.zeros((blk, LANES), F32)
    for j in range(groups):
        packed = jnp.where(lane // nf == j, jnp.broadcast_to(pos[j * blk:(j + 1) * blk, :], (blk, LANES)),
                           packed)
    ang = packed * inv_ref[...]
    cos = jnp.cos(ang)
    sin = jnp.sin(ang)
    lower = (lane % DIFF_HD) < nf
    for j in range(groups):
        cj, sj = cos, sin
        for g in range(groups):
            shift = ((g - j) % groups) * nf
            if shift:
                cj = jnp.where(lane // nf == g, pltpu.roll(cos, shift, 1), cj)
                sj = jnp.where(lane // nf == g, pltpu.roll(sin, shift, 1), sj)
        rows = slice(j * blk, (j + 1) * blk)
        c_ref[rows, :] = cj
        s1_ref[rows, :] = jnp.where(lower, -sj, 0.0)
        s2_ref[rows, :] = jnp.where(lower, 0.0, sj)


def _rope_tables(positions, tm):
    n = positions.size
    inv_freq = ROPE_THETA ** (-jnp.arange(0, DIFF_HD, 2, dtype=F32) / DIFF_HD)
    inv = jnp.tile(inv_freq, LANES // (DIFF_HD // 2)).reshape(1, LANES)
    pos = positions.reshape(n, 1)
    out = jax.ShapeDtypeStruct((n, LANES), F32)
    return pl.pallas_call(
        _rope_kernel,
        grid=(n // tm,),
        in_specs=[pl.BlockSpec((tm, 1), lambda i: (i, 0)), _const_spec((1, LANES))],
        out_specs=[pl.BlockSpec((tm, LANES), lambda i: (i, 0))] * 3,
        out_shape=[out] * 3,
        compiler_params=_params(1),
        name="rope_tables",
    )(pos, inv)


def _gdn_in_kernel(x_ref, xh_ref, w_ref, wab_ref, cw_ref, alog_ref, dtb_ref,
                   qkv_ref, gb_ref, xb_ref, *, seq_tiles, n_qk, n_v):
    _stage_rows(x_ref, xh_ref, xb_ref, pl.program_id(0), seq_tiles)
    xb = xb_ref[...]
    xt = xb[SUBLANES:, :]
    for c in range(n_qk + n_v):
        cols = slice(c * MXU_N, (c + 1) * MXU_N)
        h = _dot(xb, w_ref[:, cols])
        y = _silu(_causal_conv(h, cw_ref[:, cols], GDN_CONV))
        if c < n_qk:
            parts = []
            for j in range(MXU_N // GDN_DK):
                yh = y[:, j * GDN_DK:(j + 1) * GDN_DK]
                ss = jnp.sum(yh * yh, axis=-1, keepdims=True)
                parts.append(yh * lax.rsqrt(ss + RMS_EPS))
            y = jnp.concatenate(parts, axis=1)
        qkv_ref[:, cols] = y.astype(BF16)
    hab = _dot(xt, wab_ref[...])
    z = hab + dtb_ref[...]
    softplus = jnp.maximum(z, 0.0) + jnp.log(1.0 + jnp.exp(-jnp.abs(z)))
    g = -jnp.exp(alog_ref[...]) * softplus
    lane = lax.broadcasted_iota(jnp.int32, hab.shape, 1)
    gb_ref[...] = jnp.where(lane < GDN_HEADS, g, _sigmoid(hab))


def _gdn_in(x2, w_in, conv_w, a_log, dt_bias, tm, seq_tiles):
    n, d = x2.shape
    qk_w = 2 * GDN_HEADS * GDN_DK
    v_w = GDN_HEADS * GDN_DV
    conv_w_ = qk_w + v_w
    w_conv = w_in[:, :conv_w_].astype(BF16)
    w_ab = jnp.pad(w_in[:, conv_w_ + v_w:], ((0, 0), (0, LANES - 2 * GDN_HEADS))).astype(BF16)
    pad = (0, LANES - GDN_HEADS)
    alog = jnp.pad(a_log.astype(F32), pad).reshape(1, LANES)
    dtb = jnp.pad(dt_bias.astype(F32), pad).reshape(1, LANES)
    kern = functools.partial(_gdn_in_kernel, seq_tiles=seq_tiles, n_qk=qk_w // MXU_N,
                             n_v=v_w // MXU_N)
    row = lambda w: pl.BlockSpec((tm, w), lambda i: (i, 0))
    return pl.pallas_call(
        kern,
        grid=(n // tm,),
        in_specs=[row(d), _halo_spec(tm, d),
                  _resident_spec((d, conv_w_)),
                  _const_spec((d, LANES)),
                  _const_spec((GDN_CONV, conv_w_)), _const_spec((1, LANES)),
                  _const_spec((1, LANES))],
        out_specs=[row(conv_w_), row(LANES)],
        out_shape=[jax.ShapeDtypeStruct((n, conv_w_), BF16),
                   jax.ShapeDtypeStruct((n, LANES), F32)],
        scratch_shapes=[pltpu.VMEM((SUBLANES + tm, d), BF16)],
        compiler_params=_params(1),
        name="gdn_in",
    )(x2, x2, w_conv, w_ab, conv_w.astype(F32), alog, dtb)


def _unit_lower_inverses(a_list, eye, diag_blk, merge_masks, bdmask):
    c = GDN_CHUNK

    def bd(y):
        yy = jnp.concatenate([y, y], axis=0)
        return jnp.where(bdmask, yy, jnp.zeros_like(yy))

    n = [jnp.where(diag_blk, -a, 0.0) for a in a_list]
    p = [eye + x for x in n]
    nb = [x.astype(BF16) for x in n]
    n = [_dot(x, bd(x)) for x in nb]
    squarings = GDN_INV_BLOCK.bit_length() - 1
    for _ in range(squarings - 2):
        nb = [x.astype(BF16) for x in n]
        r = [_dot(jnp.concatenate([pi.astype(BF16), ni], axis=0), bd(ni)) for pi, ni in zip(p, nb)]
        p = [pi + ri[:c] for pi, ri in zip(p, r)]
        n = [ri[c:] for ri in r]
    p = [pi + _dot(pi.astype(BF16), bd(ni.astype(BF16))) for pi, ni in zip(p, n)]
    for off in merge_masks:
        pb = [pi.astype(BF16) for pi in p]
        t = [_dot(jnp.where(off, a, 0.0).astype(BF16), bd(pbi)) for a, pbi in zip(a_list, pb)]
        p = [pi - _dot(pbi, bd(ti.astype(BF16))) for pi, pbi, ti in zip(p, pb, t)]
    return p


def _gdn_chunk_kernel(q_ref, k_ref, v_ref, x_ref, wg_ref, gb_ref, nw_ref,
                      o_ref, s_ref, wq_ref, u_ref, ak_ref, gate_ref, *, n_sub):
    c = GDN_CHUNK

    @pl.when(pl.program_id(1) == 0)
    def _():
        s_ref[...] = jnp.zeros_like(s_ref)

    step = n_sub * c
    gb = gb_ref[0]
    in_chunk = lax.broadcasted_iota(jnp.int32, (step, LANES), 0) % c
    gc_cols = gb
    shift = 1
    while shift < c:
        gc_cols = gc_cols + jnp.where(in_chunk >= shift, pltpu.roll(gc_cols, shift, 0), 0.0)
        shift *= 2
    gc_rows = gc_cols.T[:GDN_HEADS, :]
    egc_cols = jnp.exp(gc_cols)
    bl = GDN_HEADS

    c2 = 2 * c
    r = lax.broadcasted_iota(jnp.int32, (c, c2), 0)
    lane = lax.broadcasted_iota(jnp.int32, (c, c2), 1)
    s = lane % c
    first = lane < c
    causal = s <= r
    strict = s < r
    eye = jnp.where(r == s, 1.0, 0.0)
    diag_blk = (r // GDN_INV_BLOCK) == (s // GDN_INV_BLOCK)
    merge_masks = []
    size = GDN_INV_BLOCK
    while size < c:
        merge_masks.append(((r // (2 * size)) == (s // (2 * size))) & ((r // size) != (s // size)))
        size *= 2
    bdmask = (lax.broadcasted_iota(jnp.int32, (c2, c2), 0) // c
              == lax.broadcasted_iota(jnp.int32, (c2, c2), 1) // c)
    kmask = (lax.broadcasted_iota(jnp.int32, (c2, 2 * GDN_DK), 0) // c
             == lax.broadcasted_iota(jnp.int32, (c2, 2 * GDN_DK), 1) // GDN_DK)
    scale = GDN_DK ** -0.5
    heads = range(GDN_HEADS)
    hcols = [slice(h * GDN_DK, (h + 1) * GDN_DK) for h in heads]
    n_hp = GDN_HEADS // 2
    pcols = [slice(hp * 2 * GDN_DK, (hp + 1) * 2 * GDN_DK) for hp in range(n_hp)]

    def pack_cols(x, rw, lane0):
        return jnp.where(first, jnp.broadcast_to(x[rw, lane0:lane0 + 1], (c, c2)),
                         jnp.broadcast_to(x[rw, lane0 + 1:lane0 + 2], (c, c2)))

    units = [(sub, hp) for sub in range(n_sub) for hp in range(n_hp)]
    rws = [slice(sub * c, (sub + 1) * c) for sub, _ in units]
    q2 = [q_ref[0, rw, pcols[hp]] for rw, (_, hp) in zip(rws, units)]
    k2 = [k_ref[0, rw, pcols[hp]] for rw, (_, hp) in zip(rws, units)]
    kbd = []
    for k in k2:
        kk = jnp.concatenate([k, k], axis=0)
        kbd.append(jnp.where(kmask, kk, jnp.zeros_like(kk)))
    qkk = [_dot_nt(jnp.concatenate([q, k], axis=0), kb) for q, k, kb in zip(q2, k2, kbd)]
    gate_ref[...] = _dot(x_ref[0].astype(BF16), wg_ref[...])
    decay = []
    for rw, (_, hp) in zip(rws, units):
        gcr = jnp.concatenate([gc_rows[2 * hp:2 * hp + 1, rw], gc_rows[2 * hp + 1:2 * hp + 2, rw]], axis=1)
        diff = jnp.where(causal, pack_cols(gc_cols, rw, 2 * hp) - gcr, 0.0)
        decay.append(jnp.where(causal, jnp.exp(diff), 0.0))
    a_list = [jnp.where(strict, x[c:] * dc * pack_cols(gb, rw, bl + 2 * hp), 0.0)
              for x, dc, rw, (_, hp) in zip(qkk, decay, rws, units)]
    attn = [(x[:c] * scale * dc).astype(BF16) for x, dc in zip(qkk, decay)]
    inv = _unit_lower_inverses(a_list, eye, diag_blk, merge_masks, bdmask)
    zeros = jnp.zeros((c, GDN_DV + GDN_DK), BF16)
    rhs, kfs = [], []
    for rw, (_, hp) in zip(rws, units):
        both = []
        for h in (2 * hp, 2 * hp + 1):
            bcol = gb[rw, bl + h:bl + h + 1]
            kf = k_ref[0, rw, hcols[h]].astype(F32)
            kfs.append(kf)
            both.append(jnp.concatenate([v_ref[0, rw, hcols[h]].astype(F32) * bcol,
                                         kf * (bcol * egc_cols[rw, h:h + 1])], axis=1).astype(BF16))
        rhs.append(jnp.concatenate([jnp.concatenate([both[0], zeros], axis=1),
                                    jnp.concatenate([zeros, both[1]], axis=1)], axis=0))
    sol = [_dot(t.astype(BF16), x) for t, x in zip(inv, rhs)]
    for i, (rw, (sub, hp)) in enumerate(zip(rws, units)):
        kds = []
        for j, h in enumerate((2 * hp, 2 * hp + 1)):
            base = j * (GDN_DV + GDN_DK)
            g_last = gc_cols[(sub + 1) * c - 1:(sub + 1) * c, h:h + 1]
            u_ref[sub, h] = sol[i][:, base:base + GDN_DV]
            qd = (q_ref[0, rw, hcols[h]].astype(F32) * (egc_cols[rw, h:h + 1] * scale)).astype(BF16)
            wq_ref[sub, h] = jnp.concatenate(
                [sol[i][:, base + GDN_DV:base + GDN_DV + GDN_DK].astype(BF16), qd], axis=0)
            kds.append(kfs[2 * i + j] * jnp.exp(g_last - gc_cols[rw, h:h + 1]))
        kdt = jnp.concatenate(kds, axis=0).T
        ak_ref[sub, hp] = jnp.concatenate([attn[i], kdt.astype(BF16)], axis=0)

    nw = nw_ref[...]
    zv = jnp.zeros((c, GDN_DV), BF16)
    for sub in range(n_sub):
        rows = slice(sub * c, (sub + 1) * c)
        g_last = gc_cols[(sub + 1) * c - 1:(sub + 1) * c, :]
        st = [s_ref[h] for h in heads]
        r1 = [_dot(wq_ref[sub, h], st[h].astype(BF16)) for h in heads]
        vb = [(u_ref[sub, h] - r1[h][:c]).astype(BF16) for h in heads]
        r2p = []
        for hp in range(n_hp):
            w2 = jnp.concatenate([jnp.concatenate([vb[2 * hp], zv], axis=1),
                                  jnp.concatenate([zv, vb[2 * hp + 1]], axis=1)], axis=0)
            r2p.append(_dot(ak_ref[sub, hp], w2))
        r2 = [r2p[h // 2][:, (h % 2) * GDN_DV:(h % 2 + 1) * GDN_DV] for h in heads]
        for h in heads:
            s_ref[h] = st[h] * jnp.exp(g_last[:, h:h + 1]) + r2[h][c:]
            o = r1[h][c:] + r2[h][:c]
            ms = jnp.mean(o * o, axis=-1, keepdims=True)
            on = o * lax.rsqrt(ms + RMS_EPS) * nw
            gt = gate_ref[rows, hcols[h]]
            o_ref[0, rows, hcols[h]] = (on * _silu(gt)).astype(BF16)


def _gdn_chunk(qkv, x2, w_gate, gb, norm_w, b, t):
    hq = GDN_HEADS * GDN_DK
    hv = GDN_HEADS * GDN_DV
    d = x2.shape[1]
    step = GDN_STEP if t % GDN_STEP == 0 else t
    qkv3 = qkv.reshape(b, t, 2 * hq + hv)
    x3 = x2.reshape(b, t, d)
    gb3 = gb.reshape(b, t, LANES)
    n_sub = step // GDN_CHUNK
    kern = functools.partial(_gdn_chunk_kernel, n_sub=n_sub)
    col = lambda j: pl.BlockSpec((1, step, hq), lambda bi, ci: (bi, ci, j))
    out = pl.pallas_call(
        kern,
        grid=(b, t // step),
        in_specs=[col(0), col(1), col(2),
                  pl.BlockSpec((1, step, d), lambda bi, ci: (bi, ci, 0)), _resident_spec((d, hv)),
                  pl.BlockSpec((1, step, LANES), lambda bi, ci: (bi, ci, 0)),
                  _const_spec((1, GDN_DV))],
        out_specs=col(0),
        out_shape=jax.ShapeDtypeStruct((b, t, hv), BF16),
        scratch_shapes=[pltpu.VMEM((GDN_HEADS, GDN_DK, GDN_DV), F32),
                        pltpu.VMEM((n_sub, GDN_HEADS, 2 * GDN_CHUNK, GDN_DK), BF16),
                        pltpu.VMEM((n_sub, GDN_HEADS, GDN_CHUNK, GDN_DV), F32),
                        pltpu.VMEM((n_sub, GDN_HEADS // 2, GDN_CHUNK + GDN_DK, 2 * GDN_CHUNK), BF16),
                        pltpu.VMEM((step, hv), F32)],
        compiler_params=_params(2),
        name="gdn_chunk",
    )(qkv3, qkv3, qkv3, x3, w_gate.astype(BF16), gb3, norm_w.astype(F32).reshape(1, GDN_DV))
    return out.reshape(b * t, hv)


def _proj_ln_kernel(y_ref, w_ref, x_ref, g_ref, b_ref, o_ref):
    m = _dot(y_ref[...], w_ref[...])
    z = DEEPNORM_ALPHA * x_ref[...] + m
    o_ref[...] = _layer_norm_rows(z, g_ref[...], b_ref[...])


def _proj_ln(y, w, x2, g, b, tm):
    n, d = x2.shape
    k = y.shape[1]
    row = lambda w_: pl.BlockSpec((tm, w_), lambda i: (i, 0))
    return pl.pallas_call(
        _proj_ln_kernel,
        grid=(n // tm,),
        in_specs=[row(k), _resident_spec((k, d)), row(d), _const_spec((1, d)), _const_spec((1, d))],
        out_specs=row(d),
        out_shape=jax.ShapeDtypeStruct((n, d), F32),
        compiler_params=_params(1),
        name="proj_ln",
    )(y, w.astype(BF16), x2, g.astype(F32).reshape(1, d), b.astype(F32).reshape(1, d))


def _ffn_kernel(x_ref, xh_ref, wup_ref, cw_ref, cb_ref, wdn_ref, g_ref, b_ref, o_ref,
                xb_ref, act_ref, *, seq_tiles, hidden):
    _stage_rows(x_ref, xh_ref, xb_ref, pl.program_id(0), seq_tiles)
    xb = xb_ref[...]
    n_chunks = hidden // MXU_N

    def project(c):
        return [_dot(xb, wup_ref[:, base + c * MXU_N:base + (c + 1) * MXU_N]) for base in (0, hidden)]

    def finish(c, hs):
        halves = []
        for base, h in zip((0, hidden), hs):
            cols = slice(base + c * MXU_N, base + (c + 1) * MXU_N)
            halves.append(_causal_conv(h, cw_ref[:, cols], FFN_CONV) + cb_ref[:, cols])
        act_ref[:, c * MXU_N:(c + 1) * MXU_N] = (_silu(halves[0]) * halves[1]).astype(BF16)

    for c in range(n_chunks):
        finish(c, project(c))
    f = _dot(act_ref[...], wdn_ref[...])
    z = DEEPNORM_ALPHA * x_ref[...] + f
    o_ref[...] = _layer_norm_rows(z, g_ref[...], b_ref[...])


def _ffn(x2, w_up, conv_w, conv_b, w_down, g, b, tm, seq_tiles):
    n, d = x2.shape
    hidden = w_down.shape[0]
    kern = functools.partial(_ffn_kernel, seq_tiles=seq_tiles, hidden=hidden)
    row = pl.BlockSpec((tm, d), lambda i: (i, 0))
    return pl.pallas_call(
        kern,
        grid=(n // tm,),
        in_specs=[row, _halo_spec(tm, d), _resident_spec((d, 2 * hidden)),
                  _const_spec((FFN_CONV, 2 * hidden)), _const_spec((1, 2 * hidden)),
                  _resident_spec((hidden, d)), _const_spec((1, d)), _const_spec((1, d))],
        out_specs=row,
        out_shape=jax.ShapeDtypeStruct((n, d), F32),
        scratch_shapes=[pltpu.VMEM((SUBLANES + tm, d), BF16), pltpu.VMEM((tm, hidden), BF16)],
        compiler_params=_params(1),
        name="conv_ffn",
    )(x2, x2, w_up.astype(BF16), conv_w.astype(F32), conv_b.astype(F32).reshape(1, 2 * hidden),
      w_down.astype(BF16), g.astype(F32).reshape(1, d), b.astype(F32).reshape(1, d))


def _diff_in_kernel(x_ref, w_ref, c_ref, s1_ref, s2_ref, qk_ref, v_ref, *, n_qk, n_v, q_cols):
    xb = x_ref[...].astype(BF16)
    cos = c_ref[...]
    s1 = s1_ref[...]
    s2 = s2_ref[...]
    half = DIFF_HD // 2
    for c in range(n_qk):
        h = _dot(xb, w_ref[:, c * MXU_N:(c + 1) * MXU_N])
        for j in range(MXU_N // LANES):
            hs = h[:, j * LANES:(j + 1) * LANES]
            rot = hs * cos + pltpu.roll(hs, LANES - half, 1) * s1 + pltpu.roll(hs, half, 1) * s2
            lo = c * MXU_N + j * LANES
            if lo < q_cols:
                rot = rot * (DIFF_HD ** -0.5 * math.log2(math.e))
            qk_ref[:, lo:lo + LANES] = rot.astype(BF16)
    base = n_qk * MXU_N
    for c in range(n_v):
        v_ref[:, c * MXU_N:(c + 1) * MXU_N] = _dot(
            xb, w_ref[:, base + c * MXU_N:base + (c + 1) * MXU_N]).astype(BF16)


def _diff_in(x2, w_in, tables, tm):
    n, d = x2.shape
    hw = DIFF_HEADS * 2 * DIFF_HD
    kern = functools.partial(_diff_in_kernel, n_qk=2 * hw // MXU_N, n_v=hw // MXU_N, q_cols=hw)
    row = lambda w: pl.BlockSpec((tm, w), lambda i: (i, 0))
    return pl.pallas_call(
        kern,
        grid=(n // tm,),
        in_specs=[row(d), _resident_spec((d, 3 * hw)), row(LANES), row(LANES), row(LANES)],
        out_specs=[row(2 * hw), row(hw)],
        out_shape=[jax.ShapeDtypeStruct((n, 2 * hw), BF16), jax.ShapeDtypeStruct((n, hw), BF16)],
        compiler_params=_params(1),
        name="diff_in",
    )(x2, w_in.astype(BF16), *tables)


def _diff_attn_kernel(q_ref, k_ref, v_ref, lq1_ref, lk1_ref, lq2_ref, lk2_ref, sw_ref, o_ref,
                      *, t, blk, lambda_init):
    lam = (jnp.exp(jnp.sum(lq1_ref[...] * lk1_ref[...], axis=-1, keepdims=True))
           - jnp.exp(jnp.sum(lq2_ref[...] * lk2_ref[...], axis=-1, keepdims=True))
           + lambda_init)
    lane = lax.broadcasted_iota(jnp.int32, (blk, 2 * DIFF_HD), 1)
    first_map = lane < DIFF_HD
    r = lax.broadcasted_iota(jnp.int32, (2 * blk, blk), 0)
    s = lax.broadcasted_iota(jnp.int32, (2 * blk, blk), 1)
    diag_ok = s <= (r % blk)
    sw = sw_ref[...]

    def scores(qi):
        q = q_ref[0, qi * blk:(qi + 1) * blk, :]
        zero = jnp.zeros_like(q)
        qs = jnp.concatenate([jnp.where(first_map, q, zero), jnp.where(first_map, zero, q)], axis=0)
        d0 = qi * blk
        sc_d = jnp.where(diag_ok, _dot_nt(qs, k_ref[0, d0:d0 + blk, :]), -jnp.inf)
        sc_f = _dot_nt(qs, k_ref[0, 0:d0, :]) if qi > 0 else None
        return sc_d, sc_f

    def softmax(qi, sc_d, sc_f):
        m = jnp.max(sc_d, axis=-1, keepdims=True)
        if qi > 0:
            m = jnp.maximum(m, jnp.max(sc_f, axis=-1, keepdims=True))
        p_d = jnp.exp2(sc_d - m)
        l = jnp.sum(p_d, axis=-1, keepdims=True)
        p_f = None
        if qi > 0:
            p_f = jnp.exp2(sc_f - m)
            l = l + jnp.sum(p_f, axis=-1, keepdims=True)
            p_f = p_f.astype(BF16)
        return p_d.astype(BF16), p_f, l

    def weighted_values(qi, p_d, p_f, l):
        d0 = qi * blk
        acc = _dot(p_d, v_ref[0, d0:d0 + blk, :])
        if qi > 0:
            acc = acc + _dot(p_f, v_ref[0, 0:d0, :])
        on = acc / l
        o = on[:blk] - lam * on[blk:]
        ms = jnp.mean(o * o, axis=-1, keepdims=True)
        o = o * lax.rsqrt(ms + RMS_EPS) * sw * (1.0 - lambda_init)
        o_ref[0, qi * blk:(qi + 1) * blk, :] = o.astype(BF16)

    n_q = t // blk
    order = list(range(n_q - 1, -1, -1))
    sc = {}
    for step in range(n_q + ATTN_LEAD):
        if step < n_q:
            sc[order[step]] = scores(order[step])
        if step >= ATTN_LEAD:
            qi = order[step - ATTN_LEAD]
            weighted_values(qi, *softmax(qi, *sc.pop(qi)))


def _diff_attn(qk, v, lam_q1, lam_k1, lam_q2, lam_k2, subln_w, lambda_init, b, t):
    hw = DIFF_HEADS * 2 * DIFF_HD
    hd2 = 2 * DIFF_HD
    blk = ATTN_BLOCK if t % ATTN_BLOCK == 0 else t
    qk3 = qk.reshape(b, t, 2 * hw)
    v3 = v.reshape(b, t, hw)
    kern = functools.partial(_diff_attn_kernel, t=t, blk=blk, lambda_init=lambda_init)
    head = lambda off: pl.BlockSpec((1, t, hd2), lambda bi, hi: (bi, 0, hi + off))
    vec = lambda a: a.astype(F32).reshape(1, -1)
    out = pl.pallas_call(
        kern,
        grid=(b, DIFF_HEADS),
        in_specs=[head(0), head(DIFF_HEADS), head(0)] + [_const_spec((1, DIFF_HD))] * 4
                 + [_const_spec((1, hd2))],
        out_specs=head(0),
        out_shape=jax.ShapeDtypeStruct((b, t, hw), BF16),
        compiler_params=_params(2),
        name="diff_attn",
    )(qk3, qk3, v3, vec(lam_q1), vec(lam_k1), vec(lam_q2), vec(lam_k2), vec(subln_w))
    return out.reshape(b * t, hw)


def kernel(x, positions, gdn_w_in, gdn_conv_w, gdn_a_log, gdn_dt_bias, gdn_norm_w, gdn_w_out,
           diff_w_in, diff_lam_q1, diff_lam_k1, diff_lam_q2, diff_lam_k2, diff_subln_w, diff_w_out,
           ffn_w_up, ffn_conv_w, ffn_conv_b, ffn_w_down, ln_mix_g, ln_mix_b, ln_ffn_g, ln_ffn_b):
    b, t, d = x.shape
    tm = _row_tile(t)
    seq_tiles = t // tm
    x2 = x.reshape(b * t, d).astype(F32)
    tables = _rope_tables(positions, tm)
    for i in range(DEPTH):
        j = i // 2
        if i % 2 == 0:
            qkv, gb = _gdn_in(x2, gdn_w_in[j], gdn_conv_w[j], gdn_a_log[j], gdn_dt_bias[j],
                              tm, seq_tiles)
            gate_lo = 2 * GDN_HEADS * GDN_DK + GDN_HEADS * GDN_DV
            w_gate = gdn_w_in[j][:, gate_lo:gate_lo + GDN_HEADS * GDN_DV]
            mixed = _gdn_chunk(qkv, x2, w_gate, gb, gdn_norm_w[j], b, t)
            w_out = gdn_w_out[j]
        else:
            lambda_init = 0.8 - 0.6 * math.exp(-0.3 * i)
            qk, v = _diff_in(x2, diff_w_in[j], tables, tm)
            mixed = _diff_attn(qk, v, diff_lam_q1[j], diff_lam_k1[j], diff_lam_q2[j], diff_lam_k2[j],
                               diff_subln_w[j], lambda_init, b, t)
            w_out = diff_w_out[j]
        x2 = _proj_ln(mixed, w_out, x2, ln_mix_g[i], ln_mix_b[i], tm)
        x2 = _ffn(x2, ffn_w_up[i], ffn_conv_w[i], ffn_conv_b[i], ffn_w_down[i],
                  ln_ffn_g[i], ln_ffn_b[i], tm, seq_tiles)
    return x2.reshape(b, t, d)
```

```python
import functools
import math

import jax
import jax.numpy as jnp
from jax import lax
from jax.experimental import pallas as pl
from jax.experimental.pallas import tpu as pltpu

F32 = jnp.float32
BF16 = jnp.bfloat16

DEPTH = 4
GDN_HEADS = 8
GDN_DK = 128
GDN_DV = 128
GDN_CONV = 4
GDN_CHUNK = 64
DIFF_HEADS = 8
DIFF_HD = 64
ROPE_THETA = 10000.0
FFN_CONV = 3
DEEPNORM_ALPHA = (2.0 * DEPTH) ** 0.25
LN_EPS = 1e-5
RMS_EPS = 1e-6

LANES = 128
SUBLANES = 8
MXU_N = 256
VMEM_LIMIT = 56 * 1024 * 1024

ROW_TILE = 1024
ATTN_LEAD = 1
ATTN_BLOCK = 256
GDN_STEP = 512
GDN_INV_BLOCK = 16


def _row_tile(t):
    return ROW_TILE if t % ROW_TILE == 0 else t


def _sigmoid(x):
    return 1.0 / (1.0 + jnp.exp2(x * -math.log2(math.e)))


def _silu(x):
    return x * _sigmoid(x)


def _dot(a, b):
    return jnp.dot(a, b, preferred_element_type=F32)


def _dot_nt(a, b):
    return lax.dot_general(a, b, (((1,), (1,)), ((), ())), preferred_element_type=F32)


def _shift_rows(h, s):
    return pltpu.roll(h, s, 0)


def _causal_conv(h, cw, width):
    y = h * cw[width - 1:width, :]
    for s in range(1, width):
        y = y + _shift_rows(h, s) * cw[width - 1 - s:width - s, :]
    return y[SUBLANES:, :]


def _layer_norm_rows(z, g, b):
    mu = jnp.mean(z, axis=-1, keepdims=True)
    zc = z - mu
    var = jnp.mean(zc * zc, axis=-1, keepdims=True)
    return zc * lax.rsqrt(var + LN_EPS) * g + b


def _stage_rows(x_ref, xh_ref, xb_ref, tile, seq_tiles):
    first = (tile % seq_tiles) == 0
    halo = jnp.where(first, 0.0, xh_ref[...])
    xb_ref[0:SUBLANES, :] = halo.astype(BF16)
    xb_ref[SUBLANES:, :] = x_ref[...].astype(BF16)


def _halo_spec(tm, d):
    return pl.BlockSpec((SUBLANES, d), lambda i: (jnp.maximum(i * (tm // SUBLANES) - 1, 0), 0))


def _const_spec(shape):
    return pl.BlockSpec(shape, lambda *_: (0,) * len(shape))


def _resident_spec(shape):
    return pl.BlockSpec(shape, lambda *_: (0,) * len(shape), pipeline_mode=pl.Buffered(1))


def _params(n_axes):
    return pltpu.CompilerParams(
        dimension_semantics=("arbitrary",) * n_axes, vmem_limit_bytes=VMEM_LIMIT)


def _rope_kernel(pos_ref, inv_ref, c_ref, s1_ref, s2_ref):
    tm = pos_ref.shape[0]
    nf = DIFF_HD // 2
    groups = LANES // nf
    blk = tm // groups
    lane = lax.broadcasted_iota(jnp.int32, (blk, LANES), 1)
    pos = pos_ref[...].astype(F32)
    packed = jnp.zeros((blk, LANES), F32)
    for j in range(groups):
        packed = jnp.where(lane // nf == j, jnp.broadcast_to(pos[j * blk:(j + 1) * blk, :], (blk, LANES)),
                           packed)
    ang = packed * inv_ref[...]
    cos = jnp.cos(ang)
    sin = jnp.sin(ang)
    lower = (lane % DIFF_HD) < nf
    for j in range(groups):
        cj, sj = cos, sin
        for g in range(groups):
            shift = ((g - j) % groups) * nf
            if shift:
                cj = jnp.where(lane // nf == g, pltpu.roll(cos, shift, 1), cj)
                sj = jnp.where(lane // nf == g, pltpu.roll(sin, shift, 1), sj)
        rows = slice(j * blk, (j + 1) * blk)
        c_ref[rows, :] = cj
        s1_ref[rows, :] = jnp.where(lower, -sj, 0.0)
        s2_ref[rows, :] = jnp.where(lower, 0.0, sj)


def _rope_tables(positions, tm):
    n = positions.size
    inv_freq = ROPE_THETA ** (-jnp.arange(0, DIFF_HD, 2, dtype=F32) / DIFF_HD)
    inv = jnp.tile(inv_freq, LANES // (DIFF_HD // 2)).reshape(1, LANES)
    pos = positions.reshape(n, 1)
    out = jax.ShapeDtypeStruct((n, LANES), F32)
    return pl.pallas_call(
        _rope_kernel,
        grid=(n // tm,),
        in_specs=[pl.BlockSpec((tm, 1), lambda i: (i, 0)), _const_spec((1, LANES))],
        out_specs=[pl.BlockSpec((tm, LANES), lambda i: (i, 0))] * 3,
        out_shape=[out] * 3,
        compiler_params=_params(1),
        name="rope_tables",
    )(pos, inv)


def _gdn_in_kernel(x_ref, xh_ref, w_ref, wab_ref, cw_ref, alog_ref, dtb_ref,
                   qkv_ref, gb_ref, xb_ref, *, seq_tiles, n_qk, n_v):
    _stage_rows(x_ref, xh_ref, xb_ref, pl.program_id(0), seq_tiles)
    xb = xb_ref[...]
    xt = xb[SUBLANES:, :]
    for c in range(n_qk + n_v):
        cols = slice(c * MXU_N, (c + 1) * MXU_N)
        h = _dot(xb, w_ref[:, cols])
        y = _silu(_causal_conv(h, cw_ref[:, cols], GDN_CONV))
        if c < n_qk:
            parts = []
            for j in range(MXU_N // GDN_DK):
                yh = y[:, j * GDN_DK:(j + 1) * GDN_DK]
                ss = jnp.sum(yh * yh, axis=-1, keepdims=True)
                parts.append(yh * lax.rsqrt(ss + RMS_EPS))
            y = jnp.concatenate(parts, axis=1)
        qkv_ref[:, cols] = y.astype(BF16)
    hab = _dot(xt, wab_ref[...])
    z = hab + dtb_ref[...]
    softplus = jnp.maximum(z, 0.0) + jnp.log(1.0 + jnp.exp(-jnp.abs(z)))
    g = -jnp.exp(alog_ref[...]) * softplus
    lane = lax.broadcasted_iota(jnp.int32, hab.shape, 1)
    gb_ref[...] = jnp.where(lane < GDN_HEADS, g, _sigmoid(hab))


def _gdn_in(x2, w_in, conv_w, a_log, dt_bias, tm, seq_tiles):
    n, d = x2.shape
    qk_w = 2 * GDN_HEADS * GDN_DK
    v_w = GDN_HEADS * GDN_DV
    conv_w_ = qk_w + v_w
    w_conv = w_in[:, :conv_w_].astype(BF16)
    w_ab = jnp.pad(w_in[:, conv_w_ + v_w:], ((0, 0), (0, LANES - 2 * GDN_HEADS))).astype(BF16)
    pad = (0, LANES - GDN_HEADS)
    alog = jnp.pad(a_log.astype(F32), pad).reshape(1, LANES)
    dtb = jnp.pad(dt_bias.astype(F32), pad).reshape(1, LANES)
    kern = functools.partial(_gdn_in_kernel, seq_tiles=seq_tiles, n_qk=qk_w // MXU_N,
                             n_v=v_w // MXU_N)
    row = lambda w: pl.BlockSpec((tm, w), lambda i: (i, 0))
    return pl.pallas_call(
        kern,
        grid=(n // tm,),
        in_specs=[row(d), _halo_spec(tm, d),
                  _resident_spec((d, conv_w_)),
                  _const_spec((d, LANES)),
                  _const_spec((GDN_CONV, conv_w_)), _const_spec((1, LANES)),
                  _const_spec((1, LANES))],
        out_specs=[row(conv_w_), row(LANES)],
        out_shape=[jax.ShapeDtypeStruct((n, conv_w_), BF16),
                   jax.ShapeDtypeStruct((n, LANES), F32)],
        scratch_shapes=[pltpu.VMEM((SUBLANES + tm, d), BF16)],
        compiler_params=_params(1),
        name="gdn_in",
    )(x2, x2, w_conv, w_ab, conv_w.astype(F32), alog, dtb)


def _unit_lower_inverses(a_list, eye, diag_blk, merge_masks, bdmask):
    c = GDN_CHUNK

    def bd(y):
        yy = jnp.concatenate([y, y], axis=0)
        return jnp.where(bdmask, yy, jnp.zeros_like(yy))

    n = [jnp.where(diag_blk, -a, 0.0) for a in a_list]
    p = [eye + x for x in n]
    nb = [x.astype(BF16) for x in n]
    n = [_dot(x, bd(x)) for x in nb]
    squarings = GDN_INV_BLOCK.bit_length() - 1
    for _ in range(squarings - 2):
        nb = [x.astype(BF16) for x in n]
        r = [_dot(jnp.concatenate([pi.astype(BF16), ni], axis=0), bd(ni)) for pi, ni in zip(p, nb)]
        p = [pi + ri[:c] for pi, ri in zip(p, r)]
        n = [ri[c:] for ri in r]
    p = [pi + _dot(pi.astype(BF16), bd(ni.astype(BF16))) for pi, ni in zip(p, n)]
    for off in merge_masks:
        pb = [pi.astype(BF16) for pi in p]
        t = [_dot(jnp.where(off, a, 0.0).astype(BF16), bd(pbi)) for a, pbi in zip(a_list, pb)]
        p = [pi - _dot(pbi, bd(ti.astype(BF16))) for pi, pbi, ti in zip(p, pb, t)]
    return p


def _gdn_chunk_kernel(q_ref, k_ref, v_ref, x_ref, wg_ref, gb_ref, nw_ref, wo_ref, lg_ref, lb_ref,
                      o_ref, s_ref, wq_ref, u_ref, ak_ref, gate_ref, mix_ref, *, n_sub):
    c = GDN_CHUNK

    @pl.when(pl.program_id(1) == 0)
    def _():
        s_ref[...] = jnp.zeros_like(s_ref)

    step = n_sub * c
    gb = gb_ref[0]
    in_chunk = lax.broadcasted_iota(jnp.int32, (step, LANES), 0) % c
    gc_cols = gb
    shift = 1
    while shift < c:
        gc_cols = gc_cols + jnp.where(in_chunk >= shift, pltpu.roll(gc_cols, shift, 0), 0.0)
        shift *= 2
    gc_rows = gc_cols.T[:GDN_HEADS, :]
    egc_cols = jnp.exp(gc_cols)
    bl = GDN_HEADS

    c2 = 2 * c
    r = lax.broadcasted_iota(jnp.int32, (c, c2), 0)
    lane = lax.broadcasted_iota(jnp.int32, (c, c2), 1)
    s = lane % c
    first = lane < c
    causal = s <= r
    strict = s < r
    eye = jnp.where(r == s, 1.0, 0.0)
    diag_blk = (r // GDN_INV_BLOCK) == (s // GDN_INV_BLOCK)
    merge_masks = []
    size = GDN_INV_BLOCK
    while size < c:
        merge_masks.append(((r // (2 * size)) == (s // (2 * size))) & ((r // size) != (s // size)))
        size *= 2
    bdmask = (lax.broadcasted_iota(jnp.int32, (c2, c2), 0) // c
              == lax.broadcasted_iota(jnp.int32, (c2, c2), 1) // c)
    kmask = (lax.broadcasted_iota(jnp.int32, (c2, 2 * GDN_DK), 0) // c
             == lax.broadcasted_iota(jnp.int32, (c2, 2 * GDN_DK), 1) // GDN_DK)
    scale = GDN_DK ** -0.5
    heads = range(GDN_HEADS)
    hcols = [slice(h * GDN_DK, (h + 1) * GDN_DK) for h in heads]
    n_hp = GDN_HEADS // 2
    pcols = [slice(hp * 2 * GDN_DK, (hp + 1) * 2 * GDN_DK) for hp in range(n_hp)]

    def pack_cols(x, rw, lane0):
        return jnp.where(first, jnp.broadcast_to(x[rw, lane0:lane0 + 1], (c, c2)),
                         jnp.broadcast_to(x[rw, lane0 + 1:lane0 + 2], (c, c2)))

    units = [(sub, hp) for sub in range(n_sub) for hp in range(n_hp)]
    rws = [slice(sub * c, (sub + 1) * c) for sub, _ in units]
    q2 = [q_ref[0, rw, pcols[hp]] for rw, (_, hp) in zip(rws, units)]
    k2 = [k_ref[0, rw, pcols[hp]] for rw, (_, hp) in zip(rws, units)]
    kbd = []
    for k in k2:
        kk = jnp.concatenate([k, k], axis=0)
        kbd.append(jnp.where(kmask, kk, jnp.zeros_like(kk)))
    qkk = [_dot_nt(jnp.concatenate([q, k], axis=0), kb) for q, k, kb in zip(q2, k2, kbd)]
    gate_ref[...] = _dot(x_ref[0].astype(BF16), wg_ref[...])
    decay = []
    for rw, (_, hp) in zip(rws, units):
        gcr = jnp.concatenate([gc_rows[2 * hp:2 * hp + 1, rw], gc_rows[2 * hp + 1:2 * hp + 2, rw]], axis=1)
        diff = jnp.where(causal, pack_cols(gc_cols, rw, 2 * hp) - gcr, 0.0)
        decay.append(jnp.where(causal, jnp.exp(diff), 0.0))
    a_list = [jnp.where(strict, x[c:] * dc * pack_cols(gb, rw, bl + 2 * hp), 0.0)
              for x, dc, rw, (_, hp) in zip(qkk, decay, rws, units)]
    attn = [(x[:c] * scale * dc).astype(BF16) for x, dc in zip(qkk, decay)]
    inv = _unit_lower_inverses(a_list, eye, diag_blk, merge_masks, bdmask)
    zeros = jnp.zeros((c, GDN_DV + GDN_DK), BF16)
    rhs, kfs = [], []
    for rw, (_, hp) in zip(rws, units):
        both = []
        for h in (2 * hp, 2 * hp + 1):
            bcol = gb[rw, bl + h:bl + h + 1]
            kf = k_ref[0, rw, hcols[h]].astype(F32)
            kfs.append(kf)
            both.append(jnp.concatenate([v_ref[0, rw, hcols[h]].astype(F32) * bcol,
                                         kf * (bcol * egc_cols[rw, h:h + 1])], axis=1).astype(BF16))
        rhs.append(jnp.concatenate([jnp.concatenate([both[0], zeros], axis=1),
                                    jnp.concatenate([zeros, both[1]], axis=1)], axis=0))
    sol = [_dot(t.astype(BF16), x) for t, x in zip(inv, rhs)]
    for i, (rw, (sub, hp)) in enumerate(zip(rws, units)):
        kds = []
        for j, h in enumerate((2 * hp, 2 * hp + 1)):
            base = j * (GDN_DV + GDN_DK)
            g_last = gc_cols[(sub + 1) * c - 1:(sub + 1) * c, h:h + 1]
            u_ref[sub, h] = sol[i][:, base:base + GDN_DV]
            qd = (q_ref[0, rw, hcols[h]].astype(F32) * (egc_cols[rw, h:h + 1] * scale)).astype(BF16)
            wq_ref[sub, h] = jnp.concatenate(
                [sol[i][:, base + GDN_DV:base + GDN_DV + GDN_DK].astype(BF16), qd], axis=0)
            kds.append(kfs[2 * i + j] * jnp.exp(g_last - gc_cols[rw, h:h + 1]))
        kdt = jnp.concatenate(kds, axis=0).T
        ak_ref[sub, hp] = jnp.concatenate([attn[i], kdt.astype(BF16)], axis=0)

    nw = nw_ref[...]
    zv = jnp.zeros((c, GDN_DV), BF16)
    proj_every = n_sub // 2 if n_sub % 2 == 0 else n_sub
    for sub in range(n_sub):
        rows = slice(sub * c, (sub + 1) * c)
        g_last = gc_cols[(sub + 1) * c - 1:(sub + 1) * c, :]
        st = [s_ref[h] for h in heads]
        r1 = [_dot(wq_ref[sub, h], st[h].astype(BF16)) for h in heads]
        vb = [(u_ref[sub, h] - r1[h][:c]).astype(BF16) for h in heads]
        r2p = []
        for hp in range(n_hp):
            w2 = jnp.concatenate([jnp.concatenate([vb[2 * hp], zv], axis=1),
                                  jnp.concatenate([zv, vb[2 * hp + 1]], axis=1)], axis=0)
            r2p.append(_dot(ak_ref[sub, hp], w2))
        r2 = [r2p[h // 2][:, (h % 2) * GDN_DV:(h % 2 + 1) * GDN_DV] for h in heads]
        for h in heads:
            s_ref[h] = st[h] * jnp.exp(g_last[:, h:h + 1]) + r2[h][c:]
            o = r1[h][c:] + r2[h][:c]
            ms = jnp.mean(o * o, axis=-1, keepdims=True)
            on = o * lax.rsqrt(ms + RMS_EPS) * nw
            gt = gate_ref[rows, hcols[h]]
            mix_ref[rows, hcols[h]] = (on * _silu(gt)).astype(BF16)
        if (sub + 1) % proj_every == 0:
            done = slice((sub + 1 - proj_every) * c, (sub + 1) * c)
            z = DEEPNORM_ALPHA * x_ref[0, done, :] + _dot(mix_ref[done, :], wo_ref[...])
            o_ref[0, done, :] = _layer_norm_rows(z, lg_ref[...], lb_ref[...])


def _gdn_chunk(qkv, x2, w_gate, gb, norm_w, w_out, ln_g, ln_b, b, t):
    hq = GDN_HEADS * GDN_DK
    hv = GDN_HEADS * GDN_DV
    d = x2.shape[1]
    step = GDN_STEP if t % GDN_STEP == 0 else t
    qkv3 = qkv.reshape(b, t, 2 * hq + hv)
    x3 = x2.reshape(b, t, d)
    gb3 = gb.reshape(b, t, LANES)
    n_sub = step // GDN_CHUNK
    kern = functools.partial(_gdn_chunk_kernel, n_sub=n_sub)
    col = lambda j: pl.BlockSpec((1, step, hq), lambda bi, ci: (bi, ci, j))
    out = pl.pallas_call(
        kern,
        grid=(b, t // step),
        in_specs=[col(0), col(1), col(2),
                  pl.BlockSpec((1, step, d), lambda bi, ci: (bi, ci, 0)), _resident_spec((d, hv)),
                  pl.BlockSpec((1, step, LANES), lambda bi, ci: (bi, ci, 0)),
                  _const_spec((1, GDN_DV)), _resident_spec((hv, d)), _const_spec((1, d)),
                  _const_spec((1, d))],
        out_specs=pl.BlockSpec((1, step, d), lambda bi, ci: (bi, ci, 0)),
        out_shape=jax.ShapeDtypeStruct((b, t, d), F32),
        scratch_shapes=[pltpu.VMEM((GDN_HEADS, GDN_DK, GDN_DV), F32),
                        pltpu.VMEM((n_sub, GDN_HEADS, 2 * GDN_CHUNK, GDN_DK), BF16),
                        pltpu.VMEM((n_sub, GDN_HEADS, GDN_CHUNK, GDN_DV), F32),
                        pltpu.VMEM((n_sub, GDN_HEADS // 2, GDN_CHUNK + GDN_DK, 2 * GDN_CHUNK), BF16),
                        pltpu.VMEM((step, hv), F32),
                        pltpu.VMEM((step, hv), BF16)],
        compiler_params=_params(2),
        name="gdn_chunk",
    )(qkv3, qkv3, qkv3, x3, w_gate.astype(BF16), gb3, norm_w.astype(F32).reshape(1, GDN_DV),
      w_out.astype(BF16), ln_g.astype(F32).reshape(1, d), ln_b.astype(F32).reshape(1, d))
    return out.reshape(b * t, d)


def _proj_ln_kernel(y_ref, w_ref, x_ref, g_ref, b_ref, o_ref):
    m = _dot(y_ref[...], w_ref[...])
    z = DEEPNORM_ALPHA * x_ref[...] + m
    o_ref[...] = _layer_norm_rows(z, g_ref[...], b_ref[...])


def _proj_ln(y, w, x2, g, b, tm):
    n, d = x2.shape
    k = y.shape[1]
    row = lambda w_: pl.BlockSpec((tm, w_), lambda i: (i, 0))
    return pl.pallas_call(
        _proj_ln_kernel,
        grid=(n // tm,),
        in_specs=[row(k), _resident_spec((k, d)), row(d), _const_spec((1, d)), _const_spec((1, d))],
        out_specs=row(d),
        out_shape=jax.ShapeDtypeStruct((n, d), F32),
        compiler_params=_params(1),
        name="proj_ln",
    )(y, w.astype(BF16), x2, g.astype(F32).reshape(1, d), b.astype(F32).reshape(1, d))


def _ffn_kernel(x_ref, xh_ref, wup_ref, cw_ref, cb_ref, wdn_ref, g_ref, b_ref, o_ref,
                xb_ref, act_ref, *, seq_tiles, hidden):
    _stage_rows(x_ref, xh_ref, xb_ref, pl.program_id(0), seq_tiles)
    xb = xb_ref[...]
    n_chunks = hidden // MXU_N

    def project(c):
        return [_dot(xb, wup_ref[:, base + c * MXU_N:base + (c + 1) * MXU_N]) for base in (0, hidden)]

    def finish(c, hs):
        halves = []
        for base, h in zip((0, hidden), hs):
            cols = slice(base + c * MXU_N, base + (c + 1) * MXU_N)
            halves.append(_causal_conv(h, cw_ref[:, cols], FFN_CONV) + cb_ref[:, cols])
        act_ref[:, c * MXU_N:(c + 1) * MXU_N] = (_silu(halves[0]) * halves[1]).astype(BF16)

    for c in range(n_chunks):
        finish(c, project(c))
    f = _dot(act_ref[...], wdn_ref[...])
    z = DEEPNORM_ALPHA * x_ref[...] + f
    o_ref[...] = _layer_norm_rows(z, g_ref[...], b_ref[...])


def _ffn(x2, w_up, conv_w, conv_b, w_down, g, b, tm, seq_tiles):
    n, d = x2.shape
    hidden = w_down.shape[0]
    kern = functools.partial(_ffn_kernel, seq_tiles=seq_tiles, hidden=hidden)
    row = pl.BlockSpec((tm, d), lambda i: (i, 0))
    return pl.pallas_call(
        kern,
        grid=(n // tm,),
        in_specs=[row, _halo_spec(tm, d), _resident_spec((d, 2 * hidden)),
                  _const_spec((FFN_CONV, 2 * hidden)), _const_spec((1, 2 * hidden)),
                  _resident_spec((hidden, d)), _const_spec((1, d)), _const_spec((1, d))],
        out_specs=row,
        out_shape=jax.ShapeDtypeStruct((n, d), F32),
        scratch_shapes=[pltpu.VMEM((SUBLANES + tm, d), BF16), pltpu.VMEM((tm, hidden), BF16)],
        compiler_params=_params(1),
        name="conv_ffn",
    )(x2, x2, w_up.astype(BF16), conv_w.astype(F32), conv_b.astype(F32).reshape(1, 2 * hidden),
      w_down.astype(BF16), g.astype(F32).reshape(1, d), b.astype(F32).reshape(1, d))


def _diff_in_kernel(x_ref, w_ref, c_ref, s1_ref, s2_ref, qk_ref, v_ref, *, n_qk, n_v, q_cols):
    xb = x_ref[...].astype(BF16)
    cos = c_ref[...]
    s1 = s1_ref[...]
    s2 = s2_ref[...]
    half = DIFF_HD // 2
    for c in range(n_qk):
        h = _dot(xb, w_ref[:, c * MXU_N:(c + 1) * MXU_N])
        for j in range(MXU_N // LANES):
            hs = h[:, j * LANES:(j + 1) * LANES]
            rot = hs * cos + pltpu.roll(hs, LANES - half, 1) * s1 + pltpu.roll(hs, half, 1) * s2
            lo = c * MXU_N + j * LANES
            if lo < q_cols:
                rot = rot * (DIFF_HD ** -0.5 * math.log2(math.e))
            qk_ref[:, lo:lo + LANES] = rot.astype(BF16)
    base = n_qk * MXU_N
    for c in range(n_v):
        v_ref[:, c * MXU_N:(c + 1) * MXU_N] = _dot(
            xb, w_ref[:, base + c * MXU_N:base + (c + 1) * MXU_N]).astype(BF16)


def _diff_in(x2, w_in, tables, tm):
    n, d = x2.shape
    hw = DIFF_HEADS * 2 * DIFF_HD
    kern = functools.partial(_diff_in_kernel, n_qk=2 * hw // MXU_N, n_v=hw // MXU_N, q_cols=hw)
    row = lambda w: pl.BlockSpec((tm, w), lambda i: (i, 0))
    return pl.pallas_call(
        kern,
        grid=(n // tm,),
        in_specs=[row(d), _resident_spec((d, 3 * hw)), row(LANES), row(LANES), row(LANES)],
        out_specs=[row(2 * hw), row(hw)],
        out_shape=[jax.ShapeDtypeStruct((n, 2 * hw), BF16), jax.ShapeDtypeStruct((n, hw), BF16)],
        compiler_params=_params(1),
        name="diff_in",
    )(x2, w_in.astype(BF16), *tables)


def _diff_attn_kernel(q_ref, k_ref, v_ref, lq1_ref, lk1_ref, lq2_ref, lk2_ref, sw_ref, o_ref,
                      *, t, blk, lambda_init):
    lam = (jnp.exp(jnp.sum(lq1_ref[...] * lk1_ref[...], axis=-1, keepdims=True))
           - jnp.exp(jnp.sum(lq2_ref[...] * lk2_ref[...], axis=-1, keepdims=True))
           + lambda_init)
    lane = lax.broadcasted_iota(jnp.int32, (blk, 2 * DIFF_HD), 1)
    first_map = lane < DIFF_HD
    r = lax.broadcasted_iota(jnp.int32, (2 * blk, blk), 0)
    s = lax.broadcasted_iota(jnp.int32, (2 * blk, blk), 1)
    diag_ok = s <= (r % blk)
    sw = sw_ref[...]

    def scores(qi):
        q = q_ref[0, qi * blk:(qi + 1) * blk, :]
        zero = jnp.zeros_like(q)
        qs = jnp.concatenate([jnp.where(first_map, q, zero), jnp.where(first_map, zero, q)], axis=0)
        d0 = qi * blk
        sc_d = jnp.where(diag_ok, _dot_nt(qs, k_ref[0, d0:d0 + blk, :]), -jnp.inf)
        sc_f = _dot_nt(qs, k_ref[0, 0:d0, :]) if qi > 0 else None
        return sc_d, sc_f

    def softmax(qi, sc_d, sc_f):
        m = jnp.max(sc_d, axis=-1, keepdims=True)
        if qi > 0:
            m = jnp.maximum(m, jnp.max(sc_f, axis=-1, keepdims=True))
        p_d = jnp.exp2(sc_d - m)
        l = jnp.sum(p_d, axis=-1, keepdims=True)
        p_f = None
        if qi > 0:
            p_f = jnp.exp2(sc_f - m)
            l = l + jnp.sum(p_f, axis=-1, keepdims=True)
            p_f = p_f.astype(BF16)
        return p_d.astype(BF16), p_f, l

    def weighted_values(qi, p_d, p_f, l):
        d0 = qi * blk
        acc = _dot(p_d, v_ref[0, d0:d0 + blk, :])
        if qi > 0:
            acc = acc + _dot(p_f, v_ref[0, 0:d0, :])
        on = acc / l
        o = on[:blk] - lam * on[blk:]
        ms = jnp.mean(o * o, axis=-1, keepdims=True)
        o = o * lax.rsqrt(ms + RMS_EPS) * sw * (1.0 - lambda_init)
        o_ref[0, qi * blk:(qi + 1) * blk, :] = o.astype(BF16)

    n_q = t // blk
    order = list(range(n_q - 1, -1, -1))
    sc = {}
    for step in range(n_q + ATTN_LEAD):
        if step < n_q:
            sc[order[step]] = scores(order[step])
        if step >= ATTN_LEAD:
            qi = order[step - ATTN_LEAD]
            weighted_values(qi, *softmax(qi, *sc.pop(qi)))


def _diff_attn(qk, v, lam_q1, lam_k1, lam_q2, lam_k2, subln_w, lambda_init, b, t):
    hw = DIFF_HEADS * 2 * DIFF_HD
    hd2 = 2 * DIFF_HD
    blk = ATTN_BLOCK if t % ATTN_BLOCK == 0 else t
    qk3 = qk.reshape(b, t, 2 * hw)
    v3 = v.reshape(b, t, hw)
    kern = functools.partial(_diff_attn_kernel, t=t, blk=blk, lambda_init=lambda_init)
    head = lambda off: pl.BlockSpec((1, t, hd2), lambda bi, hi: (bi, 0, hi + off))
    vec = lambda a: a.astype(F32).reshape(1, -1)
    out = pl.pallas_call(
        kern,
        grid=(b, DIFF_HEADS),
        in_specs=[head(0), head(DIFF_HEADS), head(0)] + [_const_spec((1, DIFF_HD))] * 4
                 + [_const_spec((1, hd2))],
        out_specs=head(0),
        out_shape=jax.ShapeDtypeStruct((b, t, hw), BF16),
        compiler_params=_params(2),
        name="diff_attn",
    )(qk3, qk3, v3, vec(lam_q1), vec(lam_k1), vec(lam_q2), vec(lam_k2), vec(subln_w))
    return out.reshape(b * t, hw)


def kernel(x, positions, gdn_w_in, gdn_conv_w, gdn_a_log, gdn_dt_bias, gdn_norm_w, gdn_w_out,
           diff_w_in, diff_lam_q1, diff_lam_k1, diff_lam_q2, diff_lam_k2, diff_subln_w, diff_w_out,
           ffn_w_up, ffn_conv_w, ffn_conv_b, ffn_w_down, ln_mix_g, ln_mix_b, ln_ffn_g, ln_ffn_b):
    b, t, d = x.shape
    tm = _row_tile(t)
    seq_tiles = t // tm
    x2 = x.reshape(b * t, d).astype(F32)
    tables = _rope_tables(positions, tm)
    for i in range(DEPTH):
        j = i // 2
        if i % 2 == 0:
            qkv, gb = _gdn_in(x2, gdn_w_in[j], gdn_conv_w[j], gdn_a_log[j], gdn_dt_bias[j],
                              tm, seq_tiles)
            gate_lo = 2 * GDN_HEADS * GDN_DK + GDN_HEADS * GDN_DV
            w_gate = gdn_w_in[j][:, gate_lo:gate_lo + GDN_HEADS * GDN_DV]
            x2 = _gdn_chunk(qkv, x2, w_gate, gb, gdn_norm_w[j], gdn_w_out[j], ln_mix_g[i], ln_mix_b[i],
                            b, t)
        else:
            lambda_init = 0.8 - 0.6 * math.exp(-0.3 * i)
            qk, v = _diff_in(x2, diff_w_in[j], tables, tm)
            mixed = _diff_attn(qk, v, diff_lam_q1[j], diff_lam_k1[j], diff_lam_q2[j], diff_lam_k2[j],
                               diff_subln_w[j], lambda_init, b, t)
            x2 = _proj_ln(mixed, diff_w_out[j], x2, ln_mix_g[i], ln_mix_b[i], tm)
        x2 = _ffn(x2, ffn_w_up[i], ffn_conv_w[i], ffn_conv_b[i], ffn_w_down[i],
                  ln_ffn_g[i], ln_ffn_b[i], tm, seq_tiles)
    return x2.reshape(b, t, d)
```

```python
import functools
import math

import jax
import jax.numpy as jnp
from jax import lax
from jax.experimental import pallas as pl
from jax.experimental.pallas import tpu as pltpu

F32 = jnp.float32
BF16 = jnp.bfloat16

DEPTH = 4
GDN_HEADS = 8
GDN_DK = 128
GDN_DV = 128
GDN_CONV = 4
GDN_CHUNK = 64
DIFF_HEADS = 8
DIFF_HD = 64
ROPE_THETA = 10000.0
FFN_CONV = 3
DEEPNORM_ALPHA = (2.0 * DEPTH) ** 0.25
LN_EPS = 1e-5
RMS_EPS = 1e-6

LANES = 128
SUBLANES = 8
MXU_N = 256
VMEM_LIMIT = 56 * 1024 * 1024

ROW_TILE = 1024
ATTN_HEADS_PER_STEP = 2
ATTN_LEAD = 1
ATTN_BLOCK = 256
GDN_STEP = 512
GDN_INV_BLOCK = 16


def _row_tile(t):
    return ROW_TILE if t % ROW_TILE == 0 else t


def _sigmoid(x):
    return 1.0 / (1.0 + jnp.exp2(x * -math.log2(math.e)))


def _silu(x):
    return x * _sigmoid(x)


def _dot(a, b):
    return jnp.dot(a, b, preferred_element_type=F32)


def _dot_nt(a, b):
    return lax.dot_general(a, b, (((1,), (1,)), ((), ())), preferred_element_type=F32)


def _shift_rows(h, s):
    return pltpu.roll(h, s, 0)


def _causal_conv(h, cw, width):
    y = h * cw[width - 1:width, :]
    for s in range(1, width):
        y = y + _shift_rows(h, s) * cw[width - 1 - s:width - s, :]
    return y[SUBLANES:, :]


def _layer_norm_rows(z, g, b):
    mu = jnp.mean(z, axis=-1, keepdims=True)
    zc = z - mu
    var = jnp.mean(zc * zc, axis=-1, keepdims=True)
    return zc * lax.rsqrt(var + LN_EPS) * g + b


def _stage_rows(x_ref, xh_ref, xb_ref, tile, seq_tiles):
    first = (tile % seq_tiles) == 0
    halo = jnp.where(first, 0.0, xh_ref[...])
    xb_ref[0:SUBLANES, :] = halo.astype(BF16)
    xb_ref[SUBLANES:, :] = x_ref[...].astype(BF16)


def _halo_spec(tm, d):
    return pl.BlockSpec((SUBLANES, d), lambda i: (jnp.maximum(i * (tm // SUBLANES) - 1, 0), 0))


def _const_spec(shape):
    return pl.BlockSpec(shape, lambda *_: (0,) * len(shape))


def _resident_spec(shape):
    return pl.BlockSpec(shape, lambda *_: (0,) * len(shape), pipeline_mode=pl.Buffered(1))


def _params(n_axes):
    return pltpu.CompilerParams(
        dimension_semantics=("arbitrary",) * n_axes, vmem_limit_bytes=VMEM_LIMIT)


def _rope_kernel(pos_ref, inv_ref, c_ref, s1_ref, s2_ref):
    tm = pos_ref.shape[0]
    nf = DIFF_HD // 2
    groups = LANES // nf
    blk = tm // groups
    lane = lax.broadcasted_iota(jnp.int32, (blk, LANES), 1)
    pos = pos_ref[...].astype(F32)
    packed = jnp.zeros((blk, LANES), F32)
    for j in range(groups):
        packed = jnp.where(lane // nf == j, jnp.broadcast_to(pos[j * blk:(j + 1) * blk, :], (blk, LANES)),
                           packed)
    ang = packed * inv_ref[...]
    cos = jnp.cos(ang)
    sin = jnp.sin(ang)
    lower = (lane % DIFF_HD) < nf
    for j in range(groups):
        cj, sj = cos, sin
        for g in range(groups):
            shift = ((g - j) % groups) * nf
            if shift:
                cj = jnp.where(lane // nf == g, pltpu.roll(cos, shift, 1), cj)
                sj = jnp.where(lane // nf == g, pltpu.roll(sin, shift, 1), sj)
        rows = slice(j * blk, (j + 1) * blk)
        c_ref[rows, :] = cj
        s1_ref[rows, :] = jnp.where(lower, -sj, 0.0)
        s2_ref[rows, :] = jnp.where(lower, 0.0, sj)


def _rope_tables(positions, tm):
    n = positions.size
    inv_freq = ROPE_THETA ** (-jnp.arange(0, DIFF_HD, 2, dtype=F32) / DIFF_HD)
    inv = jnp.tile(inv_freq, LANES // (DIFF_HD // 2)).reshape(1, LANES)
    pos = positions.reshape(n, 1)
    out = jax.ShapeDtypeStruct((n, LANES), F32)
    return pl.pallas_call(
        _rope_kernel,
        grid=(n // tm,),
        in_specs=[pl.BlockSpec((tm, 1), lambda i: (i, 0)), _const_spec((1, LANES))],
        out_specs=[pl.BlockSpec((tm, LANES), lambda i: (i, 0))] * 3,
        out_shape=[out] * 3,
        compiler_params=_params(1),
        name="rope_tables",
    )(pos, inv)


def _gdn_in_kernel(x_ref, xh_ref, w_ref, wab_ref, cw_ref, alog_ref, dtb_ref,
                   qkv_ref, gb_ref, xb_ref, *, seq_tiles, n_qk, n_v):
    _stage_rows(x_ref, xh_ref, xb_ref, pl.program_id(0), seq_tiles)
    xb = xb_ref[...]
    xt = xb[SUBLANES:, :]
    for c in range(n_qk + n_v):
        cols = slice(c * MXU_N, (c + 1) * MXU_N)
        h = _dot(xb, w_ref[:, cols])
        y = _silu(_causal_conv(h, cw_ref[:, cols], GDN_CONV))
        if c < n_qk:
            parts = []
            for j in range(MXU_N // GDN_DK):
                yh = y[:, j * GDN_DK:(j + 1) * GDN_DK]
                ss = jnp.sum(yh * yh, axis=-1, keepdims=True)
                parts.append(yh * lax.rsqrt(ss + RMS_EPS))
            y = jnp.concatenate(parts, axis=1)
        qkv_ref[:, cols] = y.astype(BF16)
    hab = _dot(xt, wab_ref[...])
    z = hab + dtb_ref[...]
    softplus = jnp.maximum(z, 0.0) + jnp.log(1.0 + jnp.exp(-jnp.abs(z)))
    g = -jnp.exp(alog_ref[...]) * softplus
    lane = lax.broadcasted_iota(jnp.int32, hab.shape, 1)
    gb_ref[...] = jnp.where(lane < GDN_HEADS, g, _sigmoid(hab))


def _gdn_in(x2, w_in, conv_w, a_log, dt_bias, tm, seq_tiles):
    n, d = x2.shape
    qk_w = 2 * GDN_HEADS * GDN_DK
    v_w = GDN_HEADS * GDN_DV
    conv_w_ = qk_w + v_w
    w_conv = w_in[:, :conv_w_].astype(BF16)
    w_ab = jnp.pad(w_in[:, conv_w_ + v_w:], ((0, 0), (0, LANES - 2 * GDN_HEADS))).astype(BF16)
    pad = (0, LANES - GDN_HEADS)
    alog = jnp.pad(a_log.astype(F32), pad).reshape(1, LANES)
    dtb = jnp.pad(dt_bias.astype(F32), pad).reshape(1, LANES)
    kern = functools.partial(_gdn_in_kernel, seq_tiles=seq_tiles, n_qk=qk_w // MXU_N,
                             n_v=v_w // MXU_N)
    row = lambda w: pl.BlockSpec((tm, w), lambda i: (i, 0))
    return pl.pallas_call(
        kern,
        grid=(n // tm,),
        in_specs=[row(d), _halo_spec(tm, d),
                  _resident_spec((d, conv_w_)),
                  _const_spec((d, LANES)),
                  _const_spec((GDN_CONV, conv_w_)), _const_spec((1, LANES)),
                  _const_spec((1, LANES))],
        out_specs=[row(conv_w_), row(LANES)],
        out_shape=[jax.ShapeDtypeStruct((n, conv_w_), BF16),
                   jax.ShapeDtypeStruct((n, LANES), F32)],
        scratch_shapes=[pltpu.VMEM((SUBLANES + tm, d), BF16)],
        compiler_params=_params(1),
        name="gdn_in",
    )(x2, x2, w_conv, w_ab, conv_w.astype(F32), alog, dtb)


def _unit_lower_inverses(a_list, eye, diag_blk, merge_masks, bdmask):
    c = GDN_CHUNK

    def bd(y):
        yy = jnp.concatenate([y, y], axis=0)
        return jnp.where(bdmask, yy, jnp.zeros_like(yy))

    n = [jnp.where(diag_blk, -a, 0.0) for a in a_list]
    p = [eye + x for x in n]
    nb = [x.astype(BF16) for x in n]
    n = [_dot(x, bd(x)) for x in nb]
    squarings = GDN_INV_BLOCK.bit_length() - 1
    for _ in range(squarings - 2):
        nb = [x.astype(BF16) for x in n]
        r = [_dot(jnp.concatenate([pi.astype(BF16), ni], axis=0), bd(ni)) for pi, ni in zip(p, nb)]
        p = [pi + ri[:c] for pi, ri in zip(p, r)]
        n = [ri[c:] for ri in r]
    p = [pi + _dot(pi.astype(BF16), bd(ni.astype(BF16))) for pi, ni in zip(p, n)]
    for off in merge_masks:
        pb = [pi.astype(BF16) for pi in p]
        t = [_dot(jnp.where(off, a, 0.0).astype(BF16), bd(pbi)) for a, pbi in zip(a_list, pb)]
        p = [pi - _dot(pbi, bd(ti.astype(BF16))) for pi, pbi, ti in zip(p, pb, t)]
    return p


def _gdn_chunk_kernel(q_ref, k_ref, v_ref, x_ref, wg_ref, gb_ref, nw_ref, wo_ref, lg_ref, lb_ref,
                      o_ref, s_ref, wq_ref, u_ref, ak_ref, gate_ref, mix_ref, *, n_sub):
    c = GDN_CHUNK

    @pl.when(pl.program_id(1) == 0)
    def _():
        s_ref[...] = jnp.zeros_like(s_ref)

    step = n_sub * c
    gb = gb_ref[0]
    in_chunk = lax.broadcasted_iota(jnp.int32, (step, LANES), 0) % c
    gc_cols = gb
    shift = 1
    while shift < c:
        gc_cols = gc_cols + jnp.where(in_chunk >= shift, pltpu.roll(gc_cols, shift, 0), 0.0)
        shift *= 2
    gc_rows = gc_cols.T[:GDN_HEADS, :]
    egc_cols = jnp.exp(gc_cols)
    bl = GDN_HEADS

    c2 = 2 * c
    r = lax.broadcasted_iota(jnp.int32, (c, c2), 0)
    lane = lax.broadcasted_iota(jnp.int32, (c, c2), 1)
    s = lane % c
    first = lane < c
    causal = s <= r
    strict = s < r
    eye = jnp.where(r == s, 1.0, 0.0)
    diag_blk = (r // GDN_INV_BLOCK) == (s // GDN_INV_BLOCK)
    merge_masks = []
    size = GDN_INV_BLOCK
    while size < c:
        merge_masks.append(((r // (2 * size)) == (s // (2 * size))) & ((r // size) != (s // size)))
        size *= 2
    bdmask = (lax.broadcasted_iota(jnp.int32, (c2, c2), 0) // c
              == lax.broadcasted_iota(jnp.int32, (c2, c2), 1) // c)
    kmask = (lax.broadcasted_iota(jnp.int32, (c2, 2 * GDN_DK), 0) // c
             == lax.broadcasted_iota(jnp.int32, (c2, 2 * GDN_DK), 1) // GDN_DK)
    scale = GDN_DK ** -0.5
    heads = range(GDN_HEADS)
    hcols = [slice(h * GDN_DK, (h + 1) * GDN_DK) for h in heads]
    n_hp = GDN_HEADS // 2
    pcols = [slice(hp * 2 * GDN_DK, (hp + 1) * 2 * GDN_DK) for hp in range(n_hp)]

    def pack_cols(x, rw, lane0):
        return jnp.where(first, jnp.broadcast_to(x[rw, lane0:lane0 + 1], (c, c2)),
                         jnp.broadcast_to(x[rw, lane0 + 1:lane0 + 2], (c, c2)))

    units = [(sub, hp) for sub in range(n_sub) for hp in range(n_hp)]
    rws = [slice(sub * c, (sub + 1) * c) for sub, _ in units]
    q2 = [q_ref[0, rw, pcols[hp]] for rw, (_, hp) in zip(rws, units)]
    k2 = [k_ref[0, rw, pcols[hp]] for rw, (_, hp) in zip(rws, units)]
    kbd = []
    for k in k2:
        kk = jnp.concatenate([k, k], axis=0)
        kbd.append(jnp.where(kmask, kk, jnp.zeros_like(kk)))
    qkk = [_dot_nt(jnp.concatenate([q, k], axis=0), kb) for q, k, kb in zip(q2, k2, kbd)]
    gate_ref[...] = _dot(x_ref[0].astype(BF16), wg_ref[...])
    decay = []
    for rw, (_, hp) in zip(rws, units):
        gcr = jnp.concatenate([gc_rows[2 * hp:2 * hp + 1, rw], gc_rows[2 * hp + 1:2 * hp + 2, rw]], axis=1)
        diff = jnp.where(causal, pack_cols(gc_cols, rw, 2 * hp) - gcr, 0.0)
        decay.append(jnp.where(causal, jnp.exp(diff), 0.0))
    a_list = [jnp.where(strict, x[c:] * dc * pack_cols(gb, rw, bl + 2 * hp), 0.0)
              for x, dc, rw, (_, hp) in zip(qkk, decay, rws, units)]
    attn = [(x[:c] * scale * dc).astype(BF16) for x, dc in zip(qkk, decay)]
    inv = _unit_lower_inverses(a_list, eye, diag_blk, merge_masks, bdmask)
    zeros = jnp.zeros((c, GDN_DV + GDN_DK), BF16)
    rhs, kfs = [], []
    for rw, (_, hp) in zip(rws, units):
        both = []
        for h in (2 * hp, 2 * hp + 1):
            bcol = gb[rw, bl + h:bl + h + 1]
            kf = k_ref[0, rw, hcols[h]].astype(F32)
            kfs.append(kf)
            both.append(jnp.concatenate([v_ref[0, rw, hcols[h]].astype(F32) * bcol,
                                         kf * (bcol * egc_cols[rw, h:h + 1])], axis=1).astype(BF16))
        rhs.append(jnp.concatenate([jnp.concatenate([both[0], zeros], axis=1),
                                    jnp.concatenate([zeros, both[1]], axis=1)], axis=0))
    sol = [_dot(t.astype(BF16), x) for t, x in zip(inv, rhs)]
    for i, (rw, (sub, hp)) in enumerate(zip(rws, units)):
        kds = []
        for j, h in enumerate((2 * hp, 2 * hp + 1)):
            base = j * (GDN_DV + GDN_DK)
            g_last = gc_cols[(sub + 1) * c - 1:(sub + 1) * c, h:h + 1]
            u_ref[sub, h] = sol[i][:, base:base + GDN_DV]
            qd = (q_ref[0, rw, hcols[h]].astype(F32) * (egc_cols[rw, h:h + 1] * scale)).astype(BF16)
            wq_ref[sub, h] = jnp.concatenate(
                [sol[i][:, base + GDN_DV:base + GDN_DV + GDN_DK].astype(BF16), qd], axis=0)
            kds.append(kfs[2 * i + j] * jnp.exp(g_last - gc_cols[rw, h:h + 1]))
        kdt = jnp.concatenate(kds, axis=0).T
        ak_ref[sub, hp] = jnp.concatenate([attn[i], kdt.astype(BF16)], axis=0)

    nw = nw_ref[...]
    zv = jnp.zeros((c, GDN_DV), BF16)
    proj_every = n_sub // 2 if n_sub % 2 == 0 else n_sub
    for sub in range(n_sub):
        rows = slice(sub * c, (sub + 1) * c)
        g_last = gc_cols[(sub + 1) * c - 1:(sub + 1) * c, :]
        st = [s_ref[h] for h in heads]
        r1 = [_dot(wq_ref[sub, h], st[h].astype(BF16)) for h in heads]
        vb = [(u_ref[sub, h] - r1[h][:c]).astype(BF16) for h in heads]
        r2p = []
        for hp in range(n_hp):
            w2 = jnp.concatenate([jnp.concatenate([vb[2 * hp], zv], axis=1),
                                  jnp.concatenate([zv, vb[2 * hp + 1]], axis=1)], axis=0)
            r2p.append(_dot(ak_ref[sub, hp], w2))
        r2 = [r2p[h // 2][:, (h % 2) * GDN_DV:(h % 2 + 1) * GDN_DV] for h in heads]
        for h in heads:
            s_ref[h] = st[h] * jnp.exp(g_last[:, h:h + 1]) + r2[h][c:]
            o = r1[h][c:] + r2[h][:c]
            ms = jnp.mean(o * o, axis=-1, keepdims=True)
            on = o * lax.rsqrt(ms + RMS_EPS) * nw
            gt = gate_ref[rows, hcols[h]]
            mix_ref[rows, hcols[h]] = (on * _silu(gt)).astype(BF16)
        if (sub + 1) % proj_every == 0:
            done = slice((sub + 1 - proj_every) * c, (sub + 1) * c)
            z = DEEPNORM_ALPHA * x_ref[0, done, :] + _dot(mix_ref[done, :], wo_ref[...])
            o_ref[0, done, :] = _layer_norm_rows(z, lg_ref[...], lb_ref[...])


def _gdn_chunk(qkv, x2, w_gate, gb, norm_w, w_out, ln_g, ln_b, b, t):
    hq = GDN_HEADS * GDN_DK
    hv = GDN_HEADS * GDN_DV
    d = x2.shape[1]
    step = GDN_STEP if t % GDN_STEP == 0 else t
    qkv3 = qkv.reshape(b, t, 2 * hq + hv)
    x3 = x2.reshape(b, t, d)
    gb3 = gb.reshape(b, t, LANES)
    n_sub = step // GDN_CHUNK
    kern = functools.partial(_gdn_chunk_kernel, n_sub=n_sub)
    col = lambda j: pl.BlockSpec((1, step, hq), lambda bi, ci: (bi, ci, j))
    out = pl.pallas_call(
        kern,
        grid=(b, t // step),
        in_specs=[col(0), col(1), col(2),
                  pl.BlockSpec((1, step, d), lambda bi, ci: (bi, ci, 0)), _resident_spec((d, hv)),
                  pl.BlockSpec((1, step, LANES), lambda bi, ci: (bi, ci, 0)),
                  _const_spec((1, GDN_DV)), _resident_spec((hv, d)), _const_spec((1, d)),
                  _const_spec((1, d))],
        out_specs=pl.BlockSpec((1, step, d), lambda bi, ci: (bi, ci, 0)),
        out_shape=jax.ShapeDtypeStruct((b, t, d), F32),
        scratch_shapes=[pltpu.VMEM((GDN_HEADS, GDN_DK, GDN_DV), F32),
                        pltpu.VMEM((n_sub, GDN_HEADS, 2 * GDN_CHUNK, GDN_DK), BF16),
                        pltpu.VMEM((n_sub, GDN_HEADS, GDN_CHUNK, GDN_DV), F32),
                        pltpu.VMEM((n_sub, GDN_HEADS // 2, GDN_CHUNK + GDN_DK, 2 * GDN_CHUNK), BF16),
                        pltpu.VMEM((step, hv), F32),
                        pltpu.VMEM((step, hv), BF16)],
        compiler_params=_params(2),
        name="gdn_chunk",
    )(qkv3, qkv3, qkv3, x3, w_gate.astype(BF16), gb3, norm_w.astype(F32).reshape(1, GDN_DV),
      w_out.astype(BF16), ln_g.astype(F32).reshape(1, d), ln_b.astype(F32).reshape(1, d))
    return out.reshape(b * t, d)


def _proj_ln_kernel(y_ref, w_ref, x_ref, g_ref, b_ref, o_ref):
    m = _dot(y_ref[...], w_ref[...])
    z = DEEPNORM_ALPHA * x_ref[...] + m
    o_ref[...] = _layer_norm_rows(z, g_ref[...], b_ref[...])


def _proj_ln(y, w, x2, g, b, tm):
    n, d = x2.shape
    k = y.shape[1]
    row = lambda w_: pl.BlockSpec((tm, w_), lambda i: (i, 0))
    return pl.pallas_call(
        _proj_ln_kernel,
        grid=(n // tm,),
        in_specs=[row(k), _resident_spec((k, d)), row(d), _const_spec((1, d)), _const_spec((1, d))],
        out_specs=row(d),
        out_shape=jax.ShapeDtypeStruct((n, d), F32),
        compiler_params=_params(1),
        name="proj_ln",
    )(y, w.astype(BF16), x2, g.astype(F32).reshape(1, d), b.astype(F32).reshape(1, d))


def _ffn_kernel(x_ref, xh_ref, wup_ref, cw_ref, cb_ref, wdn_ref, g_ref, b_ref, o_ref,
                xb_ref, act_ref, *, seq_tiles, hidden):
    _stage_rows(x_ref, xh_ref, xb_ref, pl.program_id(0), seq_tiles)
    xb = xb_ref[...]
    n_chunks = hidden // MXU_N

    def project(c):
        return [_dot(xb, wup_ref[:, base + c * MXU_N:base + (c + 1) * MXU_N]) for base in (0, hidden)]

    def finish(c, hs):
        halves = []
        for base, h in zip((0, hidden), hs):
            cols = slice(base + c * MXU_N, base + (c + 1) * MXU_N)
            halves.append(_causal_conv(h, cw_ref[:, cols], FFN_CONV) + cb_ref[:, cols])
        act_ref[:, c * MXU_N:(c + 1) * MXU_N] = (_silu(halves[0]) * halves[1]).astype(BF16)

    for c in range(n_chunks):
        finish(c, project(c))
    f = _dot(act_ref[...], wdn_ref[...])
    z = DEEPNORM_ALPHA * x_ref[...] + f
    o_ref[...] = _layer_norm_rows(z, g_ref[...], b_ref[...])


def _ffn(x2, w_up, conv_w, conv_b, w_down, g, b, tm, seq_tiles):
    n, d = x2.shape
    hidden = w_down.shape[0]
    kern = functools.partial(_ffn_kernel, seq_tiles=seq_tiles, hidden=hidden)
    row = pl.BlockSpec((tm, d), lambda i: (i, 0))
    return pl.pallas_call(
        kern,
        grid=(n // tm,),
        in_specs=[row, _halo_spec(tm, d), _resident_spec((d, 2 * hidden)),
                  _const_spec((FFN_CONV, 2 * hidden)), _const_spec((1, 2 * hidden)),
                  _resident_spec((hidden, d)), _const_spec((1, d)), _const_spec((1, d))],
        out_specs=row,
        out_shape=jax.ShapeDtypeStruct((n, d), F32),
        scratch_shapes=[pltpu.VMEM((SUBLANES + tm, d), BF16), pltpu.VMEM((tm, hidden), BF16)],
        compiler_params=_params(1),
        name="conv_ffn",
    )(x2, x2, w_up.astype(BF16), conv_w.astype(F32), conv_b.astype(F32).reshape(1, 2 * hidden),
      w_down.astype(BF16), g.astype(F32).reshape(1, d), b.astype(F32).reshape(1, d))


def _diff_in_kernel(x_ref, w_ref, c_ref, s1_ref, s2_ref, qk_ref, v_ref, *, n_qk, n_v, q_cols):
    xb = x_ref[...].astype(BF16)
    cos = c_ref[...]
    s1 = s1_ref[...]
    s2 = s2_ref[...]
    half = DIFF_HD // 2
    for c in range(n_qk):
        h = _dot(xb, w_ref[:, c * MXU_N:(c + 1) * MXU_N])
        for j in range(MXU_N // LANES):
            hs = h[:, j * LANES:(j + 1) * LANES]
            rot = hs * cos + pltpu.roll(hs, LANES - half, 1) * s1 + pltpu.roll(hs, half, 1) * s2
            lo = c * MXU_N + j * LANES
            if lo < q_cols:
                rot = rot * (DIFF_HD ** -0.5 * math.log2(math.e))
            qk_ref[:, lo:lo + LANES] = rot.astype(BF16)
    base = n_qk * MXU_N
    for c in range(n_v):
        v_ref[:, c * MXU_N:(c + 1) * MXU_N] = _dot(
            xb, w_ref[:, base + c * MXU_N:base + (c + 1) * MXU_N]).astype(BF16)


def _diff_in(x2, w_in, tables, tm):
    n, d = x2.shape
    hw = DIFF_HEADS * 2 * DIFF_HD
    kern = functools.partial(_diff_in_kernel, n_qk=2 * hw // MXU_N, n_v=hw // MXU_N, q_cols=hw)
    row = lambda w: pl.BlockSpec((tm, w), lambda i: (i, 0))
    return pl.pallas_call(
        kern,
        grid=(n // tm,),
        in_specs=[row(d), _resident_spec((d, 3 * hw)), row(LANES), row(LANES), row(LANES)],
        out_specs=[row(2 * hw), row(hw)],
        out_shape=[jax.ShapeDtypeStruct((n, 2 * hw), BF16), jax.ShapeDtypeStruct((n, hw), BF16)],
        compiler_params=_params(1),
        name="diff_in",
    )(x2, w_in.astype(BF16), *tables)


def _diff_attn_kernel(q_ref, k_ref, v_ref, lq1_ref, lk1_ref, lq2_ref, lk2_ref, sw_ref, o_ref,
                      *, t, blk, n_heads, lambda_init):
    lam = (jnp.exp(jnp.sum(lq1_ref[...] * lk1_ref[...], axis=-1, keepdims=True))
           - jnp.exp(jnp.sum(lq2_ref[...] * lk2_ref[...], axis=-1, keepdims=True))
           + lambda_init)
    lane = lax.broadcasted_iota(jnp.int32, (blk, 2 * DIFF_HD), 1)
    first_map = lane < DIFF_HD
    r = lax.broadcasted_iota(jnp.int32, (2 * blk, blk), 0)
    s = lax.broadcasted_iota(jnp.int32, (2 * blk, blk), 1)
    diag_ok = s <= (r % blk)
    sw = sw_ref[...]

    def scores(hh, qi):
        cols = slice(hh * 2 * DIFF_HD, (hh + 1) * 2 * DIFF_HD)
        q = q_ref[0, qi * blk:(qi + 1) * blk, cols]
        zero = jnp.zeros_like(q)
        qs = jnp.concatenate([jnp.where(first_map, q, zero), jnp.where(first_map, zero, q)], axis=0)
        d0 = qi * blk
        sc_d = jnp.where(diag_ok, _dot_nt(qs, k_ref[0, d0:d0 + blk, cols]), -jnp.inf)
        sc_f = _dot_nt(qs, k_ref[0, 0:d0, cols]) if qi > 0 else None
        return sc_d, sc_f

    def softmax(qi, sc_d, sc_f):
        m = jnp.max(sc_d, axis=-1, keepdims=True)
        if qi > 0:
            m = jnp.maximum(m, jnp.max(sc_f, axis=-1, keepdims=True))
        p_d = jnp.exp2(sc_d - m)
        l = jnp.sum(p_d, axis=-1, keepdims=True)
        p_f = None
        if qi > 0:
            p_f = jnp.exp2(sc_f - m)
            l = l + jnp.sum(p_f, axis=-1, keepdims=True)
            p_f = p_f.astype(BF16)
        return p_d.astype(BF16), p_f, l

    def weighted_values(hh, qi, p_d, p_f, l):
        cols = slice(hh * 2 * DIFF_HD, (hh + 1) * 2 * DIFF_HD)
        d0 = qi * blk
        acc = _dot(p_d, v_ref[0, d0:d0 + blk, cols])
        if qi > 0:
            acc = acc + _dot(p_f, v_ref[0, 0:d0, cols])
        on = acc / l
        o = on[:blk] - lam * on[blk:]
        ms = jnp.mean(o * o, axis=-1, keepdims=True)
        o = o * lax.rsqrt(ms + RMS_EPS) * sw * (1.0 - lambda_init)
        o_ref[0, qi * blk:(qi + 1) * blk, cols] = o.astype(BF16)

    n_q = t // blk
    order = [(hh, qi) for qi in range(n_q - 1, -1, -1) for hh in range(n_heads)]
    sc = {}
    for step in range(len(order) + ATTN_LEAD):
        if step < len(order):
            sc[order[step]] = scores(*order[step])
        if step >= ATTN_LEAD:
            hh, qi = order[step - ATTN_LEAD]
            weighted_values(hh, qi, *softmax(qi, *sc.pop((hh, qi))))


def _diff_attn(qk, v, lam_q1, lam_k1, lam_q2, lam_k2, subln_w, lambda_init, b, t):
    hw = DIFF_HEADS * 2 * DIFF_HD
    hd2 = 2 * DIFF_HD
    blk = ATTN_BLOCK if t % ATTN_BLOCK == 0 else t
    qk3 = qk.reshape(b, t, 2 * hw)
    v3 = v.reshape(b, t, hw)
    g = ATTN_HEADS_PER_STEP
    kern = functools.partial(_diff_attn_kernel, t=t, blk=blk, n_heads=g, lambda_init=lambda_init)
    head = lambda off: pl.BlockSpec((1, t, g * hd2), lambda bi, hi: (bi, 0, hi + off))
    vec = lambda a: a.astype(F32).reshape(1, -1)
    out = pl.pallas_call(
        kern,
        grid=(b, DIFF_HEADS // g),
        in_specs=[head(0), head(DIFF_HEADS // g), head(0)] + [_const_spec((1, DIFF_HD))] * 4
                 + [_const_spec((1, hd2))],
        out_specs=head(0),
        out_shape=jax.ShapeDtypeStruct((b, t, hw), BF16),
        compiler_params=_params(2),
        name="diff_attn",
    )(qk3, qk3, v3, vec(lam_q1), vec(lam_k1), vec(lam_q2), vec(lam_k2), vec(subln_w))
    return out.reshape(b * t, hw)


def kernel(x, positions, gdn_w_in, gdn_conv_w, gdn_a_log, gdn_dt_bias, gdn_norm_w, gdn_w_out,
           diff_w_in, diff_lam_q1, diff_lam_k1, diff_lam_q2, diff_lam_k2, diff_subln_w, diff_w_out,
           ffn_w_up, ffn_conv_w, ffn_conv_b, ffn_w_down, ln_mix_g, ln_mix_b, ln_ffn_g, ln_ffn_b):
    b, t, d = x.shape
    tm = _row_tile(t)
    seq_tiles = t // tm
    x2 = x.reshape(b * t, d).astype(F32)
    tables = _rope_tables(positions, tm)
    for i in range(DEPTH):
        j = i // 2
        if i % 2 == 0:
            qkv, gb = _gdn_in(x2, gdn_w_in[j], gdn_conv_w[j], gdn_a_log[j], gdn_dt_bias[j],
                              tm, seq_tiles)
            gate_lo = 2 * GDN_HEADS * GDN_DK + GDN_HEADS * GDN_DV
            w_gate = gdn_w_in[j][:, gate_lo:gate_lo + GDN_HEADS * GDN_DV]
            x2 = _gdn_chunk(qkv, x2, w_gate, gb, gdn_norm_w[j], gdn_w_out[j], ln_mix_g[i], ln_mix_b[i],
                            b, t)
        else:
            lambda_init = 0.8 - 0.6 * math.exp(-0.3 * i)
            qk, v = _diff_in(x2, diff_w_in[j], tables, tm)
            mixed = _diff_attn(qk, v, diff_lam_q1[j], diff_lam_k1[j], diff_lam_q2[j], diff_lam_k2[j],
                               diff_subln_w[j], lambda_init, b, t)
            x2 = _proj_ln(mixed, diff_w_out[j], x2, ln_mix_g[i], ln_mix_b[i], tm)
        x2 = _ffn(x2, ffn_w_up[i], ffn_conv_w[i], ffn_conv_b[i], ffn_w_down[i],
                  ln_ffn_g[i], ln_ffn_b[i], tm, seq_tiles)
    return x2.reshape(b, t, d)
```

```python
import functools
import math

import jax
import jax.numpy as jnp
from jax import lax
from jax.experimental import pallas as pl
from jax.experimental.pallas import tpu as pltpu

F32 = jnp.float32
BF16 = jnp.bfloat16

DEPTH = 4
GDN_HEADS = 8
GDN_DK = 128
GDN_DV = 128
GDN_CONV = 4
GDN_CHUNK = 64
DIFF_HEADS = 8
DIFF_HD = 64
ROPE_THETA = 10000.0
FFN_CONV = 3
DEEPNORM_ALPHA = (2.0 * DEPTH) ** 0.25
LN_EPS = 1e-5
RMS_EPS = 1e-6

LANES = 128
SUBLANES = 8
MXU_N = 256
VMEM_LIMIT = 56 * 1024 * 1024

ROW_TILE = 1024
ATTN_HEADS_PER_STEP = 2
ATTN_LEAD = 1
ATTN_BLOCK = 256
GDN_STEP = 256
GDN_SEQS = 2
GDN_INV_BLOCK = 16


def _row_tile(t):
    return ROW_TILE if t % ROW_TILE == 0 else t


def _sigmoid(x):
    return 1.0 / (1.0 + jnp.exp2(x * -math.log2(math.e)))


def _silu(x):
    return x * _sigmoid(x)


def _dot(a, b):
    return jnp.dot(a, b, preferred_element_type=F32)


def _dot_nt(a, b):
    return lax.dot_general(a, b, (((1,), (1,)), ((), ())), preferred_element_type=F32)


def _shift_rows(h, s):
    return pltpu.roll(h, s, 0)


def _causal_conv(h, cw, width):
    y = h * cw[width - 1:width, :]
    for s in range(1, width):
        y = y + _shift_rows(h, s) * cw[width - 1 - s:width - s, :]
    return y[SUBLANES:, :]


def _layer_norm_rows(z, g, b):
    mu = jnp.mean(z, axis=-1, keepdims=True)
    zc = z - mu
    var = jnp.mean(zc * zc, axis=-1, keepdims=True)
    return zc * lax.rsqrt(var + LN_EPS) * g + b


def _stage_rows(x_ref, xh_ref, xb_ref, tile, seq_tiles):
    first = (tile % seq_tiles) == 0
    halo = jnp.where(first, 0.0, xh_ref[...])
    xb_ref[0:SUBLANES, :] = halo.astype(BF16)
    xb_ref[SUBLANES:, :] = x_ref[...].astype(BF16)


def _halo_spec(tm, d):
    return pl.BlockSpec((SUBLANES, d), lambda i: (jnp.maximum(i * (tm // SUBLANES) - 1, 0), 0))


def _const_spec(shape):
    return pl.BlockSpec(shape, lambda *_: (0,) * len(shape))


def _resident_spec(shape):
    return pl.BlockSpec(shape, lambda *_: (0,) * len(shape), pipeline_mode=pl.Buffered(1))


def _params(n_axes):
    return pltpu.CompilerParams(
        dimension_semantics=("arbitrary",) * n_axes, vmem_limit_bytes=VMEM_LIMIT)


def _rope_kernel(pos_ref, inv_ref, c_ref, s1_ref, s2_ref):
    tm = pos_ref.shape[0]
    nf = DIFF_HD // 2
    groups = LANES // nf
    blk = tm // groups
    lane = lax.broadcasted_iota(jnp.int32, (blk, LANES), 1)
    pos = pos_ref[...].astype(F32)
    packed = jnp.zeros((blk, LANES), F32)
    for j in range(groups):
        packed = jnp.where(lane // nf == j, jnp.broadcast_to(pos[j * blk:(j + 1) * blk, :], (blk, LANES)),
                           packed)
    ang = packed * inv_ref[...]
    cos = jnp.cos(ang)
    sin = jnp.sin(ang)
    lower = (lane % DIFF_HD) < nf
    for j in range(groups):
        cj, sj = cos, sin
        for g in range(groups):
            shift = ((g - j) % groups) * nf
            if shift:
                cj = jnp.where(lane // nf == g, pltpu.roll(cos, shift, 1), cj)
                sj = jnp.where(lane // nf == g, pltpu.roll(sin, shift, 1), sj)
        rows = slice(j * blk, (j + 1) * blk)
        c_ref[rows, :] = cj
        s1_ref[rows, :] = jnp.where(lower, -sj, 0.0)
        s2_ref[rows, :] = jnp.where(lower, 0.0, sj)


def _rope_tables(positions, tm):
    n = positions.size
    inv_freq = ROPE_THETA ** (-jnp.arange(0, DIFF_HD, 2, dtype=F32) / DIFF_HD)
    inv = jnp.tile(inv_freq, LANES // (DIFF_HD // 2)).reshape(1, LANES)
    pos = positions.reshape(n, 1)
    out = jax.ShapeDtypeStruct((n, LANES), F32)
    return pl.pallas_call(
        _rope_kernel,
        grid=(n // tm,),
        in_specs=[pl.BlockSpec((tm, 1), lambda i: (i, 0)), _const_spec((1, LANES))],
        out_specs=[pl.BlockSpec((tm, LANES), lambda i: (i, 0))] * 3,
        out_shape=[out] * 3,
        compiler_params=_params(1),
        name="rope_tables",
    )(pos, inv)


def _gdn_in_kernel(x_ref, xh_ref, w_ref, wab_ref, cw_ref, alog_ref, dtb_ref,
                   qkv_ref, gb_ref, xb_ref, *, seq_tiles, n_qk, n_v):
    _stage_rows(x_ref, xh_ref, xb_ref, pl.program_id(0), seq_tiles)
    xb = xb_ref[...]
    xt = xb[SUBLANES:, :]
    for c in range(n_qk + n_v):
        cols = slice(c * MXU_N, (c + 1) * MXU_N)
        h = _dot(xb, w_ref[:, cols])
        y = _silu(_causal_conv(h, cw_ref[:, cols], GDN_CONV))
        if c < n_qk:
            parts = []
            for j in range(MXU_N // GDN_DK):
                yh = y[:, j * GDN_DK:(j + 1) * GDN_DK]
                ss = jnp.sum(yh * yh, axis=-1, keepdims=True)
                parts.append(yh * lax.rsqrt(ss + RMS_EPS))
            y = jnp.concatenate(parts, axis=1)
        qkv_ref[:, cols] = y.astype(BF16)
    hab = _dot(xt, wab_ref[...])
    z = hab + dtb_ref[...]
    softplus = jnp.maximum(z, 0.0) + jnp.log(1.0 + jnp.exp(-jnp.abs(z)))
    g = -jnp.exp(alog_ref[...]) * softplus
    lane = lax.broadcasted_iota(jnp.int32, hab.shape, 1)
    gb_ref[...] = jnp.where(lane < GDN_HEADS, g, _sigmoid(hab))


def _gdn_in(x2, w_in, conv_w, a_log, dt_bias, tm, seq_tiles):
    n, d = x2.shape
    qk_w = 2 * GDN_HEADS * GDN_DK
    v_w = GDN_HEADS * GDN_DV
    conv_w_ = qk_w + v_w
    w_conv = w_in[:, :conv_w_].astype(BF16)
    w_ab = jnp.pad(w_in[:, conv_w_ + v_w:], ((0, 0), (0, LANES - 2 * GDN_HEADS))).astype(BF16)
    pad = (0, LANES - GDN_HEADS)
    alog = jnp.pad(a_log.astype(F32), pad).reshape(1, LANES)
    dtb = jnp.pad(dt_bias.astype(F32), pad).reshape(1, LANES)
    kern = functools.partial(_gdn_in_kernel, seq_tiles=seq_tiles, n_qk=qk_w // MXU_N,
                             n_v=v_w // MXU_N)
    row = lambda w: pl.BlockSpec((tm, w), lambda i: (i, 0))
    return pl.pallas_call(
        kern,
        grid=(n // tm,),
        in_specs=[row(d), _halo_spec(tm, d),
                  _resident_spec((d, conv_w_)),
                  _const_spec((d, LANES)),
                  _const_spec((GDN_CONV, conv_w_)), _const_spec((1, LANES)),
                  _const_spec((1, LANES))],
        out_specs=[row(conv_w_), row(LANES)],
        out_shape=[jax.ShapeDtypeStruct((n, conv_w_), BF16),
                   jax.ShapeDtypeStruct((n, LANES), F32)],
        scratch_shapes=[pltpu.VMEM((SUBLANES + tm, d), BF16)],
        compiler_params=_params(1),
        name="gdn_in",
    )(x2, x2, w_conv, w_ab, conv_w.astype(F32), alog, dtb)


def _unit_lower_inverses(a_list, eye, diag_blk, merge_masks, bdmask):
    c = GDN_CHUNK

    def bd(y):
        yy = jnp.concatenate([y, y], axis=0)
        return jnp.where(bdmask, yy, jnp.zeros_like(yy))

    n = [jnp.where(diag_blk, -a, 0.0) for a in a_list]
    p = [eye + x for x in n]
    nb = [x.astype(BF16) for x in n]
    n = [_dot(x, bd(x)) for x in nb]
    squarings = GDN_INV_BLOCK.bit_length() - 1
    for _ in range(squarings - 2):
        nb = [x.astype(BF16) for x in n]
        r = [_dot(jnp.concatenate([pi.astype(BF16), ni], axis=0), bd(ni)) for pi, ni in zip(p, nb)]
        p = [pi + ri[:c] for pi, ri in zip(p, r)]
        n = [ri[c:] for ri in r]
    p = [pi + _dot(pi.astype(BF16), bd(ni.astype(BF16))) for pi, ni in zip(p, n)]
    for off in merge_masks:
        pb = [pi.astype(BF16) for pi in p]
        t = [_dot(jnp.where(off, a, 0.0).astype(BF16), bd(pbi)) for a, pbi in zip(a_list, pb)]
        p = [pi - _dot(pbi, bd(ti.astype(BF16))) for pi, pbi, ti in zip(p, pb, t)]
    return p


def _gdn_chunk_kernel(q_ref, k_ref, v_ref, x_ref, wg_ref, gb_ref, nw_ref, wo_ref, lg_ref, lb_ref,
                      o_ref, s_ref, wq_ref, u_ref, ak_ref, gate_ref, mix_ref, *, n_sub):
    c = GDN_CHUNK
    nb = q_ref.shape[0]
    seqs = range(nb)

    @pl.when(pl.program_id(1) == 0)
    def _():
        s_ref[...] = jnp.zeros_like(s_ref)

    step = n_sub * c
    in_chunk = lax.broadcasted_iota(jnp.int32, (step, LANES), 0) % c
    gbs, gc_cols, gc_rows, egc_cols = [], [], [], []
    for bi in seqs:
        gb = gb_ref[bi]
        gc = gb
        shift = 1
        while shift < c:
            gc = gc + jnp.where(in_chunk >= shift, pltpu.roll(gc, shift, 0), 0.0)
            shift *= 2
        gbs.append(gb)
        gc_cols.append(gc)
        gc_rows.append(gc.T[:GDN_HEADS, :])
        egc_cols.append(jnp.exp(gc))
    bl = GDN_HEADS

    c2 = 2 * c
    r = lax.broadcasted_iota(jnp.int32, (c, c2), 0)
    lane = lax.broadcasted_iota(jnp.int32, (c, c2), 1)
    s = lane % c
    first = lane < c
    causal = s <= r
    strict = s < r
    eye = jnp.where(r == s, 1.0, 0.0)
    diag_blk = (r // GDN_INV_BLOCK) == (s // GDN_INV_BLOCK)
    merge_masks = []
    size = GDN_INV_BLOCK
    while size < c:
        merge_masks.append(((r // (2 * size)) == (s // (2 * size))) & ((r // size) != (s // size)))
        size *= 2
    bdmask = (lax.broadcasted_iota(jnp.int32, (c2, c2), 0) // c
              == lax.broadcasted_iota(jnp.int32, (c2, c2), 1) // c)
    kmask = (lax.broadcasted_iota(jnp.int32, (c2, 2 * GDN_DK), 0) // c
             == lax.broadcasted_iota(jnp.int32, (c2, 2 * GDN_DK), 1) // GDN_DK)
    scale = GDN_DK ** -0.5
    heads = range(GDN_HEADS)
    hcols = [slice(h * GDN_DK, (h + 1) * GDN_DK) for h in heads]
    n_hp = GDN_HEADS // 2
    pcols = [slice(hp * 2 * GDN_DK, (hp + 1) * 2 * GDN_DK) for hp in range(n_hp)]

    def pack_cols(x, rw, lane0):
        return jnp.where(first, jnp.broadcast_to(x[rw, lane0:lane0 + 1], (c, c2)),
                         jnp.broadcast_to(x[rw, lane0 + 1:lane0 + 2], (c, c2)))

    units = [(bi, sub, hp) for bi in seqs for sub in range(n_sub) for hp in range(n_hp)]
    rws = [slice(sub * c, (sub + 1) * c) for _, sub, _ in units]
    q2 = [q_ref[bi, rw, pcols[hp]] for rw, (bi, _, hp) in zip(rws, units)]
    k2 = [k_ref[bi, rw, pcols[hp]] for rw, (bi, _, hp) in zip(rws, units)]
    kbd = []
    for k in k2:
        kk = jnp.concatenate([k, k], axis=0)
        kbd.append(jnp.where(kmask, kk, jnp.zeros_like(kk)))
    qkk = [_dot_nt(jnp.concatenate([q, k], axis=0), kb) for q, k, kb in zip(q2, k2, kbd)]
    for bi in seqs:
        gate_ref[bi] = _dot(x_ref[bi].astype(BF16), wg_ref[...])
    decay = []
    for rw, (bi, _, hp) in zip(rws, units):
        gcr = jnp.concatenate([gc_rows[bi][2 * hp:2 * hp + 1, rw],
                               gc_rows[bi][2 * hp + 1:2 * hp + 2, rw]], axis=1)
        diff = jnp.where(causal, pack_cols(gc_cols[bi], rw, 2 * hp) - gcr, 0.0)
        decay.append(jnp.where(causal, jnp.exp(diff), 0.0))
    a_list = [jnp.where(strict, x[c:] * dc * pack_cols(gbs[bi], rw, bl + 2 * hp), 0.0)
              for x, dc, rw, (bi, _, hp) in zip(qkk, decay, rws, units)]
    attn = [(x[:c] * scale * dc).astype(BF16) for x, dc in zip(qkk, decay)]
    inv = _unit_lower_inverses(a_list, eye, diag_blk, merge_masks, bdmask)
    zeros = jnp.zeros((c, GDN_DV + GDN_DK), BF16)
    rhs, kfs = [], []
    for rw, (bi, _, hp) in zip(rws, units):
        both = []
        for h in (2 * hp, 2 * hp + 1):
            bcol = gbs[bi][rw, bl + h:bl + h + 1]
            kf = k_ref[bi, rw, hcols[h]].astype(F32)
            kfs.append(kf)
            both.append(jnp.concatenate([v_ref[bi, rw, hcols[h]].astype(F32) * bcol,
                                         kf * (bcol * egc_cols[bi][rw, h:h + 1])], axis=1).astype(BF16))
        rhs.append(jnp.concatenate([jnp.concatenate([both[0], zeros], axis=1),
                                    jnp.concatenate([zeros, both[1]], axis=1)], axis=0))
    sol = [_dot(t.astype(BF16), x) for t, x in zip(inv, rhs)]
    for i, (rw, (bi, sub, hp)) in enumerate(zip(rws, units)):
        kds = []
        for j, h in enumerate((2 * hp, 2 * hp + 1)):
            base = j * (GDN_DV + GDN_DK)
            g_last = gc_cols[bi][(sub + 1) * c - 1:(sub + 1) * c, h:h + 1]
            u_ref[bi, sub, h] = sol[i][:, base:base + GDN_DV]
            qd = (q_ref[bi, rw, hcols[h]].astype(F32) * (egc_cols[bi][rw, h:h + 1] * scale)).astype(BF16)
            wq_ref[bi, sub, h] = jnp.concatenate(
                [sol[i][:, base + GDN_DV:base + GDN_DV + GDN_DK].astype(BF16), qd], axis=0)
            kds.append(kfs[2 * i + j] * jnp.exp(g_last - gc_cols[bi][rw, h:h + 1]))
        kdt = jnp.concatenate(kds, axis=0).T
        ak_ref[bi, sub, hp] = jnp.concatenate([attn[i], kdt.astype(BF16)], axis=0)

    nw = nw_ref[...]
    zv = jnp.zeros((c, GDN_DV), BF16)
    proj_every = n_sub // 2 if n_sub % 2 == 0 else n_sub
    bh = [(bi, h) for bi in seqs for h in heads]
    for sub in range(n_sub):
        rows = slice(sub * c, (sub + 1) * c)
        st = [s_ref[bi, h] for bi, h in bh]
        r1 = [_dot(wq_ref[bi, sub, h], x.astype(BF16)) for (bi, h), x in zip(bh, st)]
        vb = [(u_ref[bi, sub, h] - y[:c]).astype(BF16) for (bi, h), y in zip(bh, r1)]
        r2 = []
        for i in range(0, len(bh), 2):
            bi, h = bh[i]
            w2 = jnp.concatenate([jnp.concatenate([vb[i], zv], axis=1),
                                  jnp.concatenate([zv, vb[i + 1]], axis=1)], axis=0)
            pair = _dot(ak_ref[bi, sub, h // 2], w2)
            r2 += [pair[:, :GDN_DV], pair[:, GDN_DV:]]
        for i, (bi, h) in enumerate(bh):
            g_last = gc_cols[bi][(sub + 1) * c - 1:(sub + 1) * c, h:h + 1]
            s_ref[bi, h] = st[i] * jnp.exp(g_last) + r2[i][c:]
            o = r1[i][c:] + r2[i][:c]
            ms = jnp.mean(o * o, axis=-1, keepdims=True)
            on = o * lax.rsqrt(ms + RMS_EPS) * nw
            gt = gate_ref[bi, rows, hcols[h]]
            mix_ref[bi, rows, hcols[h]] = (on * _silu(gt)).astype(BF16)
        if (sub + 1) % proj_every == 0:
            done = slice((sub + 1 - proj_every) * c, (sub + 1) * c)
            for bi in seqs:
                z = DEEPNORM_ALPHA * x_ref[bi, done, :] + _dot(mix_ref[bi, done, :], wo_ref[...])
                o_ref[bi, done, :] = _layer_norm_rows(z, lg_ref[...], lb_ref[...])


def _gdn_chunk(qkv, x2, w_gate, gb, norm_w, w_out, ln_g, ln_b, b, t):
    hq = GDN_HEADS * GDN_DK
    hv = GDN_HEADS * GDN_DV
    d = x2.shape[1]
    step = GDN_STEP if t % GDN_STEP == 0 else t
    nb = GDN_SEQS if b % GDN_SEQS == 0 else 1
    qkv3 = qkv.reshape(b, t, 2 * hq + hv)
    x3 = x2.reshape(b, t, d)
    gb3 = gb.reshape(b, t, LANES)
    n_sub = step // GDN_CHUNK
    kern = functools.partial(_gdn_chunk_kernel, n_sub=n_sub)
    col = lambda j: pl.BlockSpec((nb, step, hq), lambda bi, ci: (bi, ci, j))
    out = pl.pallas_call(
        kern,
        grid=(b // nb, t // step),
        in_specs=[col(0), col(1), col(2),
                  pl.BlockSpec((nb, step, d), lambda bi, ci: (bi, ci, 0)), _resident_spec((d, hv)),
                  pl.BlockSpec((nb, step, LANES), lambda bi, ci: (bi, ci, 0)),
                  _const_spec((1, GDN_DV)), _resident_spec((hv, d)), _const_spec((1, d)),
                  _const_spec((1, d))],
        out_specs=pl.BlockSpec((nb, step, d), lambda bi, ci: (bi, ci, 0)),
        out_shape=jax.ShapeDtypeStruct((b, t, d), F32),
        scratch_shapes=[pltpu.VMEM((nb, GDN_HEADS, GDN_DK, GDN_DV), F32),
                        pltpu.VMEM((nb, n_sub, GDN_HEADS, 2 * GDN_CHUNK, GDN_DK), BF16),
                        pltpu.VMEM((nb, n_sub, GDN_HEADS, GDN_CHUNK, GDN_DV), F32),
                        pltpu.VMEM((nb, n_sub, GDN_HEADS // 2, GDN_CHUNK + GDN_DK, 2 * GDN_CHUNK), BF16),
                        pltpu.VMEM((nb, step, hv), F32),
                        pltpu.VMEM((nb, step, hv), BF16)],
        compiler_params=_params(2),
        name="gdn_chunk",
    )(qkv3, qkv3, qkv3, x3, w_gate.astype(BF16), gb3, norm_w.astype(F32).reshape(1, GDN_DV),
      w_out.astype(BF16), ln_g.astype(F32).reshape(1, d), ln_b.astype(F32).reshape(1, d))
    return out.reshape(b * t, d)


def _proj_ln_kernel(y_ref, w_ref, x_ref, g_ref, b_ref, o_ref):
    m = _dot(y_ref[...], w_ref[...])
    z = DEEPNORM_ALPHA * x_ref[...] + m
    o_ref[...] = _layer_norm_rows(z, g_ref[...], b_ref[...])


def _proj_ln(y, w, x2, g, b, tm):
    n, d = x2.shape
    k = y.shape[1]
    row = lambda w_: pl.BlockSpec((tm, w_), lambda i: (i, 0))
    return pl.pallas_call(
        _proj_ln_kernel,
        grid=(n // tm,),
        in_specs=[row(k), _resident_spec((k, d)), row(d), _const_spec((1, d)), _const_spec((1, d))],
        out_specs=row(d),
        out_shape=jax.ShapeDtypeStruct((n, d), F32),
        compiler_params=_params(1),
        name="proj_ln",
    )(y, w.astype(BF16), x2, g.astype(F32).reshape(1, d), b.astype(F32).reshape(1, d))


def _ffn_kernel(x_ref, xh_ref, wup_ref, cw_ref, cb_ref, wdn_ref, g_ref, b_ref, o_ref,
                xb_ref, act_ref, *, seq_tiles, hidden):
    _stage_rows(x_ref, xh_ref, xb_ref, pl.program_id(0), seq_tiles)
    xb = xb_ref[...]
    n_chunks = hidden // MXU_N

    def project(c):
        return [_dot(xb, wup_ref[:, base + c * MXU_N:base + (c + 1) * MXU_N]) for base in (0, hidden)]

    def finish(c, hs):
        halves = []
        for base, h in zip((0, hidden), hs):
            cols = slice(base + c * MXU_N, base + (c + 1) * MXU_N)
            halves.append(_causal_conv(h, cw_ref[:, cols], FFN_CONV) + cb_ref[:, cols])
        act_ref[:, c * MXU_N:(c + 1) * MXU_N] = (_silu(halves[0]) * halves[1]).astype(BF16)

    for c in range(n_chunks):
        finish(c, project(c))
    f = _dot(act_ref[...], wdn_ref[...])
    z = DEEPNORM_ALPHA * x_ref[...] + f
    o_ref[...] = _layer_norm_rows(z, g_ref[...], b_ref[...])


def _ffn(x2, w_up, conv_w, conv_b, w_down, g, b, tm, seq_tiles):
    n, d = x2.shape
    hidden = w_down.shape[0]
    kern = functools.partial(_ffn_kernel, seq_tiles=seq_tiles, hidden=hidden)
    row = pl.BlockSpec((tm, d), lambda i: (i, 0))
    return pl.pallas_call(
        kern,
        grid=(n // tm,),
        in_specs=[row, _halo_spec(tm, d), _resident_spec((d, 2 * hidden)),
                  _const_spec((FFN_CONV, 2 * hidden)), _const_spec((1, 2 * hidden)),
                  _resident_spec((hidden, d)), _const_spec((1, d)), _const_spec((1, d))],
        out_specs=row,
        out_shape=jax.ShapeDtypeStruct((n, d), F32),
        scratch_shapes=[pltpu.VMEM((SUBLANES + tm, d), BF16), pltpu.VMEM((tm, hidden), BF16)],
        compiler_params=_params(1),
        name="conv_ffn",
    )(x2, x2, w_up.astype(BF16), conv_w.astype(F32), conv_b.astype(F32).reshape(1, 2 * hidden),
      w_down.astype(BF16), g.astype(F32).reshape(1, d), b.astype(F32).reshape(1, d))


def _diff_in_kernel(x_ref, w_ref, c_ref, s1_ref, s2_ref, qk_ref, v_ref, *, n_qk, n_v, q_cols):
    xb = x_ref[...].astype(BF16)
    cos = c_ref[...]
    s1 = s1_ref[...]
    s2 = s2_ref[...]
    half = DIFF_HD // 2
    for c in range(n_qk):
        h = _dot(xb, w_ref[:, c * MXU_N:(c + 1) * MXU_N])
        for j in range(MXU_N // LANES):
            hs = h[:, j * LANES:(j + 1) * LANES]
            rot = hs * cos + pltpu.roll(hs, LANES - half, 1) * s1 + pltpu.roll(hs, half, 1) * s2
            lo = c * MXU_N + j * LANES
            if lo < q_cols:
                rot = rot * (DIFF_HD ** -0.5 * math.log2(math.e))
            qk_ref[:, lo:lo + LANES] = rot.astype(BF16)
    base = n_qk * MXU_N
    for c in range(n_v):
        v_ref[:, c * MXU_N:(c + 1) * MXU_N] = _dot(
            xb, w_ref[:, base + c * MXU_N:base + (c + 1) * MXU_N]).astype(BF16)


def _diff_in(x2, w_in, tables, tm):
    n, d = x2.shape
    hw = DIFF_HEADS * 2 * DIFF_HD
    kern = functools.partial(_diff_in_kernel, n_qk=2 * hw // MXU_N, n_v=hw // MXU_N, q_cols=hw)
    row = lambda w: pl.BlockSpec((tm, w), lambda i: (i, 0))
    return pl.pallas_call(
        kern,
        grid=(n // tm,),
        in_specs=[row(d), _resident_spec((d, 3 * hw)), row(LANES), row(LANES), row(LANES)],
        out_specs=[row(2 * hw), row(hw)],
        out_shape=[jax.ShapeDtypeStruct((n, 2 * hw), BF16), jax.ShapeDtypeStruct((n, hw), BF16)],
        compiler_params=_params(1),
        name="diff_in",
    )(x2, w_in.astype(BF16), *tables)


def _diff_attn_kernel(q_ref, k_ref, v_ref, lq1_ref, lk1_ref, lq2_ref, lk2_ref, sw_ref, o_ref,
                      *, t, blk, n_heads, lambda_init):
    lam = (jnp.exp(jnp.sum(lq1_ref[...] * lk1_ref[...], axis=-1, keepdims=True))
           - jnp.exp(jnp.sum(lq2_ref[...] * lk2_ref[...], axis=-1, keepdims=True))
           + lambda_init)
    lane = lax.broadcasted_iota(jnp.int32, (blk, 2 * DIFF_HD), 1)
    first_map = lane < DIFF_HD
    r = lax.broadcasted_iota(jnp.int32, (2 * blk, blk), 0)
    s = lax.broadcasted_iota(jnp.int32, (2 * blk, blk), 1)
    diag_ok = s <= (r % blk)
    sw = sw_ref[...]

    def scores(hh, qi):
        cols = slice(hh * 2 * DIFF_HD, (hh + 1) * 2 * DIFF_HD)
        q = q_ref[0, qi * blk:(qi + 1) * blk, cols]
        zero = jnp.zeros_like(q)
        qs = jnp.concatenate([jnp.where(first_map, q, zero), jnp.where(first_map, zero, q)], axis=0)
        d0 = qi * blk
        sc_d = jnp.where(diag_ok, _dot_nt(qs, k_ref[0, d0:d0 + blk, cols]), -jnp.inf)
        sc_f = _dot_nt(qs, k_ref[0, 0:d0, cols]) if qi > 0 else None
        return sc_d, sc_f

    def softmax(qi, sc_d, sc_f):
        m = jnp.max(sc_d, axis=-1, keepdims=True)
        if qi > 0:
            m = jnp.maximum(m, jnp.max(sc_f, axis=-1, keepdims=True))
        p_d = jnp.exp2(sc_d - m)
        l = jnp.sum(p_d, axis=-1, keepdims=True)
        p_f = None
        if qi > 0:
            p_f = jnp.exp2(sc_f - m)
            l = l + jnp.sum(p_f, axis=-1, keepdims=True)
            p_f = p_f.astype(BF16)
        return p_d.astype(BF16), p_f, l

    def weighted_values(hh, qi, p_d, p_f, l):
        cols = slice(hh * 2 * DIFF_HD, (hh + 1) * 2 * DIFF_HD)
        d0 = qi * blk
        acc = _dot(p_d, v_ref[0, d0:d0 + blk, cols])
        if qi > 0:
            acc = acc + _dot(p_f, v_ref[0, 0:d0, cols])
        on = acc / l
        o = on[:blk] - lam * on[blk:]
        ms = jnp.mean(o * o, axis=-1, keepdims=True)
        o = o * lax.rsqrt(ms + RMS_EPS) * sw * (1.0 - lambda_init)
        o_ref[0, qi * blk:(qi + 1) * blk, cols] = o.astype(BF16)

    n_q = t // blk
    order = [(hh, qi) for qi in range(n_q - 1, -1, -1) for hh in range(n_heads)]
    sc = {}
    for step in range(len(order) + ATTN_LEAD):
        if step < len(order):
            sc[order[step]] = scores(*order[step])
        if step >= ATTN_LEAD:
            hh, qi = order[step - ATTN_LEAD]
            weighted_values(hh, qi, *softmax(qi, *sc.pop((hh, qi))))


def _diff_attn(qk, v, lam_q1, lam_k1, lam_q2, lam_k2, subln_w, lambda_init, b, t):
    hw = DIFF_HEADS * 2 * DIFF_HD
    hd2 = 2 * DIFF_HD
    blk = ATTN_BLOCK if t % ATTN_BLOCK == 0 else t
    qk3 = qk.reshape(b, t, 2 * hw)
    v3 = v.reshape(b, t, hw)
    g = ATTN_HEADS_PER_STEP
    kern = functools.partial(_diff_attn_kernel, t=t, blk=blk, n_heads=g, lambda_init=lambda_init)
    head = lambda off: pl.BlockSpec((1, t, g * hd2), lambda bi, hi: (bi, 0, hi + off))
    vec = lambda a: a.astype(F32).reshape(1, -1)
    out = pl.pallas_call(
        kern,
        grid=(b, DIFF_HEADS // g),
        in_specs=[head(0), head(DIFF_HEADS // g), head(0)] + [_const_spec((1, DIFF_HD))] * 4
                 + [_const_spec((1, hd2))],
        out_specs=head(0),
        out_shape=jax.ShapeDtypeStruct((b, t, hw), BF16),
        compiler_params=_params(2),
        name="diff_attn",
    )(qk3, qk3, v3, vec(lam_q1), vec(lam_k1), vec(lam_q2), vec(lam_k2), vec(subln_w))
    return out.reshape(b * t, hw)


def kernel(x, positions, gdn_w_in, gdn_conv_w, gdn_a_log, gdn_dt_bias, gdn_norm_w, gdn_w_out,
           diff_w_in, diff_lam_q1, diff_lam_k1, diff_lam_q2, diff_lam_k2, diff_subln_w, diff_w_out,
           ffn_w_up, ffn_conv_w, ffn_conv_b, ffn_w_down, ln_mix_g, ln_mix_b, ln_ffn_g, ln_ffn_b):
    b, t, d = x.shape
    tm = _row_tile(t)
    seq_tiles = t // tm
    x2 = x.reshape(b * t, d).astype(F32)
    tables = _rope_tables(positions, tm)
    for i in range(DEPTH):
        j = i // 2
        if i % 2 == 0:
            qkv, gb = _gdn_in(x2, gdn_w_in[j], gdn_conv_w[j], gdn_a_log[j], gdn_dt_bias[j],
                              tm, seq_tiles)
            gate_lo = 2 * GDN_HEADS * GDN_DK + GDN_HEADS * GDN_DV
            w_gate = gdn_w_in[j][:, gate_lo:gate_lo + GDN_HEADS * GDN_DV]
            x2 = _gdn_chunk(qkv, x2, w_gate, gb, gdn_norm_w[j], gdn_w_out[j], ln_mix_g[i], ln_mix_b[i],
                            b, t)
        else:
            lambda_init = 0.8 - 0.6 * math.exp(-0.3 * i)
            qk, v = _diff_in(x2, diff_w_in[j], tables, tm)
            mixed = _diff_attn(qk, v, diff_lam_q1[j], diff_lam_k1[j], diff_lam_q2[j], diff_lam_k2[j],
                               diff_subln_w[j], lambda_init, b, t)
            x2 = _proj_ln(mixed, diff_w_out[j], x2, ln_mix_g[i], ln_mix_b[i], tm)
        x2 = _ffn(x2, ffn_w_up[i], ffn_conv_w[i], ffn_conv_b[i], ffn_w_down[i],
                  ln_ffn_g[i], ln_ffn_b[i], tm, seq_tiles)
    return x2.reshape(b, t, d)
```

```python
import functools
import math

import jax
import jax.numpy as jnp
from jax import lax
from jax.experimental import pallas as pl
from jax.experimental.pallas import tpu as pltpu

F32 = jnp.float32
BF16 = jnp.bfloat16

DEPTH = 4
GDN_HEADS = 8
GDN_DK = 128
GDN_DV = 128
GDN_CONV = 4
GDN_CHUNK = 64
DIFF_HEADS = 8
DIFF_HD = 64
ROPE_THETA = 10000.0
FFN_CONV = 3
DEEPNORM_ALPHA = (2.0 * DEPTH) ** 0.25
LN_EPS = 1e-5
RMS_EPS = 1e-6

LANES = 128
SUBLANES = 8
MXU_N = 256
VMEM_LIMIT = 56 * 1024 * 1024

ROW_TILE = 1024
ATTN_HEADS_PER_STEP = 2
ATTN_LEAD = 1
ATTN_BLOCK = 256
GDN_STEP = 256
GDN_SEQS = 2
GDN_INV_BLOCK = 16


def _row_tile(t):
    return ROW_TILE if t % ROW_TILE == 0 else t


def _sigmoid(x):
    return 1.0 / (1.0 + jnp.exp2(x * -math.log2(math.e)))


def _silu(x):
    return x * _sigmoid(x)


def _dot(a, b):
    return jnp.dot(a, b, preferred_element_type=F32)


def _dot_nt(a, b):
    return lax.dot_general(a, b, (((1,), (1,)), ((), ())), preferred_element_type=F32)


def _shift_rows(h, s):
    return pltpu.roll(h, s, 0)


def _causal_conv(h, cw, width):
    y = h * cw[width - 1:width, :]
    for s in range(1, width):
        y = y + _shift_rows(h, s) * cw[width - 1 - s:width - s, :]
    return y[SUBLANES:, :]


def _layer_norm_rows(z, g, b):
    mu = jnp.mean(z, axis=-1, keepdims=True)
    zc = z - mu
    var = jnp.mean(zc * zc, axis=-1, keepdims=True)
    return zc * lax.rsqrt(var + LN_EPS) * g + b


def _stage_rows(x_ref, xh_ref, xb_ref, tile, seq_tiles):
    first = (tile % seq_tiles) == 0
    halo = jnp.where(first, 0.0, xh_ref[...])
    xb_ref[0:SUBLANES, :] = halo.astype(BF16)
    xb_ref[SUBLANES:, :] = x_ref[...].astype(BF16)


def _halo_spec(tm, d):
    return pl.BlockSpec((SUBLANES, d), lambda i: (jnp.maximum(i * (tm // SUBLANES) - 1, 0), 0))


def _const_spec(shape):
    return pl.BlockSpec(shape, lambda *_: (0,) * len(shape))


def _resident_spec(shape):
    return pl.BlockSpec(shape, lambda *_: (0,) * len(shape), pipeline_mode=pl.Buffered(1))


def _params(n_axes):
    return pltpu.CompilerParams(
        dimension_semantics=("arbitrary",) * n_axes, vmem_limit_bytes=VMEM_LIMIT)


def _rope_values(pos, inv):
    tm = pos.shape[0]
    nf = DIFF_HD // 2
    groups = LANES // nf
    blk = tm // groups
    lane = lax.broadcasted_iota(jnp.int32, (blk, LANES), 1)
    pos = pos.astype(F32)
    packed = jnp.zeros((blk, LANES), F32)
    for j in range(groups):
        packed = jnp.where(lane // nf == j, jnp.broadcast_to(pos[j * blk:(j + 1) * blk, :], (blk, LANES)),
                           packed)
    ang = packed * inv
    cos = jnp.cos(ang)
    sin = jnp.sin(ang)
    lower = (lane % DIFF_HD) < nf
    cs, s1s, s2s = [], [], []
    for j in range(groups):
        cj, sj = cos, sin
        for g in range(groups):
            shift = ((g - j) % groups) * nf
            if shift:
                cj = jnp.where(lane // nf == g, pltpu.roll(cos, shift, 1), cj)
                sj = jnp.where(lane // nf == g, pltpu.roll(sin, shift, 1), sj)
        cs.append(cj)
        s1s.append(jnp.where(lower, -sj, 0.0))
        s2s.append(jnp.where(lower, 0.0, sj))
    return (jnp.concatenate(cs, axis=0), jnp.concatenate(s1s, axis=0), jnp.concatenate(s2s, axis=0))


def _gdn_in_kernel(x_ref, xh_ref, w_ref, wab_ref, cw_ref, alog_ref, dtb_ref,
                   qkv_ref, gb_ref, xb_ref, *, seq_tiles, n_qk, n_v):
    _stage_rows(x_ref, xh_ref, xb_ref, pl.program_id(0), seq_tiles)
    xb = xb_ref[...]
    xt = xb[SUBLANES:, :]
    for c in range(n_qk + n_v):
        cols = slice(c * MXU_N, (c + 1) * MXU_N)
        h = _dot(xb, w_ref[:, cols])
        y = _silu(_causal_conv(h, cw_ref[:, cols], GDN_CONV))
        if c < n_qk:
            parts = []
            for j in range(MXU_N // GDN_DK):
                yh = y[:, j * GDN_DK:(j + 1) * GDN_DK]
                ss = jnp.sum(yh * yh, axis=-1, keepdims=True)
                parts.append(yh * lax.rsqrt(ss + RMS_EPS))
            y = jnp.concatenate(parts, axis=1)
        qkv_ref[:, cols] = y.astype(BF16)
    hab = _dot(xt, wab_ref[...])
    z = hab + dtb_ref[...]
    softplus = jnp.maximum(z, 0.0) + jnp.log(1.0 + jnp.exp(-jnp.abs(z)))
    g = -jnp.exp(alog_ref[...]) * softplus
    lane = lax.broadcasted_iota(jnp.int32, hab.shape, 1)
    gb_ref[...] = jnp.where(lane < GDN_HEADS, g, _sigmoid(hab))


def _gdn_in(x2, w_in, conv_w, a_log, dt_bias, tm, seq_tiles):
    n, d = x2.shape
    qk_w = 2 * GDN_HEADS * GDN_DK
    v_w = GDN_HEADS * GDN_DV
    conv_w_ = qk_w + v_w
    w_conv = w_in[:, :conv_w_].astype(BF16)
    w_ab = jnp.pad(w_in[:, conv_w_ + v_w:], ((0, 0), (0, LANES - 2 * GDN_HEADS))).astype(BF16)
    pad = (0, LANES - GDN_HEADS)
    alog = jnp.pad(a_log.astype(F32), pad).reshape(1, LANES)
    dtb = jnp.pad(dt_bias.astype(F32), pad).reshape(1, LANES)
    kern = functools.partial(_gdn_in_kernel, seq_tiles=seq_tiles, n_qk=qk_w // MXU_N,
                             n_v=v_w // MXU_N)
    row = lambda w: pl.BlockSpec((tm, w), lambda i: (i, 0))
    return pl.pallas_call(
        kern,
        grid=(n // tm,),
        in_specs=[row(d), _halo_spec(tm, d),
                  _resident_spec((d, conv_w_)),
                  _const_spec((d, LANES)),
                  _const_spec((GDN_CONV, conv_w_)), _const_spec((1, LANES)),
                  _const_spec((1, LANES))],
        out_specs=[row(conv_w_), row(LANES)],
        out_shape=[jax.ShapeDtypeStruct((n, conv_w_), BF16),
                   jax.ShapeDtypeStruct((n, LANES), F32)],
        scratch_shapes=[pltpu.VMEM((SUBLANES + tm, d), BF16)],
        compiler_params=_params(1),
        name="gdn_in",
    )(x2, x2, w_conv, w_ab, conv_w.astype(F32), alog, dtb)


def _unit_lower_inverses(a_list, eye, diag_blk, merge_masks, bdmask):
    c = GDN_CHUNK

    def bd(y):
        yy = jnp.concatenate([y, y], axis=0)
        return jnp.where(bdmask, yy, jnp.zeros_like(yy))

    n = [jnp.where(diag_blk, -a, 0.0) for a in a_list]
    p = [eye + x for x in n]
    nb = [x.astype(BF16) for x in n]
    n = [_dot(x, bd(x)) for x in nb]
    squarings = GDN_INV_BLOCK.bit_length() - 1
    for _ in range(squarings - 2):
        nb = [x.astype(BF16) for x in n]
        r = [_dot(jnp.concatenate([pi.astype(BF16), ni], axis=0), bd(ni)) for pi, ni in zip(p, nb)]
        p = [pi + ri[:c] for pi, ri in zip(p, r)]
        n = [ri[c:] for ri in r]
    p = [pi + _dot(pi.astype(BF16), bd(ni.astype(BF16))) for pi, ni in zip(p, n)]
    for off in merge_masks:
        pb = [pi.astype(BF16) for pi in p]
        t = [_dot(jnp.where(off, a, 0.0).astype(BF16), bd(pbi)) for a, pbi in zip(a_list, pb)]
        p = [pi - _dot(pbi, bd(ti.astype(BF16))) for pi, pbi, ti in zip(p, pb, t)]
    return p


def _gdn_chunk_kernel(q_ref, k_ref, v_ref, x_ref, wg_ref, gb_ref, nw_ref, wo_ref, lg_ref, lb_ref,
                      o_ref, s_ref, wq_ref, u_ref, ak_ref, gate_ref, mix_ref, *, n_sub):
    c = GDN_CHUNK
    nb = q_ref.shape[0]
    seqs = range(nb)

    @pl.when(pl.program_id(1) == 0)
    def _():
        s_ref[...] = jnp.zeros_like(s_ref)

    step = n_sub * c
    in_chunk = lax.broadcasted_iota(jnp.int32, (step, LANES), 0) % c
    gbs, gc_cols, gc_rows, egc_cols = [], [], [], []
    for bi in seqs:
        gb = gb_ref[bi]
        gc = gb
        shift = 1
        while shift < c:
            gc = gc + jnp.where(in_chunk >= shift, pltpu.roll(gc, shift, 0), 0.0)
            shift *= 2
        gbs.append(gb)
        gc_cols.append(gc)
        gc_rows.append(gc.T[:GDN_HEADS, :])
        egc_cols.append(jnp.exp(gc))
    bl = GDN_HEADS

    c2 = 2 * c
    r = lax.broadcasted_iota(jnp.int32, (c, c2), 0)
    lane = lax.broadcasted_iota(jnp.int32, (c, c2), 1)
    s = lane % c
    first = lane < c
    causal = s <= r
    strict = s < r
    eye = jnp.where(r == s, 1.0, 0.0)
    diag_blk = (r // GDN_INV_BLOCK) == (s // GDN_INV_BLOCK)
    merge_masks = []
    size = GDN_INV_BLOCK
    while size < c:
        merge_masks.append(((r // (2 * size)) == (s // (2 * size))) & ((r // size) != (s // size)))
        size *= 2
    bdmask = (lax.broadcasted_iota(jnp.int32, (c2, c2), 0) // c
              == lax.broadcasted_iota(jnp.int32, (c2, c2), 1) // c)
    kmask = (lax.broadcasted_iota(jnp.int32, (c2, 2 * GDN_DK), 0) // c
             == lax.broadcasted_iota(jnp.int32, (c2, 2 * GDN_DK), 1) // GDN_DK)
    scale = GDN_DK ** -0.5
    heads = range(GDN_HEADS)
    hcols = [slice(h * GDN_DK, (h + 1) * GDN_DK) for h in heads]
    n_hp = GDN_HEADS // 2
    pcols = [slice(hp * 2 * GDN_DK, (hp + 1) * 2 * GDN_DK) for hp in range(n_hp)]

    def pack_cols(x, rw, lane0):
        return jnp.where(first, jnp.broadcast_to(x[rw, lane0:lane0 + 1], (c, c2)),
                         jnp.broadcast_to(x[rw, lane0 + 1:lane0 + 2], (c, c2)))

    units = [(bi, sub, hp) for bi in seqs for sub in range(n_sub) for hp in range(n_hp)]
    rws = [slice(sub * c, (sub + 1) * c) for _, sub, _ in units]
    q2 = [q_ref[bi, rw, pcols[hp]] for rw, (bi, _, hp) in zip(rws, units)]
    k2 = [k_ref[bi, rw, pcols[hp]] for rw, (bi, _, hp) in zip(rws, units)]
    kbd = []
    for k in k2:
        kk = jnp.concatenate([k, k], axis=0)
        kbd.append(jnp.where(kmask, kk, jnp.zeros_like(kk)))
    qkk = [_dot_nt(jnp.concatenate([q, k], axis=0), kb) for q, k, kb in zip(q2, k2, kbd)]
    for bi in seqs:
        gate_ref[bi] = _dot(x_ref[bi].astype(BF16), wg_ref[...])
    decay = []
    for rw, (bi, _, hp) in zip(rws, units):
        gcr = jnp.concatenate([gc_rows[bi][2 * hp:2 * hp + 1, rw],
                               gc_rows[bi][2 * hp + 1:2 * hp + 2, rw]], axis=1)
        diff = jnp.where(causal, pack_cols(gc_cols[bi], rw, 2 * hp) - gcr, 0.0)
        decay.append(jnp.where(causal, jnp.exp(diff), 0.0))
    a_list = [jnp.where(strict, x[c:] * dc * pack_cols(gbs[bi], rw, bl + 2 * hp), 0.0)
              for x, dc, rw, (bi, _, hp) in zip(qkk, decay, rws, units)]
    attn = [(x[:c] * scale * dc).astype(BF16) for x, dc in zip(qkk, decay)]
    inv = _unit_lower_inverses(a_list, eye, diag_blk, merge_masks, bdmask)
    zeros = jnp.zeros((c, GDN_DV + GDN_DK), BF16)
    rhs, kfs = [], []
    for rw, (bi, _, hp) in zip(rws, units):
        both = []
        for h in (2 * hp, 2 * hp + 1):
            bcol = gbs[bi][rw, bl + h:bl + h + 1]
            kf = k_ref[bi, rw, hcols[h]].astype(F32)
            kfs.append(kf)
            both.append(jnp.concatenate([v_ref[bi, rw, hcols[h]].astype(F32) * bcol,
                                         kf * (bcol * egc_cols[bi][rw, h:h + 1])], axis=1).astype(BF16))
        rhs.append(jnp.concatenate([jnp.concatenate([both[0], zeros], axis=1),
                                    jnp.concatenate([zeros, both[1]], axis=1)], axis=0))
    sol = [_dot(t.astype(BF16), x) for t, x in zip(inv, rhs)]
    for i, (rw, (bi, sub, hp)) in enumerate(zip(rws, units)):
        kds = []
        for j, h in enumerate((2 * hp, 2 * hp + 1)):
            base = j * (GDN_DV + GDN_DK)
            g_last = gc_cols[bi][(sub + 1) * c - 1:(sub + 1) * c, h:h + 1]
            u_ref[bi, sub, h] = sol[i][:, base:base + GDN_DV]
            qd = (q_ref[bi, rw, hcols[h]].astype(F32) * (egc_cols[bi][rw, h:h + 1] * scale)).astype(BF16)
            wq_ref[bi, sub, h] = jnp.concatenate(
                [sol[i][:, base + GDN_DV:base + GDN_DV + GDN_DK].astype(BF16), qd], axis=0)
            kds.append(kfs[2 * i + j] * jnp.exp(g_last - gc_cols[bi][rw, h:h + 1]))
        kdt = jnp.concatenate(kds, axis=0).T
        ak_ref[bi, sub, hp] = jnp.concatenate([attn[i], kdt.astype(BF16)], axis=0)

    nw = nw_ref[...]
    zv = jnp.zeros((c, GDN_DV), BF16)
    proj_every = n_sub // 2 if n_sub % 2 == 0 else n_sub
    bh = [(bi, h) for bi in seqs for h in heads]
    for sub in range(n_sub):
        rows = slice(sub * c, (sub + 1) * c)
        st = [s_ref[bi, h] for bi, h in bh]
        r1 = [_dot(wq_ref[bi, sub, h], x.astype(BF16)) for (bi, h), x in zip(bh, st)]
        vb = [(u_ref[bi, sub, h] - y[:c]).astype(BF16) for (bi, h), y in zip(bh, r1)]
        r2 = []
        for i in range(0, len(bh), 2):
            bi, h = bh[i]
            w2 = jnp.concatenate([jnp.concatenate([vb[i], zv], axis=1),
                                  jnp.concatenate([zv, vb[i + 1]], axis=1)], axis=0)
            pair = _dot(ak_ref[bi, sub, h // 2], w2)
            r2 += [pair[:, :GDN_DV], pair[:, GDN_DV:]]
        for i, (bi, h) in enumerate(bh):
            g_last = gc_cols[bi][(sub + 1) * c - 1:(sub + 1) * c, h:h + 1]
            s_ref[bi, h] = st[i] * jnp.exp(g_last) + r2[i][c:]
            o = r1[i][c:] + r2[i][:c]
            ms = jnp.mean(o * o, axis=-1, keepdims=True)
            on = o * lax.rsqrt(ms + RMS_EPS) * nw
            gt = gate_ref[bi, rows, hcols[h]]
            mix_ref[bi, rows, hcols[h]] = (on * _silu(gt)).astype(BF16)
        if (sub + 1) % proj_every == 0:
            done = slice((sub + 1 - proj_every) * c, (sub + 1) * c)
            for bi in seqs:
                z = DEEPNORM_ALPHA * x_ref[bi, done, :] + _dot(mix_ref[bi, done, :], wo_ref[...])
                o_ref[bi, done, :] = _layer_norm_rows(z, lg_ref[...], lb_ref[...])


def _gdn_chunk(qkv, x2, w_gate, gb, norm_w, w_out, ln_g, ln_b, b, t):
    hq = GDN_HEADS * GDN_DK
    hv = GDN_HEADS * GDN_DV
    d = x2.shape[1]
    step = GDN_STEP if t % GDN_STEP == 0 else t
    nb = GDN_SEQS if b % GDN_SEQS == 0 else 1
    qkv3 = qkv.reshape(b, t, 2 * hq + hv)
    x3 = x2.reshape(b, t, d)
    gb3 = gb.reshape(b, t, LANES)
    n_sub = step // GDN_CHUNK
    kern = functools.partial(_gdn_chunk_kernel, n_sub=n_sub)
    col = lambda j: pl.BlockSpec((nb, step, hq), lambda bi, ci: (bi, ci, j))
    out = pl.pallas_call(
        kern,
        grid=(b // nb, t // step),
        in_specs=[col(0), col(1), col(2),
                  pl.BlockSpec((nb, step, d), lambda bi, ci: (bi, ci, 0)), _resident_spec((d, hv)),
                  pl.BlockSpec((nb, step, LANES), lambda bi, ci: (bi, ci, 0)),
                  _const_spec((1, GDN_DV)), _resident_spec((hv, d)), _const_spec((1, d)),
                  _const_spec((1, d))],
        out_specs=pl.BlockSpec((nb, step, d), lambda bi, ci: (bi, ci, 0)),
        out_shape=jax.ShapeDtypeStruct((b, t, d), F32),
        scratch_shapes=[pltpu.VMEM((nb, GDN_HEADS, GDN_DK, GDN_DV), F32),
                        pltpu.VMEM((nb, n_sub, GDN_HEADS, 2 * GDN_CHUNK, GDN_DK), BF16),
                        pltpu.VMEM((nb, n_sub, GDN_HEADS, GDN_CHUNK, GDN_DV), F32),
                        pltpu.VMEM((nb, n_sub, GDN_HEADS // 2, GDN_CHUNK + GDN_DK, 2 * GDN_CHUNK), BF16),
                        pltpu.VMEM((nb, step, hv), F32),
                        pltpu.VMEM((nb, step, hv), BF16)],
        compiler_params=_params(2),
        name="gdn_chunk",
    )(qkv3, qkv3, qkv3, x3, w_gate.astype(BF16), gb3, norm_w.astype(F32).reshape(1, GDN_DV),
      w_out.astype(BF16), ln_g.astype(F32).reshape(1, d), ln_b.astype(F32).reshape(1, d))
    return out.reshape(b * t, d)


def _proj_ln_kernel(y_ref, w_ref, x_ref, g_ref, b_ref, o_ref):
    m = _dot(y_ref[...], w_ref[...])
    z = DEEPNORM_ALPHA * x_ref[...] + m
    o_ref[...] = _layer_norm_rows(z, g_ref[...], b_ref[...])


def _proj_ln(y, w, x2, g, b, tm):
    n, d = x2.shape
    k = y.shape[1]
    row = lambda w_: pl.BlockSpec((tm, w_), lambda i: (i, 0))
    return pl.pallas_call(
        _proj_ln_kernel,
        grid=(n // tm,),
        in_specs=[row(k), _resident_spec((k, d)), row(d), _const_spec((1, d)), _const_spec((1, d))],
        out_specs=row(d),
        out_shape=jax.ShapeDtypeStruct((n, d), F32),
        compiler_params=_params(1),
        name="proj_ln",
    )(y, w.astype(BF16), x2, g.astype(F32).reshape(1, d), b.astype(F32).reshape(1, d))


def _ffn_kernel(x_ref, xh_ref, wup_ref, cw_ref, cb_ref, wdn_ref, g_ref, b_ref, o_ref,
                xb_ref, act_ref, *, seq_tiles, hidden):
    _stage_rows(x_ref, xh_ref, xb_ref, pl.program_id(0), seq_tiles)
    xb = xb_ref[...]
    n_chunks = hidden // MXU_N

    def project(c):
        return [_dot(xb, wup_ref[:, base + c * MXU_N:base + (c + 1) * MXU_N]) for base in (0, hidden)]

    def finish(c, hs):
        halves = []
        for base, h in zip((0, hidden), hs):
            cols = slice(base + c * MXU_N, base + (c + 1) * MXU_N)
            halves.append(_causal_conv(h, cw_ref[:, cols], FFN_CONV) + cb_ref[:, cols])
        act_ref[:, c * MXU_N:(c + 1) * MXU_N] = (_silu(halves[0]) * halves[1]).astype(BF16)

    for c in range(n_chunks):
        finish(c, project(c))
    f = _dot(act_ref[...], wdn_ref[...])
    z = DEEPNORM_ALPHA * x_ref[...] + f
    o_ref[...] = _layer_norm_rows(z, g_ref[...], b_ref[...])


def _ffn(x2, w_up, conv_w, conv_b, w_down, g, b, tm, seq_tiles):
    n, d = x2.shape
    hidden = w_down.shape[0]
    kern = functools.partial(_ffn_kernel, seq_tiles=seq_tiles, hidden=hidden)
    row = pl.BlockSpec((tm, d), lambda i: (i, 0))
    return pl.pallas_call(
        kern,
        grid=(n // tm,),
        in_specs=[row, _halo_spec(tm, d), _resident_spec((d, 2 * hidden)),
                  _const_spec((FFN_CONV, 2 * hidden)), _const_spec((1, 2 * hidden)),
                  _resident_spec((hidden, d)), _const_spec((1, d)), _const_spec((1, d))],
        out_specs=row,
        out_shape=jax.ShapeDtypeStruct((n, d), F32),
        scratch_shapes=[pltpu.VMEM((SUBLANES + tm, d), BF16), pltpu.VMEM((tm, hidden), BF16)],
        compiler_params=_params(1),
        name="conv_ffn",
    )(x2, x2, w_up.astype(BF16), conv_w.astype(F32), conv_b.astype(F32).reshape(1, 2 * hidden),
      w_down.astype(BF16), g.astype(F32).reshape(1, d), b.astype(F32).reshape(1, d))


def _diff_in_kernel(x_ref, w_ref, pos_ref, inv_ref, qk_ref, v_ref, *, n_qk, n_v, q_cols):
    xb = x_ref[...].astype(BF16)
    cos, s1, s2 = _rope_values(pos_ref[...], inv_ref[...])
    half = DIFF_HD // 2
    for c in range(n_qk):
        h = _dot(xb, w_ref[:, c * MXU_N:(c + 1) * MXU_N])
        for j in range(MXU_N // LANES):
            hs = h[:, j * LANES:(j + 1) * LANES]
            rot = hs * cos + pltpu.roll(hs, LANES - half, 1) * s1 + pltpu.roll(hs, half, 1) * s2
            lo = c * MXU_N + j * LANES
            if lo < q_cols:
                rot = rot * (DIFF_HD ** -0.5 * math.log2(math.e))
            qk_ref[:, lo:lo + LANES] = rot.astype(BF16)
    base = n_qk * MXU_N
    for c in range(n_v):
        v_ref[:, c * MXU_N:(c + 1) * MXU_N] = _dot(
            xb, w_ref[:, base + c * MXU_N:base + (c + 1) * MXU_N]).astype(BF16)


def _diff_in(x2, w_in, positions, tm):
    n, d = x2.shape
    hw = DIFF_HEADS * 2 * DIFF_HD
    inv_freq = ROPE_THETA ** (-jnp.arange(0, DIFF_HD, 2, dtype=F32) / DIFF_HD)
    inv = jnp.tile(inv_freq, LANES // (DIFF_HD // 2)).reshape(1, LANES)
    pos = positions.reshape(n, 1)
    kern = functools.partial(_diff_in_kernel, n_qk=2 * hw // MXU_N, n_v=hw // MXU_N, q_cols=hw)
    row = lambda w: pl.BlockSpec((tm, w), lambda i: (i, 0))
    return pl.pallas_call(
        kern,
        grid=(n // tm,),
        in_specs=[row(d), _resident_spec((d, 3 * hw)), row(1), _const_spec((1, LANES))],
        out_specs=[row(2 * hw), row(hw)],
        out_shape=[jax.ShapeDtypeStruct((n, 2 * hw), BF16), jax.ShapeDtypeStruct((n, hw), BF16)],
        compiler_params=_params(1),
        name="diff_in",
    )(x2, w_in.astype(BF16), pos, inv)


def _diff_attn_kernel(q_ref, k_ref, v_ref, lq1_ref, lk1_ref, lq2_ref, lk2_ref, sw_ref, o_ref,
                      *, t, blk, n_heads, lambda_init):
    lam = (jnp.exp(jnp.sum(lq1_ref[...] * lk1_ref[...], axis=-1, keepdims=True))
           - jnp.exp(jnp.sum(lq2_ref[...] * lk2_ref[...], axis=-1, keepdims=True))
           + lambda_init)
    lane = lax.broadcasted_iota(jnp.int32, (blk, 2 * DIFF_HD), 1)
    first_map = lane < DIFF_HD
    r = lax.broadcasted_iota(jnp.int32, (2 * blk, blk), 0)
    s = lax.broadcasted_iota(jnp.int32, (2 * blk, blk), 1)
    diag_ok = s <= (r % blk)
    sw = sw_ref[...]

    def scores(hh, qi):
        cols = slice(hh * 2 * DIFF_HD, (hh + 1) * 2 * DIFF_HD)
        q = q_ref[0, qi * blk:(qi + 1) * blk, cols]
        zero = jnp.zeros_like(q)
        qs = jnp.concatenate([jnp.where(first_map, q, zero), jnp.where(first_map, zero, q)], axis=0)
        d0 = qi * blk
        sc_d = jnp.where(diag_ok, _dot_nt(qs, k_ref[0, d0:d0 + blk, cols]), -jnp.inf)
        sc_f = _dot_nt(qs, k_ref[0, 0:d0, cols]) if qi > 0 else None
        return sc_d, sc_f

    def softmax(qi, sc_d, sc_f):
        m = jnp.max(sc_d, axis=-1, keepdims=True)
        if qi > 0:
            m = jnp.maximum(m, jnp.max(sc_f, axis=-1, keepdims=True))
        p_d = jnp.exp2(sc_d - m)
        l = jnp.sum(p_d, axis=-1, keepdims=True)
        p_f = None
        if qi > 0:
            p_f = jnp.exp2(sc_f - m)
            l = l + jnp.sum(p_f, axis=-1, keepdims=True)
            p_f = p_f.astype(BF16)
        return p_d.astype(BF16), p_f, l

    def weighted_values(hh, qi, p_d, p_f, l):
        cols = slice(hh * 2 * DIFF_HD, (hh + 1) * 2 * DIFF_HD)
        d0 = qi * blk
        acc = _dot(p_d, v_ref[0, d0:d0 + blk, cols])
        if qi > 0:
            acc = acc + _dot(p_f, v_ref[0, 0:d0, cols])
        on = acc / l
        o = on[:blk] - lam * on[blk:]
        ms = jnp.mean(o * o, axis=-1, keepdims=True)
        o = o * lax.rsqrt(ms + RMS_EPS) * sw * (1.0 - lambda_init)
        o_ref[0, qi * blk:(qi + 1) * blk, cols] = o.astype(BF16)

    n_q = t // blk
    order = [(hh, qi) for qi in range(n_q - 1, -1, -1) for hh in range(n_heads)]
    sc = {}
    for step in range(len(order) + ATTN_LEAD):
        if step < len(order):
            sc[order[step]] = scores(*order[step])
        if step >= ATTN_LEAD:
            hh, qi = order[step - ATTN_LEAD]
            weighted_values(hh, qi, *softmax(qi, *sc.pop((hh, qi))))


def _diff_attn(qk, v, lam_q1, lam_k1, lam_q2, lam_k2, subln_w, lambda_init, b, t):
    hw = DIFF_HEADS * 2 * DIFF_HD
    hd2 = 2 * DIFF_HD
    blk = ATTN_BLOCK if t % ATTN_BLOCK == 0 else t
    qk3 = qk.reshape(b, t, 2 * hw)
    v3 = v.reshape(b, t, hw)
    g = ATTN_HEADS_PER_STEP
    kern = functools.partial(_diff_attn_kernel, t=t, blk=blk, n_heads=g, lambda_init=lambda_init)
    head = lambda off: pl.BlockSpec((1, t, g * hd2), lambda bi, hi: (bi, 0, hi + off))
    vec = lambda a: a.astype(F32).reshape(1, -1)
    out = pl.pallas_call(
        kern,
        grid=(b, DIFF_HEADS // g),
        in_specs=[head(0), head(DIFF_HEADS // g), head(0)] + [_const_spec((1, DIFF_HD))] * 4
                 + [_const_spec((1, hd2))],
        out_specs=head(0),
        out_shape=jax.ShapeDtypeStruct((b, t, hw), BF16),
        compiler_params=_params(2),
        name="diff_attn",
    )(qk3, qk3, v3, vec(lam_q1), vec(lam_k1), vec(lam_q2), vec(lam_k2), vec(subln_w))
    return out.reshape(b * t, hw)


def kernel(x, positions, gdn_w_in, gdn_conv_w, gdn_a_log, gdn_dt_bias, gdn_norm_w, gdn_w_out,
           diff_w_in, diff_lam_q1, diff_lam_k1, diff_lam_q2, diff_lam_k2, diff_subln_w, diff_w_out,
           ffn_w_up, ffn_conv_w, ffn_conv_b, ffn_w_down, ln_mix_g, ln_mix_b, ln_ffn_g, ln_ffn_b):
    b, t, d = x.shape
    tm = _row_tile(t)
    seq_tiles = t // tm
    x2 = x.reshape(b * t, d).astype(F32)
    for i in range(DEPTH):
        j = i // 2
        if i % 2 == 0:
            qkv, gb = _gdn_in(x2, gdn_w_in[j], gdn_conv_w[j], gdn_a_log[j], gdn_dt_bias[j],
                              tm, seq_tiles)
            gate_lo = 2 * GDN_HEADS * GDN_DK + GDN_HEADS * GDN_DV
            w_gate = gdn_w_in[j][:, gate_lo:gate_lo + GDN_HEADS * GDN_DV]
            x2 = _gdn_chunk(qkv, x2, w_gate, gb, gdn_norm_w[j], gdn_w_out[j], ln_mix_g[i], ln_mix_b[i],
                            b, t)
        else:
            lambda_init = 0.8 - 0.6 * math.exp(-0.3 * i)
            qk, v = _diff_in(x2, diff_w_in[j], positions, tm)
            mixed = _diff_attn(qk, v, diff_lam_q1[j], diff_lam_k1[j], diff_lam_q2[j], diff_lam_k2[j],
                               diff_subln_w[j], lambda_init, b, t)
            x2 = _proj_ln(mixed, diff_w_out[j], x2, ln_mix_g[i], ln_mix_b[i], tm)
        x2 = _ffn(x2, ffn_w_up[i], ffn_conv_w[i], ffn_conv_b[i], ffn_w_down[i],
                  ln_ffn_g[i], ln_ffn_b[i], tm, seq_tiles)
    return x2.reshape(b, t, d)
```
